```python
import math
import jax, jax.numpy as jnp
from jax import lax
import numpy as np

D_MODEL = 2048
BATCH = 2
SEQ = 4096
DEPTH = 1
DEC_BATCH = 32
DEC_SEQ = 1
PAST_LEN = 8192
PAGE_SIZE = 128

GDN_HEADS = 16
GDN_DK = 128
GDN_DV = 128
CONV_W = 4
CONV_DIM = GDN_HEADS * (2 * GDN_DK + GDN_DV)
GDN_CHUNK = 64
NSA_HEADS = 16
NSA_KV_HEADS = 4
HEAD_DIM = 128
NSA_HPG = NSA_HEADS // NSA_KV_HEADS
CMP_LEN = 32
CMP_STRIDE = 16
SEL_BLOCK = 64
N_SEL = 16
WINDOW = 512
NSA_QBLOCK = 64
N_EXPERTS = 32
TOP_K = 4
D_FF = 2048
SWIGLU_LIMIT = 7.0
SWIGLU_ALPHA = 1.702
MOE_BLOCK = 128
EPS = 1e-6
NEG = -1e30
FORCE = 1e6

IN_SPLITS = (CONV_DIM, GDN_HEADS * GDN_DV, GDN_HEADS, GDN_HEADS, NSA_HEADS * HEAD_DIM,
             6 * NSA_KV_HEADS * HEAD_DIM, 3 * NSA_HEADS, 2 * D_MODEL)
N_IN = sum(IN_SPLITS)

kernel_name = "hybrid_gdn_nsa_moe_step"


def rmsnorm(x, g):
    x32 = x.astype(jnp.float32)
    y = x32 * lax.rsqrt(jnp.mean(x32 * x32, axis=-1, keepdims=True) + EPS)
    return (y * g.astype(jnp.float32)).astype(x.dtype)


def l2norm(x):
    return x * lax.rsqrt(jnp.sum(x * x, axis=-1, keepdims=True) + EPS)


def masked_softmax(s, mask):
    s = jnp.where(mask, s, NEG)
    p = jnp.exp(s - jnp.max(s, axis=-1, keepdims=True)) * mask
    return p / jnp.maximum(jnp.sum(p, axis=-1, keepdims=True), 1e-30)


def causal_conv(x, buf, conv_w):
    T = x.shape[1]
    ext = jnp.concatenate([buf.astype(x.dtype), x], axis=1)
    y = ext[:, 0:T] * conv_w[0]
    for w in range(1, CONV_W):
        y = y + ext[:, w:w + T] * conv_w[w]
    return y, ext[:, T:]


def gated_delta_chunked(q, k, v, g, beta, s0, chunk):
    B, T, H, DK = q.shape
    DV = v.shape[-1]
    n = T // chunk
    to_c = lambda a: a.reshape(B, n, chunk, H, a.shape[-1]).transpose(1, 0, 3, 2, 4)
    to_s = lambda a: a.reshape(B, n, chunk, H).transpose(1, 0, 3, 2)
    tri = jnp.tril(jnp.ones((chunk, chunk), bool))
    strict = jnp.tril(jnp.ones((chunk, chunk), bool), -1)
    eye = jnp.eye(chunk, dtype=jnp.float32)

    def step(S, inp):
        qc, kc, vc, gc, bc = inp
        G = jnp.cumsum(gc, axis=-1)
        decay = jnp.exp(jnp.where(tri, G[..., :, None] - G[..., None, :], NEG))
        kb = kc * bc[..., None]
        a_mat = jnp.where(strict, jnp.einsum('bhid,bhjd->bhij', kb, kc) * decay, 0.0)
        rhs = jnp.concatenate([vc * bc[..., None], kb * jnp.exp(G)[..., None]], axis=-1)
        sol = lax.linalg.triangular_solve(eye + a_mat, rhs, left_side=True, lower=True,
                                          unit_diagonal=True)
        u, w = sol[..., :DV], sol[..., DV:]
        v_new = u - jnp.einsum('bhck,bhkv->bhcv', w, S)
        qk = jnp.einsum('bhik,bhjk->bhij', qc, kc) * decay
        o = (jnp.einsum('bhck,bhkv->bhcv', qc * jnp.exp(G)[..., None], S)
             + jnp.einsum('bhij,bhjv->bhiv', qk, v_new))
        g_last = G[..., -1]
        S = (S * jnp.exp(g_last)[..., None, None]
             + jnp.einsum('bhck,bhcv->bhkv', kc * jnp.exp(g_last[..., None] - G)[..., None], v_new))
        return S, o

    S, o = lax.scan(step, s0, (to_c(q), to_c(k), to_c(v), to_s(g), to_s(beta)))
    return o.transpose(1, 0, 3, 2, 4).reshape(B, T, H, DV), S


def gdn_mixer(qkv, z, b_raw, a_raw, conv_buf, s0, conv_w, a_log, dt_bias, norm_g, chunk):
    B, T, _ = qkv.shape
    y, new_buf = causal_conv(qkv, conv_buf, conv_w)
    y = jax.nn.silu(y.astype(jnp.float32))
    q, k, v = jnp.split(y, [GDN_HEADS * GDN_DK, 2 * GDN_HEADS * GDN_DK], axis=-1)
    q = l2norm(q.reshape(B, T, GDN_HEADS, GDN_DK)) * (GDN_DK ** -0.5)
    k = l2norm(k.reshape(B, T, GDN_HEADS, GDN_DK))
    v = v.reshape(B, T, GDN_HEADS, GDN_DV)
    beta = jax.nn.sigmoid(b_raw.astype(jnp.float32))
    g = -jnp.exp(a_log.astype(jnp.float32)) * jax.nn.softplus(
        a_raw.astype(jnp.float32) + dt_bias.astype(jnp.float32))
    o, s_new = gated_delta_chunked(q, k, v, g, beta, s0.astype(jnp.float32), chunk)
    o = rmsnorm(o, norm_g) * jax.nn.silu(z.reshape(B, T, GDN_HEADS, GDN_DV).astype(jnp.float32))
    return o.reshape(B, T, GDN_HEADS * GDN_DV).astype(qkv.dtype), new_buf, s_new


def cmp_chunk_sums(rows, cmp_pos):
    B, T = rows.shape[:2]
    r = rows.astype(jnp.float32).reshape(B, T // CMP_STRIDE, CMP_STRIDE, 2, NSA_KV_HEADS, HEAD_DIM)
    w = cmp_pos.astype(jnp.float32).reshape(2, CMP_LEN // CMP_STRIDE, CMP_STRIDE, HEAD_DIM)
    return jnp.einsum('bcjegd,ehjd->bhcegd', r, w)


def compress(sums, cmp_proj):
    pooled = sums[:, 0, :-1] + sums[:, 1, 1:]
    return jnp.einsum('bnegd,edf->bnegf', pooled, cmp_proj.astype(jnp.float32))


def cmp_to_sel_map(n_cmp, n_slc):
    start = CMP_STRIDE * jnp.arange(n_cmp)[:, None]
    bstart = SEL_BLOCK * jnp.arange(n_slc)[None, :]
    return ((start < bstart + SEL_BLOCK) & (start + CMP_LEN > bstart)).astype(jnp.float32)


def nsa_attend(q, q0, comp, fetch, n_slc, win_ext, l_buf, qb):
    B, Sq = q.shape[:2]
    n_cmp = comp.shape[1]
    k_sel = min(N_SEL, n_slc)
    cmap = cmp_to_sel_map(n_cmp, n_slc)
    cmp_end = CMP_STRIDE * jnp.arange(n_cmp) + CMP_LEN - 1
    j_idx = jnp.arange(n_slc)
    scale = HEAD_DIM ** -0.5

    def block(c):
        qs = c * qb
        qg = lax.dynamic_slice_in_dim(q, qs, qb, axis=1).reshape(
            B, qb, NSA_KV_HEADS, NSA_HPG, HEAD_DIM) * scale
        t = q0 + qs + jnp.arange(qb)
        s = jnp.einsum('bqghd,bngd->bghqn', qg, comp[:, :, 0])
        p = masked_softmax(s, (cmp_end[None, :] <= t[:, None])[None, None, None])
        o_cmp = jnp.einsum('bghqn,bngd->bqghd', p, comp[:, :, 1])
        imp = jnp.einsum('bghqn,nj->bgqj', p, cmap)
        jt = (t // SEL_BLOCK)[:, None]
        force = (j_idx == 0) | (j_idx == jt) | (j_idx == jt - 1)
        score = jnp.where(j_idx > jt, NEG, jnp.where(force, FORCE, imp))
        vals, idx = lax.top_k(score, k_sel)
        kv = fetch(idx).astype(jnp.float32).reshape(
            B, NSA_KV_HEADS, qb, k_sel * SEL_BLOCK, 2, HEAD_DIM)
        kpos = idx[..., None] * SEL_BLOCK + jnp.arange(SEL_BLOCK)
        ok = (vals > NEG / 2)[..., None] & (kpos <= t[:, None, None])
        s = jnp.einsum('bqghd,bgqnd->bghqn', qg, kv[..., 0, :])
        p = masked_softmax(s, ok.reshape(B, NSA_KV_HEADS, qb, -1)[:, :, None])
        o_slc = jnp.einsum('bghqn,bgqnd->bqghd', p, kv[..., 1, :])
        wkv = lax.dynamic_slice_in_dim(win_ext, qs, qb + l_buf, axis=1).astype(jnp.float32)
        wpos = q0 - l_buf + qs + jnp.arange(qb + l_buf)
        wok = ((wpos[None, :] <= t[:, None]) & (wpos[None, :] > t[:, None] - WINDOW)
               & (wpos[None, :] >= 0))
        s = jnp.einsum('bqghd,bngd->bghqn', qg, wkv[:, :, 0])
        p = masked_softmax(s, wok[None, None, None])
        o_win = jnp.einsum('bghqn,bngd->bqghd', p, wkv[:, :, 1])
        return jnp.stack([o_cmp, o_slc, o_win], axis=-2).reshape(B, qb, NSA_HEADS, 3, HEAD_DIM)

    out = lax.map(block, jnp.arange(Sq // qb))
    return jnp.moveaxis(out, 0, 1).reshape(B, Sq, NSA_HEADS, 3, HEAD_DIM)


def nsa_prompt(q, kv6, cmp_pos, cmp_proj):
    B, T = q.shape[:2]
    rows = kv6[:, :, :4]
    comp = compress(cmp_chunk_sums(rows[:, :, :2], cmp_pos), cmp_proj)
    n_slc = T // SEL_BLOCK
    blocks = rows[:, :, 2:].reshape(B, n_slc, SEL_BLOCK, 2, NSA_KV_HEADS, HEAD_DIM).transpose(
        0, 4, 1, 2, 3, 5)
    bi = jnp.arange(B)[:, None, None, None]
    gi = jnp.arange(NSA_KV_HEADS)[None, :, None, None]

    def fetch(idx):
        return blocks[bi, gi, idx]

    win = kv6[:, :, 4:]
    win_ext = jnp.concatenate([jnp.zeros((B, WINDOW) + win.shape[2:], win.dtype), win], axis=1)
    o = nsa_attend(q, 0, comp, fetch, n_slc, win_ext, WINDOW, math.gcd(T, NSA_QBLOCK))
    return o, rows, win[:, max(T - WINDOW, 0):]


def nsa_sample(q, kv6, kv_pool, win_buf, page_table, cmp_pos, cmp_proj):
    B, S = q.shape[:2]
    P = page_table.shape[1] * PAGE_SIZE
    rows = kv6[:, :, :4]
    past_cmp = kv_pool[page_table[:, :, None, None], jnp.arange(PAGE_SIZE)[:, None], jnp.arange(2)]
    past_cmp = past_cmp.reshape(B, P, 2, NSA_KV_HEADS, HEAD_DIM)
    s16 = -(-S // CMP_STRIDE) * CMP_STRIDE
    new_cmp = jnp.pad(rows[:, :, :2], ((0, 0), (0, s16 - S), (0, 0), (0, 0), (0, 0)))
    sums = jnp.concatenate([cmp_chunk_sums(past_cmp, cmp_pos), cmp_chunk_sums(new_cmp, cmp_pos)],
                           axis=2)
    comp = compress(sums, cmp_proj)
    n_past_blk = P // SEL_BLOCK
    n_new_blk = -(-S // SEL_BLOCK)
    new_slc = jnp.pad(rows[:, :, 2:], ((0, 0), (0, n_new_blk * SEL_BLOCK - S), (0, 0), (0, 0), (0, 0)))
    new_blocks = new_slc.reshape(B, n_new_blk, SEL_BLOCK, 2, NSA_KV_HEADS, HEAD_DIM).transpose(
        0, 4, 1, 2, 3, 5)
    blk_per_page = PAGE_SIZE // SEL_BLOCK
    bi = jnp.arange(B)[:, None, None, None]
    gi = jnp.arange(NSA_KV_HEADS)[None, :, None, None]

    def fetch(idx):
        jp = jnp.minimum(idx, n_past_blk - 1)
        phys = page_table[bi, jp // blk_per_page]
        rws = (jp % blk_per_page)[..., None] * SEL_BLOCK + jnp.arange(SEL_BLOCK)
        past = kv_pool[phys[..., None], rws, 2:, gi[..., None]]
        new = new_blocks[bi, gi, jnp.clip(idx - n_past_blk, 0, n_new_blk - 1)]
        return jnp.where((idx < n_past_blk)[..., None, None, None],
                         past.astype(jnp.float32), new.astype(jnp.float32))

    win_new = kv6[:, :, 4:]
    win_ext = jnp.concatenate([win_buf.astype(win_new.dtype), win_new], axis=1)
    l_buf = win_buf.shape[1]
    o = nsa_attend(q, P, comp, fetch, n_past_blk + n_new_blk, win_ext, l_buf, math.gcd(S, NSA_QBLOCK))
    keep = min(WINDOW, P + S)
    return o, rows, win_ext[:, win_ext.shape[1] - keep:]


def moe(h, w_router, b_router, w_gu, b_gu, w_down, b_down):
    N = h.shape[0]
    logits = h.astype(jnp.float32) @ w_router.astype(jnp.float32) + b_router.astype(jnp.float32)
    top_v, top_e = lax.top_k(logits, TOP_K)
    gate = jax.nn.softmax(top_v, axis=-1)
    n_pairs = N * TOP_K
    e_flat = top_e.reshape(-1)
    tok = jnp.arange(n_pairs) // TOP_K
    order = jnp.argsort(e_flat)
    e_s, tok_s, w_s = e_flat[order], tok[order], gate.reshape(-1)[order]
    counts = jnp.bincount(e_flat, length=N_EXPERTS)
    start = jnp.cumsum(counts) - counts
    padded = (counts + MOE_BLOCK - 1) // MOE_BLOCK * MOE_BLOCK
    pad_end = jnp.cumsum(padded)
    pad_start = pad_end - padded
    slot = pad_start[e_s] + (jnp.arange(n_pairs) - start[e_s])
    n_blocks = -(-n_pairs // MOE_BLOCK) + N_EXPERTS
    row_tok = jnp.zeros((n_blocks * MOE_BLOCK,), jnp.int32).at[slot].set(tok_s)
    blk_e = jnp.minimum(jnp.searchsorted(pad_end, jnp.arange(n_blocks) * MOE_BLOCK, side='right'),
                        N_EXPERTS - 1)
    xb = h[row_tok].reshape(n_blocks, MOE_BLOCK, h.shape[1])

    def expert_block(args):
        xblk, e = args
        gu = xblk @ w_gu[e] + b_gu[e]
        g = jnp.minimum(gu[:, :D_FF], SWIGLU_LIMIT)
        u = jnp.clip(gu[:, D_FF:], -SWIGLU_LIMIT, SWIGLU_LIMIT)
        hid = (u + 1.0) * g * jax.nn.sigmoid(SWIGLU_ALPHA * g)
        return hid @ w_down[e] + b_down[e]

    yb = lax.map(expert_block, (xb, blk_e)).reshape(n_blocks * MOE_BLOCK, h.shape[1])
    contrib = yb[slot] * w_s[:, None].astype(yb.dtype)
    return jax.ops.segment_sum(contrib, tok_s, num_segments=N).astype(h.dtype)


def hybrid_layer(x_p, x_s, kv_pool, win_buf, s_gdn, conv_buf, page_table,
                 norm_mix, w_in, conv_w, gdn_a_log, gdn_dt_bias, gdn_norm, w_out_a,
                 cmp_pos, cmp_proj, w_out_b, w_out, norm_ffn, w_router, b_router,
                 w_gu, b_gu, w_down, b_down):
    splits = [int(s) for s in np.cumsum(IN_SPLITS)[:-1]]

    def project(x):
        return jnp.split(rmsnorm(x, norm_mix) @ w_in, splits, axis=-1)

    def merge(x, o_a, o_nsa, gate_raw, merge_raw):
        B, T = x.shape[:2]
        gb = jax.nn.sigmoid(gate_raw.astype(jnp.float32)).reshape(B, T, NSA_HEADS, 3, 1)
        o_b = jnp.sum(o_nsa * gb, axis=-2).reshape(B, T, NSA_HEADS * HEAD_DIM).astype(x.dtype)
        m = jax.nn.sigmoid(merge_raw.astype(jnp.float32)).reshape(B, T, 2, D_MODEL)
        y = m[:, :, 0] * (o_a @ w_out_a) + m[:, :, 1] * (o_b @ w_out_b)
        return x + y.astype(x.dtype) @ w_out

    Bp, T = x_p.shape[:2]
    qkv, z, b_raw, a_raw, q_b, kv_b, gate_b, merge_raw = project(x_p)
    o_a, conv_p, s_p = gdn_mixer(qkv, z, b_raw, a_raw,
                                 jnp.zeros((Bp, CONV_W - 1, CONV_DIM), x_p.dtype),
                                 jnp.zeros((Bp, GDN_HEADS, GDN_DK, GDN_DV), jnp.float32),
                                 conv_w, gdn_a_log, gdn_dt_bias, gdn_norm, math.gcd(T, GDN_CHUNK))
    o_nsa, kv_p, win_p = nsa_prompt(q_b.reshape(Bp, T, NSA_HEADS, HEAD_DIM).astype(jnp.float32),
                                    kv_b.reshape(Bp, T, 6, NSA_KV_HEADS, HEAD_DIM), cmp_pos, cmp_proj)
    h_p = merge(x_p, o_a, o_nsa, gate_b, merge_raw)

    Bs, S = x_s.shape[:2]
    qkv, z, b_raw, a_raw, q_b, kv_b, gate_b, merge_raw = project(x_s)
    o_a, conv_s, s_s = gdn_mixer(qkv, z, b_raw, a_raw, conv_buf, s_gdn,
                                 conv_w, gdn_a_log, gdn_dt_bias, gdn_norm, math.gcd(S, GDN_CHUNK))
    o_nsa, kv_s, win_s = nsa_sample(q_b.reshape(Bs, S, NSA_HEADS, HEAD_DIM).astype(jnp.float32),
                                    kv_b.reshape(Bs, S, 6, NSA_KV_HEADS, HEAD_DIM),
                                    kv_pool, win_buf, page_table, cmp_pos, cmp_proj)
    h_s = merge(x_s, o_a, o_nsa, gate_b, merge_raw)

    n_p = Bp * T
    hs = jnp.concatenate([h_p.reshape(n_p, D_MODEL), h_s.reshape(Bs * S, D_MODEL)], axis=0)
    hs = hs + moe(rmsnorm(hs, norm_ffn), w_router, b_router, w_gu, b_gu, w_down, b_down)
    return (hs[:n_p].reshape(x_p.shape), hs[n_p:].reshape(x_s.shape), kv_p, kv_s, win_p, win_s,
            s_p.astype(s_gdn.dtype), s_s.astype(s_gdn.dtype), conv_p, conv_s)


def setup_inputs(seed: int = 0) -> dict:
    key = jax.random.key(seed)
    ks = jax.random.split(key, 32)
    f32 = jnp.float32
    n_pages = PAST_LEN // PAGE_SIZE
    n_pool = (DEC_BATCH * n_pages * 5) // 4
    win_len = min(WINDOW, PAST_LEN)

    def nrm(k, shape, scale):
        return scale * jax.random.normal(k, shape, f32)

    dt = jnp.exp(jax.random.uniform(ks[11], (DEPTH, GDN_HEADS), f32, math.log(1e-3), math.log(1e-1)))
    return {
        "x_prompt": nrm(ks[0], (BATCH, SEQ, D_MODEL), 1.0),
        "x_sample": nrm(ks[1], (DEC_BATCH, DEC_SEQ, D_MODEL), 1.0),
        "cache_nsa_kv": nrm(ks[2], (DEPTH, n_pool, PAGE_SIZE, 4, NSA_KV_HEADS, HEAD_DIM), 1.0),
        "cache_nsa_win": nrm(ks[3], (DEPTH, DEC_BATCH, win_len, 2, NSA_KV_HEADS, HEAD_DIM), 1.0),
        "state_gdn": nrm(ks[4], (DEPTH, DEC_BATCH, GDN_HEADS, GDN_DK, GDN_DV), 0.1),
        "state_conv": nrm(ks[5], (DEPTH, DEC_BATCH, CONV_W - 1, CONV_DIM), 1.0),
        "page_table": jax.random.permutation(ks[6], n_pool)[:DEC_BATCH * n_pages].reshape(
            DEC_BATCH, n_pages).astype(jnp.int32),
        "norm_mix": 1.0 + nrm(ks[7], (DEPTH, D_MODEL), 0.02),
        "w_in": nrm(ks[8], (DEPTH, D_MODEL, N_IN), D_MODEL ** -0.5),
        "conv_w": nrm(ks[9], (DEPTH, CONV_W, CONV_DIM), CONV_W ** -0.5),
        "gdn_a_log": jnp.log(jax.random.uniform(ks[10], (DEPTH, GDN_HEADS), f32, 1.0, 16.0)),
        "gdn_dt_bias": dt + jnp.log(-jnp.expm1(-dt)),
        "gdn_norm": 1.0 + nrm(ks[12], (DEPTH, GDN_DV), 0.02),
        "w_out_a": nrm(ks[13], (DEPTH, GDN_HEADS * GDN_DV, D_MODEL), (GDN_HEADS * GDN_DV) ** -0.5),
        "cmp_pos": (1.0 + nrm(ks[14], (DEPTH, 2, CMP_LEN, HEAD_DIM), 0.1)) * (CMP_LEN ** -0.5),
        "cmp_proj": nrm(ks[15], (DEPTH, 2, HEAD_DIM, HEAD_DIM), HEAD_DIM ** -0.5),
        "w_out_b": nrm(ks[16], (DEPTH, NSA_HEADS * HEAD_DIM, D_MODEL), (NSA_HEADS * HEAD_DIM) ** -0.5),
        "w_out": nrm(ks[17], (DEPTH, D_MODEL, D_MODEL), D_MODEL ** -0.5),
        "norm_ffn": 1.0 + nrm(ks[18], (DEPTH, D_MODEL), 0.02),
        "w_router": nrm(ks[19], (DEPTH, D_MODEL, N_EXPERTS), D_MODEL ** -0.5),
        "b_router": nrm(ks[20], (DEPTH, N_EXPERTS), 0.01),
        "w_gu": nrm(ks[21], (DEPTH, N_EXPERTS, D_MODEL, 2 * D_FF), D_MODEL ** -0.5),
        "b_gu": nrm(ks[22], (DEPTH, N_EXPERTS, 2 * D_FF), 0.01),
        "w_down": nrm(ks[23], (DEPTH, N_EXPERTS, D_FF, D_MODEL), D_FF ** -0.5),
        "b_down": nrm(ks[24], (DEPTH, N_EXPERTS, D_MODEL), 0.01),
        "norm_final": 1.0 + nrm(ks[25], (D_MODEL,), 0.02),
    }


def reference(x_prompt, x_sample, cache_nsa_kv, cache_nsa_win, state_gdn, state_conv, page_table,
              norm_mix, w_in, conv_w, gdn_a_log, gdn_dt_bias, gdn_norm, w_out_a, cmp_pos, cmp_proj,
              w_out_b, w_out, norm_ffn, w_router, b_router, w_gu, b_gu, w_down, b_down, norm_final):
    xp, xs = x_prompt, x_sample
    per_layer = []
    for l in range(DEPTH):
        xp, xs, *st = hybrid_layer(
            xp, xs, cache_nsa_kv[l], cache_nsa_win[l], state_gdn[l], state_conv[l], page_table,
            norm_mix[l], w_in[l], conv_w[l], gdn_a_log[l], gdn_dt_bias[l], gdn_norm[l], w_out_a[l],
            cmp_pos[l], cmp_proj[l], w_out_b[l], w_out[l], norm_ffn[l], w_router[l], b_router[l],
            w_gu[l], b_gu[l], w_down[l], b_down[l])
        per_layer.append(st)
    y_prompt = rmsnorm(xp, norm_final)
    y_sample = rmsnorm(xs, norm_final)
    new_kv_prompt = jnp.stack([s[0] for s in per_layer])
    new_kv_sample = jnp.stack([s[1] for s in per_layer])
    new_win_prompt = jnp.stack([s[2] for s in per_layer])
    new_win_sample = jnp.stack([s[3] for s in per_layer])
    new_gdn_prompt = jnp.stack([s[4] for s in per_layer])
    new_gdn_sample = jnp.stack([s[5] for s in per_layer])
    new_conv_prompt = jnp.stack([s[6] for s in per_layer])
    new_conv_sample = jnp.stack([s[7] for s in per_layer])
    return (y_prompt, y_sample, new_kv_prompt, new_kv_sample, new_win_prompt, new_win_sample,
            new_gdn_prompt, new_gdn_sample, new_conv_prompt, new_conv_sample)
```

```python
import functools
import math

import jax
import jax.numpy as jnp
from jax import lax
from jax.experimental import pallas as pl
from jax.experimental.pallas import tpu as pltpu

F32 = jnp.float32
BF16 = jnp.bfloat16
HIGHEST = lax.Precision.HIGHEST

D_MODEL = 2048
GDN_HEADS = 16
GDN_DK = 128
GDN_DV = 128
CONV_W = 4
CONV_DIM = GDN_HEADS * (2 * GDN_DK + GDN_DV)
NSA_HEADS = 16
NSA_KV_HEADS = 4
HEAD_DIM = 128
NSA_HPG = NSA_HEADS // NSA_KV_HEADS
CMP_LEN = 32
CMP_STRIDE = 16
SEL_BLOCK = 64
N_SEL = 16
WINDOW = 512
PAGE_SIZE = 128
N_EXPERTS = 32
TOP_K = 4
D_FF = 2048
SWIGLU_LIMIT = 7.0
SWIGLU_ALPHA = 1.702
EPS = 1e-6
NEG = -1e30
FORCE = 1e6

IN_SPLITS = (CONV_DIM, GDN_HEADS * GDN_DV, GDN_HEADS, GDN_HEADS, NSA_HEADS * HEAD_DIM,
             6 * NSA_KV_HEADS * HEAD_DIM, 3 * NSA_HEADS, 2 * D_MODEL)
_OFF = [0]
for _s in IN_SPLITS:
    _OFF.append(_OFF[-1] + _s)
(OFF_QKV, OFF_Z, OFF_BRAW, OFF_ARAW, OFF_QB, OFF_KVB, OFF_GATEB, OFF_MERGE, N_IN) = _OFF

LANE = 128
CB_Q, CB_K, CB_V = 0, GDN_HEADS, 2 * GDN_HEADS
CB_Z = CONV_DIM // LANE
CB_MERGE = CB_Z + GDN_HEADS
CB_QB = CB_MERGE + 2 * D_MODEL // LANE
CB_KVB = CB_QB + NSA_HEADS
N_MAIN = (CB_KVB + 6 * NSA_KV_HEADS) * LANE
SM_BRAW, SM_ARAW, SM_GATE = 0, GDN_HEADS, 2 * GDN_HEADS

GDN_CHUNK = 128
NSA_TQ = 128
NSA_KT = 512
MOE_BLK = 256
VMEM_LIMIT = 56 * 1024 * 1024


def _cp(sem, vmem=VMEM_LIMIT):
    return pltpu.CompilerParams(dimension_semantics=sem, vmem_limit_bytes=vmem)


def _lane_pick(x, lane):
    ids = lax.broadcasted_iota(jnp.int32, x.shape, 1)
    return jnp.sum(jnp.where(ids == lane, x, 0.0), axis=1, keepdims=True)


def _sigmoid(x):
    return 1.0 / (1.0 + jnp.exp(-x))


def _silu(x):
    return x * _sigmoid(x)


def _softplus(x):
    return jnp.maximum(x, 0.0) + jnp.log(1.0 + jnp.exp(-jnp.abs(x)))


def _proj_kernel(x_ref, g_ref, w_ref, ws_ref, o_ref, os_ref, xn_ref):
    @pl.when(pl.program_id(1) == 0)
    def _():
        x = x_ref[...]
        ms = jnp.mean(x * x, axis=-1, keepdims=True)
        xn = (x * lax.rsqrt(ms + EPS) * g_ref[...]).astype(BF16)
        xn_ref[...] = xn
        os_ref[...] = jnp.dot(xn, ws_ref[...], preferred_element_type=F32)

    o_ref[...] = jnp.dot(xn_ref[...], w_ref[...], preferred_element_type=F32)


def _proj_prompt(x, norm_g, w_main, w_small):
    n = x.shape[0]
    tm = min(1024, n)
    tn = 512
    return pl.pallas_call(
        _proj_kernel,
        out_shape=(jax.ShapeDtypeStruct((n, N_MAIN), F32), jax.ShapeDtypeStruct((n, LANE), F32)),
        grid=(n // tm, N_MAIN // tn),
        in_specs=[pl.BlockSpec((tm, D_MODEL), lambda i, j: (i, 0)),
                  pl.BlockSpec((1, D_MODEL), lambda i, j: (0, 0)),
                  pl.BlockSpec((D_MODEL, tn), lambda i, j: (0, j)),
                  pl.BlockSpec((D_MODEL, LANE), lambda i, j: (0, 0))],
        out_specs=(pl.BlockSpec((tm, tn), lambda i, j: (i, j)),
                   pl.BlockSpec((tm, LANE), lambda i, j: (i, 0))),
        scratch_shapes=[pltpu.VMEM((tm, D_MODEL), BF16)],
        compiler_params=_cp(("parallel", "arbitrary")),
        name="proj_prompt",
    )(x, norm_g.reshape(1, D_MODEL), w_main, w_small)


def _shift_rows(x, prev8, s):
    xs = pltpu.roll(x, s, 0)
    ps = pltpu.roll(prev8, s, 0)
    row = lax.broadcasted_iota(jnp.int32, prev8.shape, 0)
    top = jnp.where(row < s, ps, xs[0:8])
    return jnp.concatenate([top, xs[8:]], axis=0)


def _conv_silu(x, prev8, cw):
    y = x * cw[CONV_W - 1:CONV_W]
    for s in range(1, CONV_W):
        y = y + _shift_rows(x, prev8, s) * cw[CONV_W - 1 - s:CONV_W - s]
    return _silu(y)


def _l2norm(x):
    return x * lax.rsqrt(jnp.sum(x * x, axis=-1, keepdims=True) + EPS)


def _cumsum_rows(x):
    c = x.shape[0]
    row = lax.broadcasted_iota(jnp.int32, x.shape, 0)
    s = 1
    while s < c:
        x = x + jnp.where(row >= s, pltpu.roll(x, s, 0), 0.0)
        s *= 2
    return x


def _bdot(a, b):
    return jnp.dot(a.astype(BF16), b.astype(BF16), preferred_element_type=F32)


def _bdot_nt(a, b):
    return lax.dot_general(a.astype(BF16), b.astype(BF16), (((1,), (1,)), ((), ())),
                           preferred_element_type=F32)


def _bdot_tn(a, b):
    return lax.dot_general(a.astype(BF16), b.astype(BF16), (((0,), (0,)), ((), ())),
                           preferred_element_type=F32)


INV_BASE = 16


def _unit_lower_inverse(a, ri, ci):
    c = a.shape[0]
    same = lambda s: (ri // s) == (ci // s)
    d = jnp.where(same(INV_BASE), a, 0.0)
    t = jnp.where(ri == ci, 1.0, 0.0) - d
    p = d
    n = 2
    while n < INV_BASE:
        p = _bdot(p, p)
        t = t + _bdot(t, p)
        n *= 2
    s = INV_BASE
    while s < c:
        m = jnp.where(same(2 * s) & jnp.logical_not(same(s)), a, 0.0)
        t = t - _bdot(_bdot(t, m), t)
        s *= 2
    return t


def _gdn_kernel(q_ref, k_ref, v_ref, z_ref, sm_ref, cwq_ref, cwk_ref, cwv_ref, hp_ref, ng_ref,
                o_ref, s_out_ref, s_ref, pq_ref, pk_ref, pv_ref):
    h = pl.program_id(1)
    c = pl.program_id(2)
    C = GDN_CHUNK

    @pl.when(c == 0)
    def _():
        s_ref[...] = jnp.zeros_like(s_ref)
        pq_ref[...] = jnp.zeros_like(pq_ref)
        pk_ref[...] = jnp.zeros_like(pk_ref)
        pv_ref[...] = jnp.zeros_like(pv_ref)

    xq, xk, xv = q_ref[...], k_ref[...], v_ref[...]
    q = _l2norm(_conv_silu(xq, pq_ref[...], cwq_ref[...])) * (GDN_DK ** -0.5)
    k = _l2norm(_conv_silu(xk, pk_ref[...], cwk_ref[...]))
    v = _conv_silu(xv, pv_ref[...], cwv_ref[...])
    pq_ref[...] = xq[C - 8:]
    pk_ref[...] = xk[C - 8:]
    pv_ref[...] = xv[C - 8:]

    sm = sm_ref[...]
    beta = _sigmoid(_lane_pick(sm, SM_BRAW + h))
    hp = hp_ref[...]
    g_all = -jnp.exp(hp[0:1]) * _softplus(sm + hp[1:2])
    G = _lane_pick(_cumsum_rows(g_all), SM_ARAW + h)
    g_last = G[C - 1:C]
    Gb = jnp.broadcast_to(G, (C, C))
    ri = lax.broadcasted_iota(jnp.int32, (C, C), 0)
    ci = lax.broadcasted_iota(jnp.int32, (C, C), 1)
    decay = jnp.exp(jnp.where(ri >= ci, Gb - Gb.T, NEG))

    kb = k * beta
    a_mat = jnp.where(ri > ci, _bdot_nt(kb, k) * decay, 0.0)
    rhs = jnp.concatenate([v * beta, kb * jnp.exp(G)], axis=1)
    x = _bdot(_unit_lower_inverse(a_mat, ri, ci), rhs)
    u, w = x[:, :GDN_DV], x[:, GDN_DV:]

    S = s_ref[...]
    v_new = u - _bdot(w, S)
    qk = _bdot_nt(q, k) * decay
    o = _bdot(q * jnp.exp(G), S) + _bdot(qk, v_new)
    s_new = S * jnp.exp(g_last) + _bdot_tn(k * jnp.exp(g_last - G), v_new)
    s_ref[...] = s_new

    @pl.when(c == pl.num_programs(2) - 1)
    def _():
        s_out_ref[0, 0] = s_new

    on = o * lax.rsqrt(jnp.mean(o * o, axis=-1, keepdims=True) + EPS) * ng_ref[...]
    o_ref[...] = (on * _silu(z_ref[...])).astype(o_ref.dtype)


def _head_params(a_log, dt_bias):
    hp = jnp.zeros((8, LANE), F32)
    hp = hp.at[0, SM_ARAW:SM_ARAW + GDN_HEADS].set(a_log.astype(F32))
    return hp.at[1, SM_ARAW:SM_ARAW + GDN_HEADS].set(dt_bias.astype(F32))


def _gdn_prompt(proj, small, conv_w, head_par, norm_g, batch, seq):
    C = GDN_CHUNK
    nc = seq // C
    H = GDN_HEADS

    def rows(cb):
        return pl.BlockSpec((C, LANE), lambda b, h, c, cb=cb: (b * nc + c, cb + h))

    def cw(cb):
        return pl.BlockSpec((CONV_W, LANE), lambda b, h, c, cb=cb: (0, cb + h))

    return pl.pallas_call(
        _gdn_kernel,
        out_shape=(jax.ShapeDtypeStruct((batch * seq, H * GDN_DV), BF16),
                   jax.ShapeDtypeStruct((batch, H, GDN_DK, GDN_DV), F32)),
        grid=(batch, H, nc),
        in_specs=[rows(CB_Q), rows(CB_K), rows(CB_V), rows(CB_Z),
                  pl.BlockSpec((C, LANE), lambda b, h, c: (b * nc + c, 0)),
                  cw(CB_Q), cw(CB_K), cw(CB_V),
                  pl.BlockSpec((8, LANE), lambda b, h, c: (0, 0)),
                  pl.BlockSpec((1, GDN_DV), lambda b, h, c: (0, 0))],
        out_specs=(pl.BlockSpec((C, GDN_DV), lambda b, h, c: (b * nc + c, h)),
                   pl.BlockSpec((1, 1, GDN_DK, GDN_DV), lambda b, h, c: (b, h, 0, 0))),
        scratch_shapes=[pltpu.VMEM((GDN_DK, GDN_DV), F32), pltpu.VMEM((8, LANE), F32),
                        pltpu.VMEM((8, LANE), F32), pltpu.VMEM((8, LANE), F32)],
        compiler_params=_cp(("parallel", "parallel", "arbitrary")),
        name="gdn_prompt",
    )(proj, proj, proj, proj, small, conv_w, conv_w, conv_w, head_par, norm_g.reshape(1, GDN_DV))


def _cmp_kernel(x_ref, w_ref, p_ref, o_ref):
    t = x_ref.shape[0]
    nch = t // CMP_STRIDE
    x3 = x_ref[...].reshape(nch, CMP_STRIDE, HEAD_DIM)
    w = w_ref[0]
    s0 = jnp.sum(x3 * w[None, :CMP_STRIDE], axis=1)
    s1 = jnp.sum(x3 * w[None, CMP_STRIDE:], axis=1)
    row = lax.broadcasted_iota(jnp.int32, (nch, HEAD_DIM), 0)
    pooled = jnp.where(row < nch - 1, s0 + pltpu.roll(s1, nch - 1, 0), 0.0)
    o_ref[0, 0, 0] = jnp.dot(pooled, p_ref[0], precision=HIGHEST, preferred_element_type=F32)


def _compress_prompt(proj, cmp_pos, cmp_proj, batch, seq):
    nch = seq // CMP_STRIDE
    return pl.pallas_call(
        _cmp_kernel,
        out_shape=jax.ShapeDtypeStruct((batch, 2, NSA_KV_HEADS, nch, HEAD_DIM), F32),
        grid=(batch, 2, NSA_KV_HEADS),
        in_specs=[pl.BlockSpec((seq, LANE), lambda b, e, g: (b, CB_KVB + e * NSA_KV_HEADS + g)),
                  pl.BlockSpec((1, CMP_LEN, HEAD_DIM), lambda b, e, g: (e, 0, 0)),
                  pl.BlockSpec((1, HEAD_DIM, HEAD_DIM), lambda b, e, g: (e, 0, 0))],
        out_specs=pl.BlockSpec((1, 1, 1, nch, HEAD_DIM), lambda b, e, g: (b, e, g, 0, 0)),
        compiler_params=_cp(("parallel", "parallel", "parallel")),
        name="nsa_compress_prompt",
    )(proj, cmp_pos, cmp_proj)


def _masked_softmax(s, mask):
    s = jnp.where(mask, s, NEG)
    p = jnp.exp(s - jnp.max(s, axis=-1, keepdims=True)) * mask.astype(F32)
    return p / jnp.maximum(jnp.sum(p, axis=-1, keepdims=True), 1e-30)


def _select_blocks(score, n_cols):
    j_idx = lax.broadcasted_iota(jnp.int32, score.shape, 1)
    cnt = jnp.zeros(score.shape, jnp.int32)
    for jp in range(n_cols):
        col = score[:, jp:jp + 1]
        ahead = (col > score) | ((col == score) & (j_idx > jp))
        cnt = cnt + ahead.astype(jnp.int32)
    return (cnt < N_SEL) & (score > NEG / 2)


def _nsa_kernel(q0_ref, q1_ref, q2_ref, q3_ref, sm_ref, ck_ref, cv_ref, ks_ref, vs_ref, kw_ref,
                vw_ref, o_ref, acc_ref, m_ref, l_ref):
    g = pl.program_id(1)
    qi = pl.program_id(2)
    TQ = q0_ref.shape[0]
    R4 = NSA_HPG * TQ
    T = ks_ref.shape[0]
    KT = min(NSA_KT, T)
    qs = qi * TQ
    q4 = jnp.concatenate([q0_ref[...], q1_ref[...], q2_ref[...], q3_ref[...]], axis=0)
    q4 = (q4 * (HEAD_DIM ** -0.5)).astype(BF16)
    t4 = qs + lax.rem(lax.broadcasted_iota(jnp.int32, (R4, 1), 0), TQ)
    t1 = qs + lax.broadcasted_iota(jnp.int32, (TQ, 1), 0)

    ck = ck_ref[0, 0, 0]
    nc = ck.shape[0]
    s = _bdot_nt(q4, ck)
    cmp_end = CMP_STRIDE * lax.broadcasted_iota(jnp.int32, (1, nc), 1) + (CMP_LEN - 1)
    p = _masked_softmax(s, cmp_end <= t4)
    o_cmp = _bdot(p, cv_ref[0, 0, 0])

    n_slc = T // SEL_BLOCK
    nsp = max(n_slc, LANE)
    psum = p[0:TQ]
    for j in range(1, NSA_HPG):
        psum = psum + p[j * TQ:(j + 1) * TQ]
    ci = CMP_STRIDE * lax.broadcasted_iota(jnp.int32, (nc, nsp), 0)
    bj = SEL_BLOCK * lax.broadcasted_iota(jnp.int32, (nc, nsp), 1)
    cmap = ((ci < bj + SEL_BLOCK) & (ci + CMP_LEN > bj)).astype(F32)
    imp = jnp.dot(psum, cmap, precision=HIGHEST, preferred_element_type=F32)
    j_idx = lax.broadcasted_iota(jnp.int32, (TQ, nsp), 1)
    jt = lax.shift_right_logical(t1, int(math.log2(SEL_BLOCK)))
    force = (j_idx == 0) | (j_idx == jt) | (j_idx == jt - 1)
    score = jnp.where(j_idx > jt, NEG, jnp.where(force, FORCE, imp))
    sel = _select_blocks(score, n_slc).astype(BF16)

    acc_ref[...] = jnp.zeros_like(acc_ref)
    m_ref[...] = jnp.full_like(m_ref, NEG)
    l_ref[...] = jnp.zeros_like(l_ref)
    n_kt = (qs + TQ - 1) // KT + 1

    def slc_body(kt, carry):
        k0 = pl.multiple_of(kt * KT, KT)
        kb = ks_ref[pl.ds(k0, KT), :]
        vb = vs_ref[pl.ds(k0, KT), :]
        s = _bdot_nt(q4, kb)
        kpos = k0 + lax.broadcasted_iota(jnp.int32, (1, KT), 1)
        eb = lax.broadcasted_iota(jnp.int32, (nsp, KT), 0) == lax.shift_right_logical(
            k0 + lax.broadcasted_iota(jnp.int32, (nsp, KT), 1), int(math.log2(SEL_BLOCK)))
        selx = jnp.dot(sel, eb.astype(BF16), preferred_element_type=F32)
        selx = jnp.concatenate([selx] * NSA_HPG, axis=0)
        ok = (selx > 0.5) & (kpos <= t4)
        s = jnp.where(ok, s, NEG)
        m_old = m_ref[...]
        m_new = jnp.maximum(m_old, jnp.max(s, axis=-1, keepdims=True))
        pe = jnp.exp(s - m_new) * ok.astype(F32)
        alpha = jnp.exp(m_old - m_new)
        l_ref[...] = alpha * l_ref[...] + jnp.sum(pe, axis=-1, keepdims=True)
        acc_ref[...] = alpha * acc_ref[...] + _bdot(pe, vb)
        m_ref[...] = m_new
        return carry

    lax.fori_loop(0, n_kt, slc_body, 0)
    o_slc = acc_ref[...] / jnp.maximum(l_ref[...], 1e-30)

    WK = min(WINDOW + TQ, T)
    w0 = pl.multiple_of(jnp.minimum(jnp.maximum(qs - WINDOW, 0), T - WK), TQ)
    s = _bdot_nt(q4, kw_ref[pl.ds(w0, WK), :])
    wpos = w0 + lax.broadcasted_iota(jnp.int32, (1, WK), 1)
    p = _masked_softmax(s, (wpos <= t4) & (wpos > t4 - WINDOW))
    o_win = _bdot(p, vw_ref[pl.ds(w0, WK), :])

    gates = _sigmoid(sm_ref[...])
    for j in range(NSA_HPG):
        lane0 = SM_GATE + (g * NSA_HPG + j) * 3
        rows = slice(j * TQ, (j + 1) * TQ)
        ob = (o_cmp[rows] * _lane_pick(gates, lane0) + o_slc[rows] * _lane_pick(gates, lane0 + 1)
              + o_win[rows] * _lane_pick(gates, lane0 + 2))
        o_ref[:, j * HEAD_DIM:(j + 1) * HEAD_DIM] = ob.astype(o_ref.dtype)


def _nsa_prompt(proj, small, comp, batch, seq):
    TQ = min(NSA_TQ, seq)
    nq = seq // TQ
    nch = seq // CMP_STRIDE
    G = NSA_KV_HEADS

    def qspec(j):
        return pl.BlockSpec((TQ, LANE), lambda b, g, i, j=j: (b * nq + i, CB_QB + g * NSA_HPG + j))

    def kvspec(e):
        return pl.BlockSpec((seq, LANE), lambda b, g, i, e=e: (b, CB_KVB + e * G + g))

    def cspec(e):
        return pl.BlockSpec((1, 1, 1, nch, HEAD_DIM), lambda b, g, i, e=e: (b, e, g, 0, 0))

    R4 = NSA_HPG * TQ
    return pl.pallas_call(
        _nsa_kernel,
        out_shape=jax.ShapeDtypeStruct((batch * seq, NSA_HEADS * HEAD_DIM), BF16),
        grid=(batch, G, nq),
        in_specs=[qspec(0), qspec(1), qspec(2), qspec(3),
                  pl.BlockSpec((TQ, LANE), lambda b, g, i: (b * nq + i, 0)),
                  cspec(0), cspec(1), kvspec(2), kvspec(3), kvspec(4), kvspec(5)],
        out_specs=pl.BlockSpec((TQ, NSA_HPG * HEAD_DIM), lambda b, g, i: (b * nq + i, g)),
        scratch_shapes=[pltpu.VMEM((R4, HEAD_DIM), F32), pltpu.VMEM((R4, 1), F32),
                        pltpu.VMEM((R4, 1), F32)],
        compiler_params=_cp(("parallel", "parallel", "arbitrary")),
        name="nsa_prompt",
    )(proj, proj, proj, proj, small, comp, comp, proj, proj, proj, proj)


def _dot(a, b, exact):
    if exact:
        return jnp.dot(a.astype(F32), b.astype(F32), precision=HIGHEST, preferred_element_type=F32)
    return jnp.dot(a.astype(BF16), b.astype(BF16), preferred_element_type=F32)


def _merge_kernel(oa_ref, ob_ref, wa_ref, wb_ref, m0_ref, m1_ref, y_ref, *, exact):
    ya = _dot(oa_ref[...], wa_ref[...], exact)
    yb = _dot(ob_ref[...], wb_ref[...], exact)
    y = _sigmoid(m0_ref[...]) * ya + _sigmoid(m1_ref[...]) * yb
    y_ref[...] = y.astype(y_ref.dtype)


def _merge(o_a, o_b, w_a, w_b, gates, m0_cb, m1_cb, exact):
    n = o_a.shape[0]
    tm = min(1024, n)
    tn = 512
    cpb = tn // LANE
    return pl.pallas_call(
        functools.partial(_merge_kernel, exact=exact),
        out_shape=jax.ShapeDtypeStruct((n, D_MODEL), F32 if exact else BF16),
        grid=(n // tm, D_MODEL // tn),
        in_specs=[pl.BlockSpec((tm, D_MODEL), lambda i, j: (i, 0)),
                  pl.BlockSpec((tm, D_MODEL), lambda i, j: (i, 0)),
                  pl.BlockSpec((D_MODEL, tn), lambda i, j: (0, j)),
                  pl.BlockSpec((D_MODEL, tn), lambda i, j: (0, j)),
                  pl.BlockSpec((tm, tn), lambda i, j: (i, m0_cb // cpb + j)),
                  pl.BlockSpec((tm, tn), lambda i, j: (i, m1_cb // cpb + j))],
        out_specs=pl.BlockSpec((tm, tn), lambda i, j: (i, j)),
        compiler_params=_cp(("parallel", "parallel")),
        name="merge_exact" if exact else "merge",
    )(o_a, o_b, w_a, w_b, gates, gates)


def _top4(logits):
    lane = lax.broadcasted_iota(jnp.int32, logits.shape, 1)
    ids = jnp.full(logits.shape, -1, jnp.int32)
    vals = jnp.full(logits.shape, NEG, F32)
    cur = logits
    for k in range(TOP_K):
        m = jnp.max(cur, axis=1, keepdims=True)
        idx = jnp.min(jnp.where(cur == m, lane, LANE), axis=1, keepdims=True)
        ids = jnp.where(lane == k, idx, ids)
        vals = jnp.where(lane == k, m, vals)
        cur = jnp.where(lane == idx, 2 * NEG, cur)
    e = jnp.where(lane < TOP_K, jnp.exp(vals - jnp.max(vals, axis=1, keepdims=True)), 0.0)
    return ids, e / jnp.sum(e, axis=1, keepdims=True)


def _outproj_kernel(x_ref, y_ref, w_ref, gn_ref, wr_ref, br_ref, h_ref, hn_ref, ids_ref, gate_ref,
                    *, exact):
    h = x_ref[...] + _dot(y_ref[...], w_ref[...], exact)
    h_ref[...] = h
    hn = h * lax.rsqrt(jnp.mean(h * h, axis=-1, keepdims=True) + EPS) * gn_ref[...]
    hn_ref[...] = hn
    logits = jnp.dot(hn, wr_ref[...], precision=HIGHEST, preferred_element_type=F32) + br_ref[...]
    lane = lax.broadcasted_iota(jnp.int32, logits.shape, 1)
    ids, gate = _top4(jnp.where(lane < N_EXPERTS, logits, NEG))
    ids_ref[...] = ids
    gate_ref[...] = gate


def _outproj_router(x, y, w_out, norm_g, w_router, b_router, exact):
    n = x.shape[0]
    tm = min(256, n)
    row = lambda i: (i, 0)
    fixed = lambda i: (0, 0)
    return pl.pallas_call(
        functools.partial(_outproj_kernel, exact=exact),
        out_shape=(jax.ShapeDtypeStruct((n, D_MODEL), F32), jax.ShapeDtypeStruct((n, D_MODEL), F32),
                   jax.ShapeDtypeStruct((n, LANE), jnp.int32), jax.ShapeDtypeStruct((n, LANE), F32)),
        grid=(n // tm,),
        in_specs=[pl.BlockSpec((tm, D_MODEL), row), pl.BlockSpec((tm, D_MODEL), row),
                  pl.BlockSpec((D_MODEL, D_MODEL), fixed), pl.BlockSpec((1, D_MODEL), fixed),
                  pl.BlockSpec((D_MODEL, LANE), fixed), pl.BlockSpec((1, LANE), fixed)],
        out_specs=(pl.BlockSpec((tm, D_MODEL), row), pl.BlockSpec((tm, D_MODEL), row),
                   pl.BlockSpec((tm, LANE), row), pl.BlockSpec((tm, LANE), row)),
        compiler_params=_cp(("parallel",)),
        name="outproj_router_exact" if exact else "outproj_router",
    )(x, y, w_out, norm_g.reshape(1, D_MODEL), w_router, b_router)


def _rank_kernel(ids_ref, rank_ref, cnt_ref, carry_ref):
    i = pl.program_id(0)

    @pl.when(i == 0)
    def _():
        carry_ref[...] = jnp.zeros_like(carry_ref)

    ids = ids_ref[...]
    tm = ids.shape[0]
    lane = lax.broadcasted_iota(jnp.int32, ids.shape, 1)
    cols = [jnp.sum(jnp.where(lane == k, ids, 0), axis=1, keepdims=True) for k in range(TOP_K)]
    mask = jnp.zeros(ids.shape, F32)
    for k in range(TOP_K):
        mask = mask + (lane == cols[k]).astype(F32)
    r = lax.broadcasted_iota(jnp.int32, (tm, tm), 0)
    c = lax.broadcasted_iota(jnp.int32, (tm, tm), 1)
    before = jnp.dot((r > c).astype(BF16), mask.astype(BF16), preferred_element_type=F32)
    pos = before + carry_ref[0:1]
    rank = jnp.zeros(ids.shape, F32)
    for k in range(TOP_K):
        rk = jnp.sum(jnp.where(lane == cols[k], pos, 0.0), axis=1, keepdims=True)
        rank = jnp.where(lane == k, rk, rank)
    rank_ref[...] = rank.astype(jnp.int32)
    total = carry_ref[0:1] + jnp.sum(mask, axis=0, keepdims=True)
    carry_ref[...] = jnp.broadcast_to(total, carry_ref.shape)
    cnt_ref[...] = jnp.broadcast_to(total, cnt_ref.shape).astype(jnp.int32)


def _expert_ranks(ids):
    n = ids.shape[0]
    tm = min(256, n)
    return pl.pallas_call(
        _rank_kernel,
        out_shape=(jax.ShapeDtypeStruct((n, LANE), jnp.int32), jax.ShapeDtypeStruct((8, LANE), jnp.int32)),
        grid=(n // tm,),
        in_specs=[pl.BlockSpec((tm, LANE), lambda i: (i, 0))],
        out_specs=(pl.BlockSpec((tm, LANE), lambda i: (i, 0)), pl.BlockSpec((8, LANE), lambda i: (0, 0))),
        scratch_shapes=[pltpu.VMEM((8, LANE), F32)],
        compiler_params=_cp(("arbitrary",)),
        name="expert_ranks",
    )(ids)


def _scatter_kernel(slot_ref, x_ref, xs_in_ref, xs_ref, sem):
    del xs_in_ref
    tm = x_ref.shape[0]

    def copy(r, k):
        return pltpu.make_async_copy(x_ref.at[pl.ds(r, 1)],
                                     xs_ref.at[pl.ds(slot_ref[0, 0, r * TOP_K + k], 1)], sem)

    def start(r, c):
        for k in range(TOP_K):
            copy(r, k).start()
        return c

    def wait(r, c):
        for k in range(TOP_K):
            copy(r, k).wait()
        return c

    lax.fori_loop(0, tm, start, 0)
    lax.fori_loop(0, tm, wait, 0)


def _scatter_rows(x, slots, xs):
    n = x.shape[0]
    tm = min(256, n)
    nt = n // tm
    return pl.pallas_call(
        _scatter_kernel,
        out_shape=jax.ShapeDtypeStruct(xs.shape, xs.dtype),
        grid=(nt,),
        in_specs=[pl.BlockSpec((1, 1, tm * TOP_K), lambda i: (i, 0, 0), memory_space=pltpu.SMEM),
                  pl.BlockSpec((tm, D_MODEL), lambda i: (i, 0)),
                  pl.BlockSpec(memory_space=pl.ANY)],
        out_specs=pl.BlockSpec(memory_space=pl.ANY),
        scratch_shapes=[pltpu.SemaphoreType.DMA(())],
        input_output_aliases={2: 0},
        compiler_params=_cp(("arbitrary",)),
        name="moe_scatter",
    )(slots.reshape(nt, 1, tm * TOP_K), x, xs)


def _moe_up_kernel(be_ref, nu_ref, x_ref, wg_ref, wu_ref, bg_ref, bu_ref, o_ref, wgb_ref, wub_ref):
    b = pl.program_id(1)
    used = b < nu_ref[0]
    changed = (b == 0) | (be_ref[b] != be_ref[jnp.maximum(b - 1, 0)])

    @pl.when(used & changed)
    def _():
        wgb_ref[...] = wg_ref[0].astype(BF16)
        wub_ref[...] = wu_ref[0].astype(BF16)

    @pl.when(used)
    def _():
        x = x_ref[...].astype(BF16)
        g = jnp.dot(x, wgb_ref[...], preferred_element_type=F32) + bg_ref[0]
        u = jnp.dot(x, wub_ref[...], preferred_element_type=F32) + bu_ref[0]
        g = jnp.minimum(g, SWIGLU_LIMIT)
        u = jnp.clip(u, -SWIGLU_LIMIT, SWIGLU_LIMIT)
        o_ref[...] = ((u + 1.0) * g * _sigmoid(SWIGLU_ALPHA * g)).astype(o_ref.dtype)

    @pl.when(jnp.logical_not(used))
    def _():
        o_ref[...] = jnp.zeros_like(o_ref)


def _moe_up(xs, blk_e, n_used, w_gu, b_gu, tf=1024):
    n_blocks = xs.shape[0] // MOE_BLK
    nf = D_FF // tf
    grid_spec = pltpu.PrefetchScalarGridSpec(
        num_scalar_prefetch=2,
        grid=(nf, n_blocks),
        in_specs=[pl.BlockSpec((MOE_BLK, D_MODEL), lambda j, b, be, nu: (b, 0)),
                  pl.BlockSpec((1, D_MODEL, tf), lambda j, b, be, nu: (be[b], 0, j)),
                  pl.BlockSpec((1, D_MODEL, tf), lambda j, b, be, nu: (be[b], 0, nf + j)),
                  pl.BlockSpec((1, 1, tf), lambda j, b, be, nu: (be[b], 0, j)),
                  pl.BlockSpec((1, 1, tf), lambda j, b, be, nu: (be[b], 0, nf + j))],
        out_specs=pl.BlockSpec((MOE_BLK, tf), lambda j, b, be, nu: (b, j)),
        scratch_shapes=[pltpu.VMEM((D_MODEL, tf), BF16), pltpu.VMEM((D_MODEL, tf), BF16)])
    return pl.pallas_call(
        _moe_up_kernel,
        out_shape=jax.ShapeDtypeStruct((xs.shape[0], D_FF), BF16),
        grid_spec=grid_spec,
        compiler_params=_cp(("arbitrary", "arbitrary")),
        name="moe_up",
    )(blk_e, n_used, xs, w_gu, w_gu, b_gu.reshape(N_EXPERTS, 1, 2 * D_FF),
      b_gu.reshape(N_EXPERTS, 1, 2 * D_FF))


def _moe_down_kernel(be_ref, nu_ref, h_ref, w_ref, b_ref, o_ref, wb_ref):
    b = pl.program_id(1)
    used = b < nu_ref[0]
    changed = (b == 0) | (be_ref[b] != be_ref[jnp.maximum(b - 1, 0)])

    @pl.when(used & changed)
    def _():
        wb_ref[...] = w_ref[0].astype(BF16)

    @pl.when(used)
    def _():
        o_ref[...] = jnp.dot(h_ref[...], wb_ref[...], preferred_element_type=F32) + b_ref[0]

    @pl.when(jnp.logical_not(used))
    def _():
        o_ref[...] = jnp.zeros_like(o_ref)


def _moe_down(hid, blk_e, n_used, w_down, b_down, tn=1024):
    n_blocks = hid.shape[0] // MOE_BLK
    grid_spec = pltpu.PrefetchScalarGridSpec(
        num_scalar_prefetch=2,
        grid=(D_MODEL // tn, n_blocks),
        in_specs=[pl.BlockSpec((MOE_BLK, D_FF), lambda j, b, be, nu: (b, 0)),
                  pl.BlockSpec((1, D_FF, tn), lambda j, b, be, nu: (be[b], 0, j)),
                  pl.BlockSpec((1, 1, tn), lambda j, b, be, nu: (be[b], 0, j))],
        out_specs=pl.BlockSpec((MOE_BLK, tn), lambda j, b, be, nu: (b, j)),
        scratch_shapes=[pltpu.VMEM((D_FF, tn), BF16)])
    return pl.pallas_call(
        _moe_down_kernel,
        out_shape=jax.ShapeDtypeStruct((hid.shape[0], D_MODEL), F32),
        grid_spec=grid_spec,
        compiler_params=_cp(("arbitrary", "arbitrary")),
        name="moe_down",
    )(blk_e, n_used, hid, w_down, b_down.reshape(N_EXPERTS, 1, D_MODEL))


def _combine_kernel(slot_ref, h_ref, gate_ref, gn_ref, y_ref, o_ref, buf_ref, sem):
    tm = h_ref.shape[0]

    def copy(r, k):
        return pltpu.make_async_copy(y_ref.at[pl.ds(slot_ref[0, 0, r * TOP_K + k], 1)],
                                     buf_ref.at[k, pl.ds(r, 1)], sem)

    def start(r, c):
        for k in range(TOP_K):
            copy(r, k).start()
        return c

    def wait(r, c):
        for k in range(TOP_K):
            copy(r, k).wait()
        return c

    lax.fori_loop(0, tm, start, 0)
    lax.fori_loop(0, tm, wait, 0)
    gate = gate_ref[...]
    acc = h_ref[...]
    for k in range(TOP_K):
        acc = acc + gate[:, k:k + 1] * buf_ref[k]
    o_ref[...] = acc * lax.rsqrt(jnp.mean(acc * acc, axis=-1, keepdims=True) + EPS) * gn_ref[...]


def _combine(h, gate, slots, y, norm_g):
    n = h.shape[0]
    tm = min(256, n)
    nt = n // tm
    return pl.pallas_call(
        _combine_kernel,
        out_shape=jax.ShapeDtypeStruct((n, D_MODEL), F32),
        grid=(nt,),
        in_specs=[pl.BlockSpec((1, 1, tm * TOP_K), lambda i: (i, 0, 0), memory_space=pltpu.SMEM),
                  pl.BlockSpec((tm, D_MODEL), lambda i: (i, 0)),
                  pl.BlockSpec((tm, LANE), lambda i: (i, 0)),
                  pl.BlockSpec((1, D_MODEL), lambda i: (0, 0)),
                  pl.BlockSpec(memory_space=pl.ANY)],
        out_specs=pl.BlockSpec((tm, D_MODEL), lambda i: (i, 0)),
        scratch_shapes=[pltpu.VMEM((TOP_K, tm, D_MODEL), F32), pltpu.SemaphoreType.DMA(())],
        compiler_params=_cp(("arbitrary",)),
        name="moe_combine",
    )(slots.reshape(nt, 1, tm * TOP_K), h, gate, norm_g.reshape(1, D_MODEL), y)


def _moe_plan(ids_p, ids_s):
    n_p, n_s = ids_p.shape[0], ids_s.shape[0]
    pad = (-n_s) % 256
    ids = jnp.concatenate([ids_p, ids_s, jnp.full((pad, LANE), -1, jnp.int32)], axis=0)
    rank, cnt = _expert_ranks(ids)
    counts = cnt[0, :N_EXPERTS]
    padded = (counts + MOE_BLK - 1) // MOE_BLK * MOE_BLK
    pad_end = jnp.cumsum(padded)
    pad_start = pad_end - padded
    n_pairs = (n_p + n_s) * TOP_K
    n_blocks = -(-n_pairs // MOE_BLK) + N_EXPERTS
    blk_e = jnp.minimum(jnp.searchsorted(pad_end, jnp.arange(n_blocks) * MOE_BLK, side='right'),
                        N_EXPERTS - 1).astype(jnp.int32)
    n_used = (pad_end[-1:] // MOE_BLK).astype(jnp.int32)
    e = ids[:n_p + n_s, :TOP_K]
    slot = (pad_start[e] + rank[:n_p + n_s, :TOP_K]).astype(jnp.int32)
    return slot[:n_p], slot[n_p:], blk_e, n_used, n_blocks


def _moe(hn_p, hn_s, h_p, h_s, ids_p, gate_p, ids_s, gate_s, w_gu, b_gu, w_down, b_down, norm_final):
    slot_p, slot_s, blk_e, n_used, n_blocks = _moe_plan(ids_p, ids_s)
    xs = jnp.zeros((n_blocks * MOE_BLK, D_MODEL), F32)
    xs = _scatter_rows(hn_p, slot_p, xs)
    xs = _scatter_rows(hn_s, slot_s, xs)
    hid = _moe_up(xs, blk_e, n_used, w_gu, b_gu)
    y = _moe_down(hid, blk_e, n_used, w_down, b_down)
    return (_combine(h_p, gate_p, slot_p, y, norm_final),
            _combine(h_s, gate_s, slot_s, y, norm_final))


def _proj_sample_kernel(x_ref, g_ref, w_ref, o_ref):
    x = x_ref[...]
    xn = x * lax.rsqrt(jnp.mean(x * x, axis=-1, keepdims=True) + EPS) * g_ref[...]
    o_ref[...] = jnp.dot(xn, w_ref[...], precision=HIGHEST, preferred_element_type=F32)


def _proj_sample(x, norm_g, w_in):
    n = x.shape[0]
    tn = 1024
    return pl.pallas_call(
        _proj_sample_kernel,
        out_shape=jax.ShapeDtypeStruct((n, N_IN), F32),
        grid=(pl.cdiv(N_IN, tn),),
        in_specs=[pl.BlockSpec((n, D_MODEL), lambda j: (0, 0)),
                  pl.BlockSpec((1, D_MODEL), lambda j: (0, 0)),
                  pl.BlockSpec((D_MODEL, tn), lambda j: (0, j))],
        out_specs=pl.BlockSpec((n, tn), lambda j: (0, j)),
        compiler_params=_cp(("parallel",)),
        name="proj_sample",
    )(x, norm_g.reshape(1, D_MODEL), w_in)


def _gdn_sample_kernel(ext_ref, cw_ref, z_ref, br_ref, ar_ref, al_ref, dt_ref, ng_ref, s_ref,
                       o_ref, so_ref):
    H = GDN_HEADS
    y = ext_ref[0, 0] * cw_ref[0]
    for w in range(1, CONV_W):
        y = y + ext_ref[0, w] * cw_ref[w]
    y = _silu(y)
    q = _l2norm(y[0:H]) * (GDN_DK ** -0.5)
    k = _l2norm(y[H:2 * H])
    v = y[2 * H:3 * H]
    beta = _sigmoid(br_ref[0])
    eg = jnp.exp(-jnp.exp(al_ref[...]) * _softplus(ar_ref[0] + dt_ref[...]))
    qk = jnp.sum(q * k, axis=-1, keepdims=True)
    qT = q.T
    kT = k.T
    rows = []
    for h in range(H):
        S = s_ref[0, h]
        kc = kT[:, h:h + 1]
        qc = qT[:, h:h + 1]
        e = eg[h:h + 1]
        k_s = jnp.sum(S * kc, axis=0, keepdims=True)
        q_s = jnp.sum(S * qc, axis=0, keepdims=True)
        v_new = beta[h:h + 1] * (v[h:h + 1] - e * k_s)
        rows.append(e * q_s + qk[h:h + 1] * v_new)
        so_ref[0, h] = e * S + kc * v_new
    o = jnp.concatenate(rows, axis=0)
    on = o * lax.rsqrt(jnp.mean(o * o, axis=-1, keepdims=True) + EPS) * ng_ref[...]
    o_ref[0] = on * _silu(z_ref[0])


def _gdn_sample(ext, conv_w, z, b_raw, a_raw, a_log, dt_bias, norm_g, state):
    bsz = ext.shape[0]
    H = GDN_HEADS
    col = lambda a: a.reshape(bsz, H, 1)
    hcol = lambda a: a.astype(F32).reshape(H, 1)
    return pl.pallas_call(
        _gdn_sample_kernel,
        out_shape=(jax.ShapeDtypeStruct((bsz, H, GDN_DV), F32),
                   jax.ShapeDtypeStruct(state.shape, F32)),
        grid=(bsz,),
        in_specs=[pl.BlockSpec((1, CONV_W, 3 * H, LANE), lambda b: (b, 0, 0, 0)),
                  pl.BlockSpec((CONV_W, 3 * H, LANE), lambda b: (0, 0, 0)),
                  pl.BlockSpec((1, H, GDN_DV), lambda b: (b, 0, 0)),
                  pl.BlockSpec((1, H, 1), lambda b: (b, 0, 0)),
                  pl.BlockSpec((1, H, 1), lambda b: (b, 0, 0)),
                  pl.BlockSpec((H, 1), lambda b: (0, 0)),
                  pl.BlockSpec((H, 1), lambda b: (0, 0)),
                  pl.BlockSpec((1, GDN_DV), lambda b: (0, 0)),
                  pl.BlockSpec((1, H, GDN_DK, GDN_DV), lambda b: (b, 0, 0, 0))],
        out_specs=(pl.BlockSpec((1, H, GDN_DV), lambda b: (b, 0, 0)),
                   pl.BlockSpec((1, H, GDN_DK, GDN_DV), lambda b: (b, 0, 0, 0))),
        compiler_params=_cp(("parallel",)),
        name="gdn_sample",
    )(ext, conv_w.reshape(CONV_W, 3 * H, LANE), z.reshape(bsz, H, GDN_DV), col(b_raw), col(a_raw),
      hcol(a_log), hcol(dt_bias), norm_g.reshape(1, GDN_DV), state)


def _page_sums_kernel(pt_ref, x_ref, w_ref, o_ref):
    del pt_ref
    nch = PAGE_SIZE // CMP_STRIDE
    width = NSA_KV_HEADS * HEAD_DIM
    x = x_ref[0]
    for e in range(2):
        x3 = x[:, e * width:(e + 1) * width].reshape(nch, CMP_STRIDE, width)
        for half in range(2):
            o_ref[0, 2 * e + half] = jnp.sum(x3 * w_ref[2 * e + half][None], axis=1)


def _page_sums(kv_pool3, page_table, w_tiled):
    bsz, n_pages = page_table.shape
    nch = PAGE_SIZE // CMP_STRIDE
    width = NSA_KV_HEADS * HEAD_DIM
    grid_spec = pltpu.PrefetchScalarGridSpec(
        num_scalar_prefetch=1,
        grid=(bsz, n_pages),
        in_specs=[pl.BlockSpec((1, PAGE_SIZE, 2 * width), lambda b, p, pt: (pt[b * n_pages + p], 0, 0)),
                  pl.BlockSpec((4, CMP_STRIDE, width), lambda b, p, pt: (0, 0, 0))],
        out_specs=pl.BlockSpec((1, 4, nch, width), lambda b, p, pt: (b, 0, p, 0)))
    return pl.pallas_call(
        _page_sums_kernel,
        out_shape=jax.ShapeDtypeStruct((bsz, 4, n_pages * nch, width), F32),
        grid_spec=grid_spec,
        compiler_params=_cp(("parallel", "arbitrary")),
        name="nsa_page_sums",
    )(page_table.reshape(-1), kv_pool3, w_tiled)


def _row_to_col(r):
    n = r.shape[1]
    eye = lax.broadcasted_iota(jnp.int32, (n, n), 0) == lax.broadcasted_iota(jnp.int32, (n, n), 1)
    return jnp.sum(jnp.where(eye, jnp.broadcast_to(r, (n, n)), 0), axis=1, keepdims=True)


def _hdot(a, b):
    return jnp.dot(a, b, precision=HIGHEST, preferred_element_type=F32)


def _hdot_nt(a, b):
    return lax.dot_general(a, b, (((1,), (1,)), ((), ())), precision=HIGHEST,
                           preferred_element_type=F32)


def _nsa_sample_select_kernel(sums_ref, new_ref, w_ref, proj_ref, q_ref, ocmp_ref, idx_ref, *, past):
    G = NSA_KV_HEADS
    nch = sums_ref.shape[2]
    row = lax.broadcasted_iota(jnp.int32, (nch, G * HEAD_DIM), 0)
    pooled = []
    for e in range(2):
        new_s1 = new_ref[0, e:e + 1] * w_ref[2 * e + 1, 0:1]
        nxt = jnp.where(row < nch - 1, pltpu.roll(sums_ref[0, 2 * e + 1], nch - 1, 0), new_s1)
        pooled.append(sums_ref[0, 2 * e] + nxt)
    n_slc = past // SEL_BLOCK + 1
    nsp = -(-n_slc // LANE) * LANE
    t = past
    cmp_end = CMP_STRIDE * lax.broadcasted_iota(jnp.int32, (1, nch), 1) + (CMP_LEN - 1)
    ci = CMP_STRIDE * lax.broadcasted_iota(jnp.int32, (nch, nsp), 0)
    bj = SEL_BLOCK * lax.broadcasted_iota(jnp.int32, (nch, nsp), 1)
    cmap = ((ci < bj + SEL_BLOCK) & (ci + CMP_LEN > bj)).astype(F32)
    j_idx = lax.broadcasted_iota(jnp.int32, (1, nsp), 1)
    jt = t // SEL_BLOCK
    force = (j_idx == 0) | (j_idx == jt) | (j_idx == jt - 1)
    i_idx = lax.broadcasted_iota(jnp.int32, (nsp, nsp), 0)
    jj_idx = lax.broadcasted_iota(jnp.int32, (nsp, nsp), 1)
    lane = lax.broadcasted_iota(jnp.int32, (1, LANE), 1)
    idx_rows = []
    for g in range(G):
        cols = slice(g * HEAD_DIM, (g + 1) * HEAD_DIM)
        ck = _hdot(pooled[0][:, cols], proj_ref[0])
        cv = _hdot(pooled[1][:, cols], proj_ref[1])
        q = q_ref[0, g] * (HEAD_DIM ** -0.5)
        p = _masked_softmax(_hdot_nt(q, ck), cmp_end <= t)
        ocmp_ref[0, g] = _hdot(p, cv)
        psum = jnp.sum(p[0:NSA_HPG], axis=0, keepdims=True)
        imp = _hdot(psum, cmap)
        score = jnp.where(j_idx > jt, NEG, jnp.where(force, FORCE, imp))
        sc = _row_to_col(score)
        ahead = (sc > score) | ((sc == score) & (i_idx < jj_idx))
        cnt = jnp.sum(ahead.astype(jnp.int32), axis=0, keepdims=True)
        sel = (cnt < N_SEL) & (score > NEG / 2)
        selc = _row_to_col(sel.astype(jnp.int32))
        before = jnp.sum(jnp.where(i_idx < jj_idx, selc, 0), axis=0, keepdims=True)
        out = jnp.full((1, LANE), -1, jnp.int32)
        for k in range(N_SEL):
            hit = sel & (before == k)
            jk = jnp.sum(jnp.where(hit, j_idx + 1, 0), axis=1, keepdims=True) - 1
            out = jnp.where(lane == k, jk, out)
        idx_rows.append(out)
    idx_ref[0] = jnp.concatenate(idx_rows + [jnp.full((8 - G, LANE), -1, jnp.int32)], axis=0)


def _nsa_sample_select(sums, new_cmp, w_tiled, cmp_proj, q8, past):
    bsz = sums.shape[0]
    G = NSA_KV_HEADS
    width = G * HEAD_DIM
    return pl.pallas_call(
        functools.partial(_nsa_sample_select_kernel, past=past),
        out_shape=(jax.ShapeDtypeStruct((bsz, G, 8, HEAD_DIM), F32),
                   jax.ShapeDtypeStruct((bsz, 8, LANE), jnp.int32)),
        grid=(bsz,),
        in_specs=[pl.BlockSpec((1,) + sums.shape[1:], lambda b: (b, 0, 0, 0)),
                  pl.BlockSpec((1, 2, width), lambda b: (b, 0, 0)),
                  pl.BlockSpec((4, CMP_STRIDE, width), lambda b: (0, 0, 0)),
                  pl.BlockSpec((2, HEAD_DIM, HEAD_DIM), lambda b: (0, 0, 0)),
                  pl.BlockSpec((1, G, 8, HEAD_DIM), lambda b: (b, 0, 0, 0))],
        out_specs=(pl.BlockSpec((1, G, 8, HEAD_DIM), lambda b: (b, 0, 0, 0)),
                   pl.BlockSpec((1, 8, LANE), lambda b: (b, 0, 0))),
        compiler_params=_cp(("parallel",)),
        name="nsa_sample_select",
    )(sums, new_cmp, w_tiled, cmp_proj, q8)


def _nsa_sample_attend_kernel(idx_ref, pt_ref, *refs, past, n_pages):
    del pt_ref
    kb = refs[0:N_SEL]
    vb = refs[N_SEL:2 * N_SEL]
    new_ref, kw_ref, vw_ref, q_ref, ocmp_ref, gate_ref, o_ref = refs[2 * N_SEL:]
    b = pl.program_id(0)
    g = pl.program_id(1)
    G = NSA_KV_HEADS
    t = past
    n_past_blk = past // SEL_BLOCK
    q = q_ref[0, 0] * (HEAD_DIM ** -0.5)
    k_new = new_ref[0, pl.ds(2 * G + g, 1)]
    v_new = new_ref[0, pl.ds(3 * G + g, 1)]
    kw_new = new_ref[0, pl.ds(4 * G + g, 1)]
    vw_new = new_ref[0, pl.ds(5 * G + g, 1)]

    r64 = lax.broadcasted_iota(jnp.int32, (SEL_BLOCK, HEAD_DIM), 0)
    ks, vs, kpos = [], [], []
    r_lane = lax.broadcasted_iota(jnp.int32, (1, SEL_BLOCK), 1)
    for k in range(N_SEL):
        j = idx_ref[(b * G + g) * N_SEL + k]
        is_new = j >= n_past_blk
        ks.append(jnp.where(is_new, jnp.where(r64 == 0, k_new, 0.0), kb[k][0]))
        vs.append(jnp.where(is_new, jnp.where(r64 == 0, v_new, 0.0), vb[k][0]))
        kpos.append(jnp.where(j >= 0, j * SEL_BLOCK + r_lane, t + 1))
    ksel = jnp.concatenate(ks, axis=0)
    vsel = jnp.concatenate(vs, axis=0)
    kpos = jnp.concatenate(kpos, axis=1)
    p = _masked_softmax(_hdot_nt(q, ksel), kpos <= t)
    o_slc = _hdot(p, vsel)

    L = kw_ref.shape[1]
    wpos = (past - L) + lax.broadcasted_iota(jnp.int32, (1, L), 1)
    ok = (wpos > t - WINDOW) & (wpos >= 0)
    s_w = jnp.where(ok, _hdot_nt(q, kw_ref[0]), NEG)
    s_n = jnp.sum(q * kw_new, axis=-1, keepdims=True)
    m = jnp.maximum(jnp.max(s_w, axis=-1, keepdims=True), s_n)
    p_w = jnp.exp(s_w - m) * ok.astype(F32)
    p_n = jnp.exp(s_n - m)
    den = jnp.maximum(jnp.sum(p_w, axis=-1, keepdims=True) + p_n, 1e-30)
    o_win = (_hdot(p_w, vw_ref[0]) + p_n * vw_new) / den

    gb = _sigmoid(gate_ref[0, 0])
    o_ref[0, 0] = ocmp_ref[0, 0] * gb[:, 0:1] + o_slc * gb[:, 1:2] + o_win * gb[:, 2:3]


def _nsa_sample_attend(idx, page_table, kv_pool3, new24, win3, q8, o_cmp, gate8, past):
    bsz, n_pages = page_table.shape
    G = NSA_KV_HEADS
    n_past_blk = past // SEL_BLOCK
    bpp = PAGE_SIZE // SEL_BLOCK
    L = win3.shape[1]

    def blk(kind, k):
        def imap(b, g, idx_ref, pt_ref):
            j = jnp.clip(idx_ref[(b * G + g) * N_SEL + k], 0, n_past_blk - 1)
            return (pt_ref[b * n_pages + j // bpp], j % bpp, kind * G + g)
        return pl.BlockSpec((1, SEL_BLOCK, HEAD_DIM), imap)

    per_bg = pl.BlockSpec((1, 1, 8, HEAD_DIM), lambda b, g, i, p: (b, g, 0, 0))
    grid_spec = pltpu.PrefetchScalarGridSpec(
        num_scalar_prefetch=2,
        grid=(bsz, G),
        in_specs=([blk(2, k) for k in range(N_SEL)] + [blk(3, k) for k in range(N_SEL)]
                  + [pl.BlockSpec((1, 6 * G, HEAD_DIM), lambda b, g, i, p: (b, 0, 0)),
                     pl.BlockSpec((1, L, HEAD_DIM), lambda b, g, i, p: (b, 0, g)),
                     pl.BlockSpec((1, L, HEAD_DIM), lambda b, g, i, p: (b, 0, G + g)),
                     per_bg, per_bg, per_bg]),
        out_specs=per_bg)
    return pl.pallas_call(
        functools.partial(_nsa_sample_attend_kernel, past=past, n_pages=n_pages),
        out_shape=jax.ShapeDtypeStruct((bsz, G, 8, HEAD_DIM), F32),
        grid_spec=grid_spec,
        compiler_params=_cp(("parallel", "arbitrary")),
        name="nsa_sample_attend",
    )(idx, page_table.reshape(-1), *([kv_pool3] * (2 * N_SEL)), new24, win3, win3, q8, o_cmp, gate8)


def _prompt_mixers(x2d, batch, seq, norm_mix, w_in, conv_w, a_log, dt_bias, gdn_norm, cmp_pos,
                   cmp_proj, w_out_a, w_out_b):
    w_main = jnp.concatenate([w_in[:, OFF_QKV:OFF_BRAW], w_in[:, OFF_MERGE:],
                              w_in[:, OFF_QB:OFF_GATEB]], axis=1).astype(BF16)
    w_small = jnp.concatenate([w_in[:, OFF_BRAW:OFF_QB], w_in[:, OFF_GATEB:OFF_MERGE],
                               jnp.zeros((D_MODEL, LANE - 2 * GDN_HEADS - 3 * NSA_HEADS), w_in.dtype)],
                              axis=1).astype(BF16)
    proj, small = _proj_prompt(x2d, norm_mix, w_main, w_small)
    o_a, s_new = _gdn_prompt(proj, small, conv_w, _head_params(a_log, dt_bias), gdn_norm, batch, seq)
    comp = _compress_prompt(proj, cmp_pos, cmp_proj, batch, seq)
    o_b = _nsa_prompt(proj, small, comp, batch, seq)
    y = _merge(o_a, o_b, w_out_a.astype(BF16), w_out_b.astype(BF16), proj, CB_MERGE,
               CB_MERGE + D_MODEL // LANE, False)
    G = NSA_KV_HEADS
    kv6 = proj[:, CB_KVB * LANE:].reshape(batch, seq, 6, G, HEAD_DIM)
    conv_new = proj[:, :CONV_DIM].reshape(batch, seq, CONV_DIM)[:, seq - (CONV_W - 1):]
    return y, kv6[:, :, :4], kv6[:, max(seq - WINDOW, 0):, 4:], s_new, conv_new


def _sample_mixers(x2d, norm_mix, w_in, conv_w, a_log, dt_bias, gdn_norm, cmp_pos, cmp_proj,
                   w_out_a, w_out_b, kv_pool, win_buf, state, conv_buf, page_table):
    bsz = x2d.shape[0]
    G = NSA_KV_HEADS
    past = page_table.shape[1] * PAGE_SIZE
    ps = _proj_sample(x2d, norm_mix, w_in)
    qkv, z = ps[:, OFF_QKV:OFF_Z], ps[:, OFF_Z:OFF_BRAW]
    b_raw, a_raw = ps[:, OFF_BRAW:OFF_ARAW], ps[:, OFF_ARAW:OFF_QB]
    q_b, kv_b = ps[:, OFF_QB:OFF_KVB], ps[:, OFF_KVB:OFF_GATEB]
    gate_b, merge_raw = ps[:, OFF_GATEB:OFF_MERGE], ps[:, OFF_MERGE:]

    ext = jnp.concatenate([conv_buf.astype(F32), qkv[:, None]], axis=1)
    o_a, s_new = _gdn_sample(ext.reshape(bsz, CONV_W, 3 * GDN_HEADS, LANE), conv_w, z, b_raw, a_raw,
                             a_log, dt_bias, gdn_norm, state.astype(F32))

    kv_pool3 = kv_pool.reshape(kv_pool.shape[0], PAGE_SIZE, 4 * G * HEAD_DIM)
    w_tiled = jnp.tile(cmp_pos.astype(F32).reshape(4, CMP_STRIDE, HEAD_DIM), (1, 1, G))
    sums = _page_sums(kv_pool3, page_table, w_tiled)
    q8 = jnp.pad(q_b.reshape(bsz, G, NSA_HPG, HEAD_DIM), ((0, 0), (0, 0), (0, 8 - NSA_HPG), (0, 0)))
    o_cmp, idx8 = _nsa_sample_select(sums, kv_b[:, :2 * G * HEAD_DIM].reshape(bsz, 2, G * HEAD_DIM),
                                     w_tiled, cmp_proj.astype(F32), q8, past)
    gate8 = jnp.pad(gate_b.reshape(bsz, G, NSA_HPG, 3),
                    ((0, 0), (0, 0), (0, 8 - NSA_HPG), (0, LANE - 3)))
    win_len = win_buf.shape[1]
    o_b8 = _nsa_sample_attend(idx8[:, :G, :N_SEL].reshape(-1), page_table, kv_pool3,
                              kv_b.reshape(bsz, 6 * G, HEAD_DIM),
                              win_buf.reshape(bsz, win_len, 2 * G * HEAD_DIM), q8, o_cmp, gate8, past)
    o_b = o_b8[:, :, :NSA_HPG].reshape(bsz, NSA_HEADS * HEAD_DIM)

    y = _merge(o_a.reshape(bsz, GDN_HEADS * GDN_DV), o_b, w_out_a, w_out_b, merge_raw, 0,
               D_MODEL // LANE, True)
    kv6 = kv_b.reshape(bsz, 1, 6, G, HEAD_DIM)
    win_ext = jnp.concatenate([win_buf.astype(F32), kv6[:, :, 4:]], axis=1)
    keep = min(WINDOW, past + 1)
    return y, kv6[:, :, :4], win_ext[:, win_ext.shape[1] - keep:], s_new, ext[:, 1:]


def kernel(x_prompt, x_sample, cache_nsa_kv, cache_nsa_win, state_gdn, state_conv, page_table, norm_mix, w_in, conv_w, gdn_a_log, gdn_dt_bias, gdn_norm, w_out_a, cmp_pos, cmp_proj, w_out_b, w_out, norm_ffn, w_router, b_router, w_gu, b_gu, w_down, b_down, norm_final):
    assert w_in.shape[0] == 1 and x_sample.shape[1] == 1, "one layer, one new token per sequence"
    bp, seq, _ = x_prompt.shape
    bs = x_sample.shape[0]
    xp = x_prompt.reshape(bp * seq, D_MODEL)
    xs = x_sample.reshape(bs, D_MODEL)
    l = 0
    y_p, kv_p, win_p, gdn_p, conv_p = _prompt_mixers(
        xp, bp, seq, norm_mix[l], w_in[l], conv_w[l], gdn_a_log[l], gdn_dt_bias[l], gdn_norm[l],
        cmp_pos[l], cmp_proj[l], w_out_a[l], w_out_b[l])
    y_s, kv_s, win_s, gdn_s, conv_s = _sample_mixers(
        xs, norm_mix[l], w_in[l], conv_w[l], gdn_a_log[l], gdn_dt_bias[l], gdn_norm[l], cmp_pos[l],
        cmp_proj[l], w_out_a[l], w_out_b[l], cache_nsa_kv[l], cache_nsa_win[l], state_gdn[l],
        state_conv[l], page_table)

    wr = jnp.pad(w_router[l].astype(F32), ((0, 0), (0, LANE - N_EXPERTS)))
    br = jnp.pad(b_router[l].astype(F32), (0, LANE - N_EXPERTS)).reshape(1, LANE)
    h_p, hn_p, ids_p, gate_p = _outproj_router(xp, y_p, w_out[l].astype(BF16), norm_ffn[l], wr, br, False)
    h_s, hn_s, ids_s, gate_s = _outproj_router(xs, y_s, w_out[l], norm_ffn[l], wr, br, True)
    out_p, out_s = _moe(hn_p, hn_s, h_p, h_s, ids_p, gate_p, ids_s, gate_s, w_gu[l], b_gu[l],
                        w_down[l], b_down[l], norm_final)
    return (out_p.reshape(x_prompt.shape), out_s.reshape(x_sample.shape),
            kv_p[None], kv_s[None], win_p[None], win_s[None],
            gdn_p[None].astype(state_gdn.dtype), gdn_s[None].astype(state_gdn.dtype),
            conv_p[None], conv_s[None])
```

```python
import functools
import math

import jax
import jax.numpy as jnp
from jax import lax
from jax.experimental import pallas as pl
from jax.experimental.pallas import tpu as pltpu

F32 = jnp.float32
BF16 = jnp.bfloat16
HIGHEST = lax.Precision.HIGHEST

D_MODEL = 2048
GDN_HEADS = 16
GDN_DK = 128
GDN_DV = 128
CONV_W = 4
CONV_DIM = GDN_HEADS * (2 * GDN_DK + GDN_DV)
NSA_HEADS = 16
NSA_KV_HEADS = 4
HEAD_DIM = 128
NSA_HPG = NSA_HEADS // NSA_KV_HEADS
CMP_LEN = 32
CMP_STRIDE = 16
SEL_BLOCK = 64
N_SEL = 16
WINDOW = 512
PAGE_SIZE = 128
N_EXPERTS = 32
TOP_K = 4
D_FF = 2048
SWIGLU_LIMIT = 7.0
SWIGLU_ALPHA = 1.702
EPS = 1e-6
NEG = -1e30
FORCE = 1e6

IN_SPLITS = (CONV_DIM, GDN_HEADS * GDN_DV, GDN_HEADS, GDN_HEADS, NSA_HEADS * HEAD_DIM,
             6 * NSA_KV_HEADS * HEAD_DIM, 3 * NSA_HEADS, 2 * D_MODEL)
_OFF = [0]
for _s in IN_SPLITS:
    _OFF.append(_OFF[-1] + _s)
(OFF_QKV, OFF_Z, OFF_BRAW, OFF_ARAW, OFF_QB, OFF_KVB, OFF_GATEB, OFF_MERGE, N_IN) = _OFF

LANE = 128
CB_Q, CB_K, CB_V = 0, GDN_HEADS, 2 * GDN_HEADS
CB_Z = CONV_DIM // LANE
CB_MERGE = CB_Z + GDN_HEADS
CB_QB = CB_MERGE + 2 * D_MODEL // LANE
CB_KVB = CB_QB + NSA_HEADS
N_MAIN = (CB_KVB + 6 * NSA_KV_HEADS) * LANE
SM_BRAW, SM_ARAW, SM_GATE = 0, GDN_HEADS, 2 * GDN_HEADS

GDN_CHUNK = 128
GDN_HB = 4
NSA_TQ = 128
NSA_KT = 512
MOE_BLK = 256
VMEM_LIMIT = 56 * 1024 * 1024


def _cp(sem, vmem=VMEM_LIMIT):
    return pltpu.CompilerParams(dimension_semantics=sem, vmem_limit_bytes=vmem)


def _lane_pick(x, lane):
    ids = lax.broadcasted_iota(jnp.int32, x.shape, 1)
    return jnp.sum(jnp.where(ids == lane, x, 0.0), axis=1, keepdims=True)


def _sigmoid(x):
    return 1.0 / (1.0 + jnp.exp(-x))


def _silu(x):
    return x * _sigmoid(x)


def _softplus(x):
    return jnp.maximum(x, 0.0) + jnp.log(1.0 + jnp.exp(-jnp.abs(x)))


def _proj_kernel(x_ref, g_ref, w_ref, ws_ref, o_ref, os_ref, xn_ref):
    @pl.when(pl.program_id(1) == 0)
    def _():
        x = x_ref[...]
        ms = jnp.mean(x * x, axis=-1, keepdims=True)
        xn = (x * lax.rsqrt(ms + EPS) * g_ref[...]).astype(BF16)
        xn_ref[...] = xn
        os_ref[...] = jnp.dot(xn, ws_ref[...], preferred_element_type=F32)

    o_ref[...] = jnp.dot(xn_ref[...], w_ref[...], preferred_element_type=F32)


def _proj_prompt(x, norm_g, w_main, w_small):
    n = x.shape[0]
    tm = min(1024, n)
    tn = 512
    return pl.pallas_call(
        _proj_kernel,
        out_shape=(jax.ShapeDtypeStruct((n, N_MAIN), F32), jax.ShapeDtypeStruct((n, LANE), F32)),
        grid=(n // tm, N_MAIN // tn),
        in_specs=[pl.BlockSpec((tm, D_MODEL), lambda i, j: (i, 0)),
                  pl.BlockSpec((1, D_MODEL), lambda i, j: (0, 0)),
                  pl.BlockSpec((D_MODEL, tn), lambda i, j: (0, j)),
                  pl.BlockSpec((D_MODEL, LANE), lambda i, j: (0, 0))],
        out_specs=(pl.BlockSpec((tm, tn), lambda i, j: (i, j)),
                   pl.BlockSpec((tm, LANE), lambda i, j: (i, 0))),
        scratch_shapes=[pltpu.VMEM((tm, D_MODEL), BF16)],
        compiler_params=_cp(("parallel", "arbitrary")),
        name="proj_prompt",
    )(x, norm_g.reshape(1, D_MODEL), w_main, w_small)


def _shift_rows(x, prev8, s):
    xs = pltpu.roll(x, s, 0)
    ps = pltpu.roll(prev8, s, 0)
    row = lax.broadcasted_iota(jnp.int32, prev8.shape, 0)
    top = jnp.where(row < s, ps, xs[0:8])
    return jnp.concatenate([top, xs[8:]], axis=0)


def _conv_silu(x, prev8, cw):
    y = x * cw[CONV_W - 1:CONV_W]
    for s in range(1, CONV_W):
        y = y + _shift_rows(x, prev8, s) * cw[CONV_W - 1 - s:CONV_W - s]
    return _silu(y)


def _l2norm(x):
    return x * lax.rsqrt(jnp.sum(x * x, axis=-1, keepdims=True) + EPS)


def _cumsum_rows(x):
    c = x.shape[0]
    row = lax.broadcasted_iota(jnp.int32, x.shape, 0)
    s = 1
    while s < c:
        x = x + jnp.where(row >= s, pltpu.roll(x, s, 0), 0.0)
        s *= 2
    return x


def _bdot(a, b):
    return jnp.dot(a.astype(BF16), b.astype(BF16), preferred_element_type=F32)


def _bdot_nt(a, b):
    return lax.dot_general(a.astype(BF16), b.astype(BF16), (((1,), (1,)), ((), ())),
                           preferred_element_type=F32)


def _bdot_tn(a, b):
    return lax.dot_general(a.astype(BF16), b.astype(BF16), (((0,), (0,)), ((), ())),
                           preferred_element_type=F32)


INV_BASE = 16


def _unit_lower_inverse(a, ri, ci):
    c = a.shape[0]
    same = lambda s: (ri // s) == (ci // s)
    d = jnp.where(same(INV_BASE), a, 0.0)
    t = jnp.where(ri == ci, 1.0, 0.0) - d
    p = d
    n = 2
    while n < INV_BASE:
        p = _bdot(p, p)
        t = t + _bdot(t, p)
        n *= 2
    s = INV_BASE
    while s < c:
        m = jnp.where(same(2 * s) & jnp.logical_not(same(s)), a, 0.0)
        t = t - _bdot(_bdot(t, m), t)
        s *= 2
    return t


def _gdn_kernel(q_ref, k_ref, v_ref, z_ref, sm_ref, cwq_ref, cwk_ref, cwv_ref, hp_ref, ng_ref,
                o_ref, s_out_ref, s_ref, pq_ref, pk_ref, pv_ref):
    hg = pl.program_id(1)
    c = pl.program_id(2)
    C = GDN_CHUNK

    @pl.when(c == 0)
    def _():
        s_ref[...] = jnp.zeros_like(s_ref)
        pq_ref[...] = jnp.zeros_like(pq_ref)
        pk_ref[...] = jnp.zeros_like(pk_ref)
        pv_ref[...] = jnp.zeros_like(pv_ref)

    xq, xk, xv = q_ref[...], k_ref[...], v_ref[...]
    yq = _conv_silu(xq, pq_ref[...], cwq_ref[...])
    yk = _conv_silu(xk, pk_ref[...], cwk_ref[...])
    yv = _conv_silu(xv, pv_ref[...], cwv_ref[...])
    pq_ref[...] = xq[C - 8:]
    pk_ref[...] = xk[C - 8:]
    pv_ref[...] = xv[C - 8:]

    sm = sm_ref[...]
    hp = hp_ref[...]
    g_all = -jnp.exp(hp[0:1]) * _softplus(sm + hp[1:2])
    G_all = _cumsum_rows(g_all)
    ri = lax.broadcasted_iota(jnp.int32, (C, C), 0)
    ci = lax.broadcasted_iota(jnp.int32, (C, C), 1)
    z = _silu(z_ref[...])
    s_news = []
    for i in range(GDN_HB):
        h = hg * GDN_HB + i
        cols = slice(i * LANE, (i + 1) * LANE)
        q = _l2norm(yq[:, cols]) * (GDN_DK ** -0.5)
        k = _l2norm(yk[:, cols])
        v = yv[:, cols]
        beta = _sigmoid(_lane_pick(sm, SM_BRAW + h))
        G = _lane_pick(G_all, SM_ARAW + h)
        g_last = G[C - 1:C]
        Gb = jnp.broadcast_to(G, (C, C))
        decay = jnp.exp(jnp.where(ri >= ci, Gb - Gb.T, NEG))

        kb = k * beta
        a_mat = jnp.where(ri > ci, _bdot_nt(kb, k) * decay, 0.0)
        rhs = jnp.concatenate([v * beta, kb * jnp.exp(G)], axis=1)
        x = _bdot(_unit_lower_inverse(a_mat, ri, ci), rhs)
        u, w = x[:, :GDN_DV], x[:, GDN_DV:]

        S = s_ref[i]
        v_new = u - _bdot(w, S)
        qk = _bdot_nt(q, k) * decay
        o = _bdot(q * jnp.exp(G), S) + _bdot(qk, v_new)
        s_new = S * jnp.exp(g_last) + _bdot_tn(k * jnp.exp(g_last - G), v_new)
        s_ref[i] = s_new
        s_news.append(s_new)
        on = o * lax.rsqrt(jnp.mean(o * o, axis=-1, keepdims=True) + EPS) * ng_ref[...]
        o_ref[:, cols] = (on * z[:, cols]).astype(o_ref.dtype)

    @pl.when(c == pl.num_programs(2) - 1)
    def _():
        for i in range(GDN_HB):
            s_out_ref[0, i] = s_news[i]


def _head_params(a_log, dt_bias):
    hp = jnp.zeros((8, LANE), F32)
    hp = hp.at[0, SM_ARAW:SM_ARAW + GDN_HEADS].set(a_log.astype(F32))
    return hp.at[1, SM_ARAW:SM_ARAW + GDN_HEADS].set(dt_bias.astype(F32))


def _gdn_prompt(proj, small, conv_w, head_par, norm_g, batch, seq):
    C = GDN_CHUNK
    nc = seq // C
    H = GDN_HEADS
    HB = GDN_HB
    W = HB * LANE

    def rows(cb):
        return pl.BlockSpec((C, W), lambda b, h, c, cb=cb: (b * nc + c, cb // HB + h))

    def cw(cb):
        return pl.BlockSpec((CONV_W, W), lambda b, h, c, cb=cb: (0, cb // HB + h))

    return pl.pallas_call(
        _gdn_kernel,
        out_shape=(jax.ShapeDtypeStruct((batch * seq, H * GDN_DV), BF16),
                   jax.ShapeDtypeStruct((batch, H, GDN_DK, GDN_DV), F32)),
        grid=(batch, H // HB, nc),
        in_specs=[rows(CB_Q), rows(CB_K), rows(CB_V), rows(CB_Z),
                  pl.BlockSpec((C, LANE), lambda b, h, c: (b * nc + c, 0)),
                  cw(CB_Q), cw(CB_K), cw(CB_V),
                  pl.BlockSpec((8, LANE), lambda b, h, c: (0, 0)),
                  pl.BlockSpec((1, GDN_DV), lambda b, h, c: (0, 0))],
        out_specs=(pl.BlockSpec((C, W), lambda b, h, c: (b * nc + c, h)),
                   pl.BlockSpec((1, HB, GDN_DK, GDN_DV), lambda b, h, c: (b, h, 0, 0))),
        scratch_shapes=[pltpu.VMEM((HB, GDN_DK, GDN_DV), F32), pltpu.VMEM((8, W), F32),
                        pltpu.VMEM((8, W), F32), pltpu.VMEM((8, W), F32)],
        compiler_params=_cp(("parallel", "parallel", "arbitrary")),
        name="gdn_prompt",
    )(proj, proj, proj, proj, small, conv_w, conv_w, conv_w, head_par, norm_g.reshape(1, GDN_DV))


def _cmp_kernel(x_ref, w_ref, p_ref, o_ref):
    t = x_ref.shape[0]
    nch = t // CMP_STRIDE
    x3 = x_ref[...].reshape(nch, CMP_STRIDE, HEAD_DIM)
    w = w_ref[0]
    s0 = jnp.sum(x3 * w[None, :CMP_STRIDE], axis=1)
    s1 = jnp.sum(x3 * w[None, CMP_STRIDE:], axis=1)
    row = lax.broadcasted_iota(jnp.int32, (nch, HEAD_DIM), 0)
    pooled = jnp.where(row < nch - 1, s0 + pltpu.roll(s1, nch - 1, 0), 0.0)
    o_ref[0, 0, 0] = jnp.dot(pooled, p_ref[0], precision=HIGHEST, preferred_element_type=F32)


def _compress_prompt(proj, cmp_pos, cmp_proj, batch, seq):
    nch = seq // CMP_STRIDE
    return pl.pallas_call(
        _cmp_kernel,
        out_shape=jax.ShapeDtypeStruct((batch, 2, NSA_KV_HEADS, nch, HEAD_DIM), F32),
        grid=(batch, 2, NSA_KV_HEADS),
        in_specs=[pl.BlockSpec((seq, LANE), lambda b, e, g: (b, CB_KVB + e * NSA_KV_HEADS + g)),
                  pl.BlockSpec((1, CMP_LEN, HEAD_DIM), lambda b, e, g: (e, 0, 0)),
                  pl.BlockSpec((1, HEAD_DIM, HEAD_DIM), lambda b, e, g: (e, 0, 0))],
        out_specs=pl.BlockSpec((1, 1, 1, nch, HEAD_DIM), lambda b, e, g: (b, e, g, 0, 0)),
        compiler_params=_cp(("parallel", "parallel", "parallel")),
        name="nsa_compress_prompt",
    )(proj, cmp_pos, cmp_proj)


def _masked_softmax(s, mask):
    s = jnp.where(mask, s, NEG)
    p = jnp.exp(s - jnp.max(s, axis=-1, keepdims=True)) * mask.astype(F32)
    return p / jnp.maximum(jnp.sum(p, axis=-1, keepdims=True), 1e-30)


def _select_blocks_t(score_t):
    nj = score_t.shape[0]
    sub = 8
    groups = [score_t[v:v + sub] for v in range(0, nj, sub)]
    j_in = lax.broadcasted_iota(jnp.int32, groups[0].shape, 0)
    cnts = [jnp.zeros(g.shape, jnp.int32) for g in groups]
    for jp in range(nj):
        row = score_t[jp:jp + 1, :]
        for v, g in enumerate(groups):
            if v * sub + sub - 1 <= jp:
                ahead = jnp.where(row > g, 1, 0)
            elif v * sub > jp:
                ahead = jnp.where(row >= g, 1, 0)
            else:
                ahead = jnp.where(j_in + v * sub > jp, jnp.where(row >= g, 1, 0),
                                  jnp.where(row > g, 1, 0))
            cnts[v] = cnts[v] + ahead
    cnt = jnp.concatenate(cnts, axis=0)
    return (cnt < N_SEL) & (score_t > NEG / 2)


def _nsa_kernel(q0_ref, q1_ref, q2_ref, q3_ref, sm_ref, ck_ref, cv_ref, ks_ref, vs_ref, kw_ref,
                vw_ref, o_ref, acc_ref, m_ref, l_ref):
    g = pl.program_id(1)
    qi = pl.program_id(2)
    TQ = q0_ref.shape[0]
    R4 = NSA_HPG * TQ
    T = ks_ref.shape[0]
    KT = min(NSA_KT, T)
    qs = qi * TQ
    q4 = jnp.concatenate([q0_ref[...], q1_ref[...], q2_ref[...], q3_ref[...]], axis=0)
    q4 = (q4 * (HEAD_DIM ** -0.5)).astype(BF16)
    t4 = qs + lax.rem(lax.broadcasted_iota(jnp.int32, (R4, 1), 0), TQ)
    t1 = qs + lax.broadcasted_iota(jnp.int32, (TQ, 1), 0)

    ck = ck_ref[0, 0, 0]
    nc = ck.shape[0]
    s = _bdot_nt(q4, ck)
    cmp_end = CMP_STRIDE * lax.broadcasted_iota(jnp.int32, (1, nc), 1) + (CMP_LEN - 1)
    p = _masked_softmax(s, cmp_end <= t4)
    o_cmp = _bdot(p, cv_ref[0, 0, 0])

    n_slc = T // SEL_BLOCK
    nsp = max(n_slc, LANE)
    psum = p[0:TQ]
    for j in range(1, NSA_HPG):
        psum = psum + p[j * TQ:(j + 1) * TQ]
    ci = CMP_STRIDE * lax.broadcasted_iota(jnp.int32, (nc, nsp), 0)
    bj = SEL_BLOCK * lax.broadcasted_iota(jnp.int32, (nc, nsp), 1)
    cmap = ((ci < bj + SEL_BLOCK) & (ci + CMP_LEN > bj)).astype(F32)
    imp = jnp.dot(psum, cmap, precision=HIGHEST, preferred_element_type=F32)
    imp_t = imp.T[:n_slc]
    j_idx = lax.broadcasted_iota(jnp.int32, (n_slc, TQ), 0)
    jt = lax.shift_right_logical(qs + lax.broadcasted_iota(jnp.int32, (1, TQ), 1),
                                 int(math.log2(SEL_BLOCK)))
    force = (j_idx == 0) | (j_idx == jt) | (j_idx == jt - 1)
    score_t = jnp.where(j_idx > jt, NEG, jnp.where(force, FORCE, imp_t))
    sel_t = jnp.where(_select_blocks_t(score_t), 1.0, 0.0)

    acc_ref[...] = jnp.zeros_like(acc_ref)
    m_ref[...] = jnp.full_like(m_ref, NEG)
    l_ref[...] = jnp.zeros_like(l_ref)
    n_kt = (qs + TQ - 1) // KT + 1

    def slc_body(kt, carry):
        k0 = pl.multiple_of(kt * KT, KT)
        kb = ks_ref[pl.ds(k0, KT), :]
        vb = vs_ref[pl.ds(k0, KT), :]
        s = _bdot_nt(q4, kb)
        kpos = k0 + lax.broadcasted_iota(jnp.int32, (1, KT), 1)
        eb = lax.broadcasted_iota(jnp.int32, (n_slc, KT), 0) == lax.shift_right_logical(
            k0 + lax.broadcasted_iota(jnp.int32, (n_slc, KT), 1), int(math.log2(SEL_BLOCK)))
        selx = _bdot_tn(sel_t, jnp.where(eb, 1.0, 0.0))
        bias = jnp.where((selx > 0.5) & (kpos <= t1), 0.0, NEG)
        s = s + jnp.concatenate([bias] * NSA_HPG, axis=0)
        m_old = m_ref[...]
        m_new = jnp.maximum(m_old, jnp.max(s, axis=-1, keepdims=True))
        pe = jnp.exp(s - m_new)
        alpha = jnp.exp(m_old - m_new)
        l_ref[...] = alpha * l_ref[...] + jnp.sum(pe, axis=-1, keepdims=True)
        acc_ref[...] = alpha * acc_ref[...] + _bdot(pe, vb)
        m_ref[...] = m_new
        return carry

    lax.fori_loop(0, n_kt, slc_body, 0)
    o_slc = acc_ref[...] / jnp.maximum(l_ref[...], 1e-30)

    WK = min(WINDOW + TQ, T)
    w0 = pl.multiple_of(jnp.minimum(jnp.maximum(qs - WINDOW, 0), T - WK), TQ)
    s = _bdot_nt(q4, kw_ref[pl.ds(w0, WK), :])
    wpos = w0 + lax.broadcasted_iota(jnp.int32, (1, WK), 1)
    p = _masked_softmax(s, (wpos <= t4) & (wpos > t4 - WINDOW))
    o_win = _bdot(p, vw_ref[pl.ds(w0, WK), :])

    gates = _sigmoid(sm_ref[...])
    for j in range(NSA_HPG):
        lane0 = SM_GATE + (g * NSA_HPG + j) * 3
        rows = slice(j * TQ, (j + 1) * TQ)
        ob = (o_cmp[rows] * _lane_pick(gates, lane0) + o_slc[rows] * _lane_pick(gates, lane0 + 1)
              + o_win[rows] * _lane_pick(gates, lane0 + 2))
        o_ref[:, j * HEAD_DIM:(j + 1) * HEAD_DIM] = ob.astype(o_ref.dtype)


def _nsa_prompt(proj, small, comp, batch, seq):
    TQ = min(NSA_TQ, seq)
    nq = seq // TQ
    nch = seq // CMP_STRIDE
    G = NSA_KV_HEADS

    def qspec(j):
        return pl.BlockSpec((TQ, LANE), lambda b, g, i, j=j: (b * nq + i, CB_QB + g * NSA_HPG + j))

    def kvspec(e):
        return pl.BlockSpec((seq, LANE), lambda b, g, i, e=e: (b, CB_KVB + e * G + g))

    def cspec(e):
        return pl.BlockSpec((1, 1, 1, nch, HEAD_DIM), lambda b, g, i, e=e: (b, e, g, 0, 0))

    R4 = NSA_HPG * TQ
    return pl.pallas_call(
        _nsa_kernel,
        out_shape=jax.ShapeDtypeStruct((batch * seq, NSA_HEADS * HEAD_DIM), BF16),
        grid=(batch, G, nq),
        in_specs=[qspec(0), qspec(1), qspec(2), qspec(3),
                  pl.BlockSpec((TQ, LANE), lambda b, g, i: (b * nq + i, 0)),
                  cspec(0), cspec(1), kvspec(2), kvspec(3), kvspec(4), kvspec(5)],
        out_specs=pl.BlockSpec((TQ, NSA_HPG * HEAD_DIM), lambda b, g, i: (b * nq + i, g)),
        scratch_shapes=[pltpu.VMEM((R4, HEAD_DIM), F32), pltpu.VMEM((R4, 1), F32),
                        pltpu.VMEM((R4, 1), F32)],
        compiler_params=_cp(("parallel", "parallel", "arbitrary")),
        name="nsa_prompt",
    )(proj, proj, proj, proj, small, comp, comp, proj, proj, proj, proj)


def _dot(a, b, exact):
    if exact:
        return jnp.dot(a.astype(F32), b.astype(F32), precision=HIGHEST, preferred_element_type=F32)
    return jnp.dot(a.astype(BF16), b.astype(BF16), preferred_element_type=F32)


def _merge_kernel(oa_ref, ob_ref, wa_ref, wb_ref, m0_ref, m1_ref, y_ref, *, exact):
    ya = _dot(oa_ref[...], wa_ref[...], exact)
    yb = _dot(ob_ref[...], wb_ref[...], exact)
    y = _sigmoid(m0_ref[...]) * ya + _sigmoid(m1_ref[...]) * yb
    y_ref[...] = y.astype(y_ref.dtype)


def _merge(o_a, o_b, w_a, w_b, gates, m0_cb, m1_cb, exact):
    n = o_a.shape[0]
    tm = min(1024, n)
    tn = 512
    cpb = tn // LANE
    return pl.pallas_call(
        functools.partial(_merge_kernel, exact=exact),
        out_shape=jax.ShapeDtypeStruct((n, D_MODEL), F32 if exact else BF16),
        grid=(n // tm, D_MODEL // tn),
        in_specs=[pl.BlockSpec((tm, D_MODEL), lambda i, j: (i, 0)),
                  pl.BlockSpec((tm, D_MODEL), lambda i, j: (i, 0)),
                  pl.BlockSpec((D_MODEL, tn), lambda i, j: (0, j)),
                  pl.BlockSpec((D_MODEL, tn), lambda i, j: (0, j)),
                  pl.BlockSpec((tm, tn), lambda i, j: (i, m0_cb // cpb + j)),
                  pl.BlockSpec((tm, tn), lambda i, j: (i, m1_cb // cpb + j))],
        out_specs=pl.BlockSpec((tm, tn), lambda i, j: (i, j)),
        compiler_params=_cp(("parallel", "parallel")),
        name="merge_exact" if exact else "merge",
    )(o_a, o_b, w_a, w_b, gates, gates)


def _top4(logits):
    lane = lax.broadcasted_iota(jnp.int32, logits.shape, 1)
    ids = jnp.full(logits.shape, -1, jnp.int32)
    vals = jnp.full(logits.shape, NEG, F32)
    cur = logits
    for k in range(TOP_K):
        m = jnp.max(cur, axis=1, keepdims=True)
        idx = jnp.min(jnp.where(cur == m, lane, LANE), axis=1, keepdims=True)
        ids = jnp.where(lane == k, idx, ids)
        vals = jnp.where(lane == k, m, vals)
        cur = jnp.where(lane == idx, 2 * NEG, cur)
    e = jnp.where(lane < TOP_K, jnp.exp(vals - jnp.max(vals, axis=1, keepdims=True)), 0.0)
    return ids, e / jnp.sum(e, axis=1, keepdims=True)


def _outproj_kernel(x_ref, y_ref, w_ref, gn_ref, wr_ref, br_ref, h_ref, hn_ref, ids_ref, gate_ref,
                    *, exact):
    h = x_ref[...] + _dot(y_ref[...], w_ref[...], exact)
    h_ref[...] = h
    hn = h * lax.rsqrt(jnp.mean(h * h, axis=-1, keepdims=True) + EPS) * gn_ref[...]
    hn_ref[...] = hn
    logits = jnp.dot(hn, wr_ref[...], precision=HIGHEST, preferred_element_type=F32) + br_ref[...]
    lane = lax.broadcasted_iota(jnp.int32, logits.shape, 1)
    ids, gate = _top4(jnp.where(lane < N_EXPERTS, logits, NEG))
    ids_ref[...] = ids
    gate_ref[...] = gate


def _outproj_router(x, y, w_out, norm_g, w_router, b_router, exact):
    n = x.shape[0]
    tm = min(256, n)
    row = lambda i: (i, 0)
    fixed = lambda i: (0, 0)
    return pl.pallas_call(
        functools.partial(_outproj_kernel, exact=exact),
        out_shape=(jax.ShapeDtypeStruct((n, D_MODEL), F32), jax.ShapeDtypeStruct((n, D_MODEL), F32),
                   jax.ShapeDtypeStruct((n, LANE), jnp.int32), jax.ShapeDtypeStruct((n, LANE), F32)),
        grid=(n // tm,),
        in_specs=[pl.BlockSpec((tm, D_MODEL), row), pl.BlockSpec((tm, D_MODEL), row),
                  pl.BlockSpec((D_MODEL, D_MODEL), fixed), pl.BlockSpec((1, D_MODEL), fixed),
                  pl.BlockSpec((D_MODEL, LANE), fixed), pl.BlockSpec((1, LANE), fixed)],
        out_specs=(pl.BlockSpec((tm, D_MODEL), row), pl.BlockSpec((tm, D_MODEL), row),
                   pl.BlockSpec((tm, LANE), row), pl.BlockSpec((tm, LANE), row)),
        compiler_params=_cp(("parallel",)),
        name="outproj_router_exact" if exact else "outproj_router",
    )(x, y, w_out, norm_g.reshape(1, D_MODEL), w_router, b_router)


def _rank_kernel(ids_ref, rank_ref, cnt_ref, carry_ref):
    i = pl.program_id(0)

    @pl.when(i == 0)
    def _():
        carry_ref[...] = jnp.zeros_like(carry_ref)

    ids = ids_ref[...]
    tm = ids.shape[0]
    lane = lax.broadcasted_iota(jnp.int32, ids.shape, 1)
    cols = [jnp.sum(jnp.where(lane == k, ids, 0), axis=1, keepdims=True) for k in range(TOP_K)]
    mask = jnp.zeros(ids.shape, F32)
    for k in range(TOP_K):
        mask = mask + (lane == cols[k]).astype(F32)
    r = lax.broadcasted_iota(jnp.int32, (tm, tm), 0)
    c = lax.broadcasted_iota(jnp.int32, (tm, tm), 1)
    before = jnp.dot((r > c).astype(BF16), mask.astype(BF16), preferred_element_type=F32)
    pos = before + carry_ref[0:1]
    rank = jnp.zeros(ids.shape, F32)
    for k in range(TOP_K):
        rk = jnp.sum(jnp.where(lane == cols[k], pos, 0.0), axis=1, keepdims=True)
        rank = jnp.where(lane == k, rk, rank)
    rank_ref[...] = rank.astype(jnp.int32)
    total = carry_ref[0:1] + jnp.sum(mask, axis=0, keepdims=True)
    carry_ref[...] = jnp.broadcast_to(total, carry_ref.shape)
    cnt_ref[...] = jnp.broadcast_to(total, cnt_ref.shape).astype(jnp.int32)


def _expert_ranks(ids):
    n = ids.shape[0]
    tm = min(256, n)
    return pl.pallas_call(
        _rank_kernel,
        out_shape=(jax.ShapeDtypeStruct((n, LANE), jnp.int32), jax.ShapeDtypeStruct((8, LANE), jnp.int32)),
        grid=(n // tm,),
        in_specs=[pl.BlockSpec((tm, LANE), lambda i: (i, 0))],
        out_specs=(pl.BlockSpec((tm, LANE), lambda i: (i, 0)), pl.BlockSpec((8, LANE), lambda i: (0, 0))),
        scratch_shapes=[pltpu.VMEM((8, LANE), F32)],
        compiler_params=_cp(("arbitrary",)),
        name="expert_ranks",
    )(ids)


def _scatter_kernel(slot_ref, x_ref, xs_in_ref, xs_ref, sem):
    del xs_in_ref
    tm = x_ref.shape[0]

    def copy(r, k):
        return pltpu.make_async_copy(x_ref.at[pl.ds(r, 1)],
                                     xs_ref.at[pl.ds(slot_ref[0, 0, r * TOP_K + k], 1)], sem)

    def start(r, c):
        for k in range(TOP_K):
            copy(r, k).start()
        return c

    def wait(r, c):
        for k in range(TOP_K):
            copy(r, k).wait()
        return c

    lax.fori_loop(0, tm, start, 0)
    lax.fori_loop(0, tm, wait, 0)


def _scatter_rows(x, slots, xs):
    n = x.shape[0]
    tm = min(256, n)
    nt = n // tm
    return pl.pallas_call(
        _scatter_kernel,
        out_shape=jax.ShapeDtypeStruct(xs.shape, xs.dtype),
        grid=(nt,),
        in_specs=[pl.BlockSpec((1, 1, tm * TOP_K), lambda i: (i, 0, 0), memory_space=pltpu.SMEM),
                  pl.BlockSpec((tm, D_MODEL), lambda i: (i, 0)),
                  pl.BlockSpec(memory_space=pl.ANY)],
        out_specs=pl.BlockSpec(memory_space=pl.ANY),
        scratch_shapes=[pltpu.SemaphoreType.DMA(())],
        input_output_aliases={2: 0},
        compiler_params=_cp(("arbitrary",)),
        name="moe_scatter",
    )(slots.reshape(nt, 1, tm * TOP_K), x, xs)


def _moe_up_kernel(be_ref, nu_ref, x_ref, wg_ref, wu_ref, bg_ref, bu_ref, o_ref, wgb_ref, wub_ref):
    b = pl.program_id(1)
    used = b < nu_ref[0]
    changed = (b == 0) | (be_ref[b] != be_ref[jnp.maximum(b - 1, 0)])

    @pl.when(used & changed)
    def _():
        wgb_ref[...] = wg_ref[0].astype(BF16)
        wub_ref[...] = wu_ref[0].astype(BF16)

    @pl.when(used)
    def _():
        x = x_ref[...].astype(BF16)
        g = jnp.dot(x, wgb_ref[...], preferred_element_type=F32) + bg_ref[0]
        u = jnp.dot(x, wub_ref[...], preferred_element_type=F32) + bu_ref[0]
        g = jnp.minimum(g, SWIGLU_LIMIT)
        u = jnp.clip(u, -SWIGLU_LIMIT, SWIGLU_LIMIT)
        o_ref[...] = ((u + 1.0) * g * _sigmoid(SWIGLU_ALPHA * g)).astype(o_ref.dtype)

    @pl.when(jnp.logical_not(used))
    def _():
        o_ref[...] = jnp.zeros_like(o_ref)


def _moe_up(xs, blk_e, n_used, w_gu, b_gu, tf=1024):
    n_blocks = xs.shape[0] // MOE_BLK
    nf = D_FF // tf
    grid_spec = pltpu.PrefetchScalarGridSpec(
        num_scalar_prefetch=2,
        grid=(nf, n_blocks),
        in_specs=[pl.BlockSpec((MOE_BLK, D_MODEL), lambda j, b, be, nu: (b, 0)),
                  pl.BlockSpec((1, D_MODEL, tf), lambda j, b, be, nu: (be[b], 0, j)),
                  pl.BlockSpec((1, D_MODEL, tf), lambda j, b, be, nu: (be[b], 0, nf + j)),
                  pl.BlockSpec((1, 1, tf), lambda j, b, be, nu: (be[b], 0, j)),
                  pl.BlockSpec((1, 1, tf), lambda j, b, be, nu: (be[b], 0, nf + j))],
        out_specs=pl.BlockSpec((MOE_BLK, tf), lambda j, b, be, nu: (b, j)),
        scratch_shapes=[pltpu.VMEM((D_MODEL, tf), BF16), pltpu.VMEM((D_MODEL, tf), BF16)])
    return pl.pallas_call(
        _moe_up_kernel,
        out_shape=jax.ShapeDtypeStruct((xs.shape[0], D_FF), BF16),
        grid_spec=grid_spec,
        compiler_params=_cp(("arbitrary", "arbitrary")),
        name="moe_up",
    )(blk_e, n_used, xs, w_gu, w_gu, b_gu.reshape(N_EXPERTS, 1, 2 * D_FF),
      b_gu.reshape(N_EXPERTS, 1, 2 * D_FF))


def _moe_down_kernel(be_ref, nu_ref, h_ref, w_ref, b_ref, o_ref, wb_ref):
    b = pl.program_id(1)
    used = b < nu_ref[0]
    changed = (b == 0) | (be_ref[b] != be_ref[jnp.maximum(b - 1, 0)])

    @pl.when(used & changed)
    def _():
        wb_ref[...] = w_ref[0].astype(BF16)

    @pl.when(used)
    def _():
        o_ref[...] = jnp.dot(h_ref[...], wb_ref[...], preferred_element_type=F32) + b_ref[0]

    @pl.when(jnp.logical_not(used))
    def _():
        o_ref[...] = jnp.zeros_like(o_ref)


def _moe_down(hid, blk_e, n_used, w_down, b_down, tn=1024):
    n_blocks = hid.shape[0] // MOE_BLK
    grid_spec = pltpu.PrefetchScalarGridSpec(
        num_scalar_prefetch=2,
        grid=(D_MODEL // tn, n_blocks),
        in_specs=[pl.BlockSpec((MOE_BLK, D_FF), lambda j, b, be, nu: (b, 0)),
                  pl.BlockSpec((1, D_FF, tn), lambda j, b, be, nu: (be[b], 0, j)),
                  pl.BlockSpec((1, 1, tn), lambda j, b, be, nu: (be[b], 0, j))],
        out_specs=pl.BlockSpec((MOE_BLK, tn), lambda j, b, be, nu: (b, j)),
        scratch_shapes=[pltpu.VMEM((D_FF, tn), BF16)])
    return pl.pallas_call(
        _moe_down_kernel,
        out_shape=jax.ShapeDtypeStruct((hid.shape[0], D_MODEL), F32),
        grid_spec=grid_spec,
        compiler_params=_cp(("arbitrary", "arbitrary")),
        name="moe_down",
    )(blk_e, n_used, hid, w_down, b_down.reshape(N_EXPERTS, 1, D_MODEL))


def _combine_kernel(slot_ref, h_ref, gate_ref, gn_ref, y_ref, o_ref, buf_ref, sem):
    tm = h_ref.shape[0]

    def copy(r, k):
        return pltpu.make_async_copy(y_ref.at[pl.ds(slot_ref[0, 0, r * TOP_K + k], 1)],
                                     buf_ref.at[k, pl.ds(r, 1)], sem)

    def start(r, c):
        for k in range(TOP_K):
            copy(r, k).start()
        return c

    def wait(r, c):
        for k in range(TOP_K):
            copy(r, k).wait()
        return c

    lax.fori_loop(0, tm, start, 0)
    lax.fori_loop(0, tm, wait, 0)
    gate = gate_ref[...]
    acc = h_ref[...]
    for k in range(TOP_K):
        acc = acc + gate[:, k:k + 1] * buf_ref[k]
    o_ref[...] = acc * lax.rsqrt(jnp.mean(acc * acc, axis=-1, keepdims=True) + EPS) * gn_ref[...]


def _combine(h, gate, slots, y, norm_g):
    n = h.shape[0]
    tm = min(256, n)
    nt = n // tm
    return pl.pallas_call(
        _combine_kernel,
        out_shape=jax.ShapeDtypeStruct((n, D_MODEL), F32),
        grid=(nt,),
        in_specs=[pl.BlockSpec((1, 1, tm * TOP_K), lambda i: (i, 0, 0), memory_space=pltpu.SMEM),
                  pl.BlockSpec((tm, D_MODEL), lambda i: (i, 0)),
                  pl.BlockSpec((tm, LANE), lambda i: (i, 0)),
                  pl.BlockSpec((1, D_MODEL), lambda i: (0, 0)),
                  pl.BlockSpec(memory_space=pl.ANY)],
        out_specs=pl.BlockSpec((tm, D_MODEL), lambda i: (i, 0)),
        scratch_shapes=[pltpu.VMEM((TOP_K, tm, D_MODEL), F32), pltpu.SemaphoreType.DMA(())],
        compiler_params=_cp(("arbitrary",)),
        name="moe_combine",
    )(slots.reshape(nt, 1, tm * TOP_K), h, gate, norm_g.reshape(1, D_MODEL), y)


def _moe_plan(ids_p, ids_s):
    n_p, n_s = ids_p.shape[0], ids_s.shape[0]
    pad = (-n_s) % 256
    ids = jnp.concatenate([ids_p, ids_s, jnp.full((pad, LANE), -1, jnp.int32)], axis=0)
    rank, cnt = _expert_ranks(ids)
    counts = cnt[0, :N_EXPERTS]
    padded = (counts + MOE_BLK - 1) // MOE_BLK * MOE_BLK
    pad_end = jnp.cumsum(padded)
    pad_start = pad_end - padded
    n_pairs = (n_p + n_s) * TOP_K
    n_blocks = -(-n_pairs // MOE_BLK) + N_EXPERTS
    blk_start = jnp.arange(n_blocks, dtype=jnp.int32) * MOE_BLK
    blk_e = jnp.minimum(jnp.sum((pad_end[None, :] <= blk_start[:, None]).astype(jnp.int32), axis=1),
                        N_EXPERTS - 1)
    n_used = (pad_end[-1:] // MOE_BLK).astype(jnp.int32)
    e = ids[:n_p + n_s, :TOP_K]
    slot = (pad_start[e] + rank[:n_p + n_s, :TOP_K]).astype(jnp.int32)
    return slot[:n_p], slot[n_p:], blk_e, n_used, n_blocks


def _moe(hn_p, hn_s, h_p, h_s, ids_p, gate_p, ids_s, gate_s, w_gu, b_gu, w_down, b_down, norm_final):
    slot_p, slot_s, blk_e, n_used, n_blocks = _moe_plan(ids_p, ids_s)
    xs = jnp.zeros((n_blocks * MOE_BLK, D_MODEL), F32)
    xs = _scatter_rows(hn_p, slot_p, xs)
    xs = _scatter_rows(hn_s, slot_s, xs)
    hid = _moe_up(xs, blk_e, n_used, w_gu, b_gu)
    y = _moe_down(hid, blk_e, n_used, w_down, b_down)
    return (_combine(h_p, gate_p, slot_p, y, norm_final),
            _combine(h_s, gate_s, slot_s, y, norm_final))


def _proj_sample_kernel(x_ref, g_ref, w_ref, o_ref):
    x = x_ref[...]
    xn = x * lax.rsqrt(jnp.mean(x * x, axis=-1, keepdims=True) + EPS) * g_ref[...]
    o_ref[...] = jnp.dot(xn, w_ref[...], precision=HIGHEST, preferred_element_type=F32)


def _proj_sample(x, norm_g, w_in):
    n = x.shape[0]
    tn = 1024
    return pl.pallas_call(
        _proj_sample_kernel,
        out_shape=jax.ShapeDtypeStruct((n, N_IN), F32),
        grid=(pl.cdiv(N_IN, tn),),
        in_specs=[pl.BlockSpec((n, D_MODEL), lambda j: (0, 0)),
                  pl.BlockSpec((1, D_MODEL), lambda j: (0, 0)),
                  pl.BlockSpec((D_MODEL, tn), lambda j: (0, j))],
        out_specs=pl.BlockSpec((n, tn), lambda j: (0, j)),
        compiler_params=_cp(("parallel",)),
        name="proj_sample",
    )(x, norm_g.reshape(1, D_MODEL), w_in)


def _gdn_sample_kernel(ext_ref, cw_ref, z_ref, br_ref, ar_ref, al_ref, dt_ref, ng_ref, s_ref,
                       o_ref, so_ref):
    H = GDN_HEADS
    y = ext_ref[0, 0] * cw_ref[0]
    for w in range(1, CONV_W):
        y = y + ext_ref[0, w] * cw_ref[w]
    y = _silu(y)
    q = _l2norm(y[0:H]) * (GDN_DK ** -0.5)
    k = _l2norm(y[H:2 * H])
    v = y[2 * H:3 * H]
    beta = _sigmoid(br_ref[0])
    eg = jnp.exp(-jnp.exp(al_ref[...]) * _softplus(ar_ref[0] + dt_ref[...]))
    qk = jnp.sum(q * k, axis=-1, keepdims=True)
    qT = q.T
    kT = k.T
    rows = []
    for h in range(H):
        S = s_ref[0, h]
        kc = kT[:, h:h + 1]
        qc = qT[:, h:h + 1]
        e = eg[h:h + 1]
        k_s = jnp.sum(S * kc, axis=0, keepdims=True)
        q_s = jnp.sum(S * qc, axis=0, keepdims=True)
        v_new = beta[h:h + 1] * (v[h:h + 1] - e * k_s)
        rows.append(e * q_s + qk[h:h + 1] * v_new)
        so_ref[0, h] = e * S + kc * v_new
    o = jnp.concatenate(rows, axis=0)
    on = o * lax.rsqrt(jnp.mean(o * o, axis=-1, keepdims=True) + EPS) * ng_ref[...]
    o_ref[0] = on * _silu(z_ref[0])


def _gdn_sample(ext, conv_w, z, b_raw, a_raw, a_log, dt_bias, norm_g, state):
    bsz = ext.shape[0]
    H = GDN_HEADS
    col = lambda a: a.reshape(bsz, H, 1)
    hcol = lambda a: a.astype(F32).reshape(H, 1)
    return pl.pallas_call(
        _gdn_sample_kernel,
        out_shape=(jax.ShapeDtypeStruct((bsz, H, GDN_DV), F32),
                   jax.ShapeDtypeStruct(state.shape, F32)),
        grid=(bsz,),
        in_specs=[pl.BlockSpec((1, CONV_W, 3 * H, LANE), lambda b: (b, 0, 0, 0)),
                  pl.BlockSpec((CONV_W, 3 * H, LANE), lambda b: (0, 0, 0)),
                  pl.BlockSpec((1, H, GDN_DV), lambda b: (b, 0, 0)),
                  pl.BlockSpec((1, H, 1), lambda b: (b, 0, 0)),
                  pl.BlockSpec((1, H, 1), lambda b: (b, 0, 0)),
                  pl.BlockSpec((H, 1), lambda b: (0, 0)),
                  pl.BlockSpec((H, 1), lambda b: (0, 0)),
                  pl.BlockSpec((1, GDN_DV), lambda b: (0, 0)),
                  pl.BlockSpec((1, H, GDN_DK, GDN_DV), lambda b: (b, 0, 0, 0))],
        out_specs=(pl.BlockSpec((1, H, GDN_DV), lambda b: (b, 0, 0)),
                   pl.BlockSpec((1, H, GDN_DK, GDN_DV), lambda b: (b, 0, 0, 0))),
        compiler_params=_cp(("parallel",)),
        name="gdn_sample",
    )(ext, conv_w.reshape(CONV_W, 3 * H, LANE), z.reshape(bsz, H, GDN_DV), col(b_raw), col(a_raw),
      hcol(a_log), hcol(dt_bias), norm_g.reshape(1, GDN_DV), state)


PAGES_PER_STEP = 8


def _page_sums_kernel(pt_ref, *refs):
    del pt_ref
    x_refs, w_ref, o_ref = refs[:-2], refs[-2], refs[-1]
    nch = PAGE_SIZE // CMP_STRIDE
    for i, x_ref in enumerate(x_refs):
        for e in range(2):
            for g in range(NSA_KV_HEADS):
                x3 = x_ref[0, 0, :, e, g, :].reshape(nch, CMP_STRIDE, HEAD_DIM)
                for half in range(2):
                    o_ref[0, 2 * e + half, g, i * nch:(i + 1) * nch, :] = jnp.sum(
                        x3 * w_ref[2 * e + half][None], axis=1)


def _page_sums(kv_cache, layer, page_table, w4):
    bsz, n_pages = page_table.shape
    G = NSA_KV_HEADS
    nch = PAGE_SIZE // CMP_STRIDE
    pps = math.gcd(PAGES_PER_STEP, n_pages)

    def page(i):
        return pl.BlockSpec((1, 1, PAGE_SIZE, 2, G, HEAD_DIM),
                            lambda b, p, pt: (layer, pt[b * n_pages + p * pps + i], 0, 0, 0, 0))

    grid_spec = pltpu.PrefetchScalarGridSpec(
        num_scalar_prefetch=1,
        grid=(bsz, n_pages // pps),
        in_specs=[page(i) for i in range(pps)]
        + [pl.BlockSpec((4, CMP_STRIDE, HEAD_DIM), lambda b, p, pt: (0, 0, 0))],
        out_specs=pl.BlockSpec((1, 4, G, pps * nch, HEAD_DIM), lambda b, p, pt: (b, 0, 0, p, 0)))
    return pl.pallas_call(
        _page_sums_kernel,
        out_shape=jax.ShapeDtypeStruct((bsz, 4, G, n_pages * nch, HEAD_DIM), F32),
        grid_spec=grid_spec,
        compiler_params=_cp(("parallel", "arbitrary")),
        name="nsa_page_sums",
    )(page_table.reshape(-1), *([kv_cache] * pps), w4)


def _row_to_col(r):
    n = r.shape[1]
    eye = lax.broadcasted_iota(jnp.int32, (n, n), 0) == lax.broadcasted_iota(jnp.int32, (n, n), 1)
    return jnp.sum(jnp.where(eye, jnp.broadcast_to(r, (n, n)), 0), axis=1, keepdims=True)


def _hdot(a, b):
    return jnp.dot(a, b, precision=HIGHEST, preferred_element_type=F32)


def _hdot_nt(a, b):
    return lax.dot_general(a, b, (((1,), (1,)), ((), ())), precision=HIGHEST,
                           preferred_element_type=F32)


def _nsa_sample_select_kernel(sums_ref, new_ref, w_ref, proj_ref, q_ref, ocmp_ref, idx_ref, *, past):
    G = NSA_KV_HEADS
    nch = sums_ref.shape[3]
    row = lax.broadcasted_iota(jnp.int32, (nch, HEAD_DIM), 0)

    def pooled(e, g):
        new_s1 = new_ref[0, e, g:g + 1, :] * w_ref[2 * e + 1, 0:1]
        nxt = jnp.where(row < nch - 1, pltpu.roll(sums_ref[0, 2 * e + 1, g], nch - 1, 0), new_s1)
        return sums_ref[0, 2 * e, g] + nxt

    n_slc = past // SEL_BLOCK + 1
    nsp = -(-n_slc // LANE) * LANE
    t = past
    cmp_end = CMP_STRIDE * lax.broadcasted_iota(jnp.int32, (1, nch), 1) + (CMP_LEN - 1)
    ci = CMP_STRIDE * lax.broadcasted_iota(jnp.int32, (nch, nsp), 0)
    bj = SEL_BLOCK * lax.broadcasted_iota(jnp.int32, (nch, nsp), 1)
    cmap = ((ci < bj + SEL_BLOCK) & (ci + CMP_LEN > bj)).astype(F32)
    j_idx = lax.broadcasted_iota(jnp.int32, (1, nsp), 1)
    jt = t // SEL_BLOCK
    force = (j_idx == 0) | (j_idx == jt) | (j_idx == jt - 1)
    i_idx = lax.broadcasted_iota(jnp.int32, (nsp, nsp), 0)
    jj_idx = lax.broadcasted_iota(jnp.int32, (nsp, nsp), 1)
    lane = lax.broadcasted_iota(jnp.int32, (1, LANE), 1)
    idx_rows = []
    for g in range(G):
        ck = _hdot(pooled(0, g), proj_ref[0])
        cv = _hdot(pooled(1, g), proj_ref[1])
        q = q_ref[0, g] * (HEAD_DIM ** -0.5)
        p = _masked_softmax(_hdot_nt(q, ck), cmp_end <= t)
        ocmp_ref[0, g] = _hdot(p, cv)
        psum = jnp.sum(p[0:NSA_HPG], axis=0, keepdims=True)
        imp = _hdot(psum, cmap)
        score = jnp.where(j_idx > jt, NEG, jnp.where(force, FORCE, imp))
        sc = _row_to_col(score)
        ahead = (sc > score) | ((sc == score) & (i_idx < jj_idx))
        cnt = jnp.sum(ahead.astype(jnp.int32), axis=0, keepdims=True)
        sel = (cnt < N_SEL) & (score > NEG / 2)
        selc = _row_to_col(sel.astype(jnp.int32))
        before = jnp.sum(jnp.where(i_idx < jj_idx, selc, 0), axis=0, keepdims=True)
        out = jnp.full((1, LANE), -1, jnp.int32)
        for k in range(N_SEL):
            hit = sel & (before == k)
            jk = jnp.sum(jnp.where(hit, j_idx + 1, 0), axis=1, keepdims=True) - 1
            out = jnp.where(lane == k, jk, out)
        idx_rows.append(out)
    idx_ref[0] = jnp.concatenate(idx_rows + [jnp.full((8 - G, LANE), -1, jnp.int32)], axis=0)


def _nsa_sample_select(sums, new6, w4, cmp_proj, q8, past):
    bsz = sums.shape[0]
    G = NSA_KV_HEADS
    return pl.pallas_call(
        functools.partial(_nsa_sample_select_kernel, past=past),
        out_shape=(jax.ShapeDtypeStruct((bsz, G, 8, HEAD_DIM), F32),
                   jax.ShapeDtypeStruct((bsz, 8, LANE), jnp.int32)),
        grid=(bsz,),
        in_specs=[pl.BlockSpec((1,) + sums.shape[1:], lambda b: (b, 0, 0, 0, 0)),
                  pl.BlockSpec((1, 6, G, HEAD_DIM), lambda b: (b, 0, 0, 0)),
                  pl.BlockSpec((4, CMP_STRIDE, HEAD_DIM), lambda b: (0, 0, 0)),
                  pl.BlockSpec((2, HEAD_DIM, HEAD_DIM), lambda b: (0, 0, 0)),
                  pl.BlockSpec((1, G, 8, HEAD_DIM), lambda b: (b, 0, 0, 0))],
        out_specs=(pl.BlockSpec((1, G, 8, HEAD_DIM), lambda b: (b, 0, 0, 0)),
                   pl.BlockSpec((1, 8, LANE), lambda b: (b, 0, 0))),
        compiler_params=_cp(("parallel",)),
        name="nsa_sample_select",
    )(sums, new6, w4, cmp_proj, q8)


def _nsa_sample_attend_kernel(idx_ref, pt_ref, *refs, past, n_pages):
    del pt_ref
    kvb = refs[0:N_SEL]
    new_ref, win_ref, q_ref, ocmp_ref, gate_ref, o_ref = refs[N_SEL:]
    b = pl.program_id(0)
    g = pl.program_id(1)
    G = NSA_KV_HEADS
    t = past
    n_past_blk = past // SEL_BLOCK
    q = q_ref[0, 0] * (HEAD_DIM ** -0.5)
    k_new = new_ref[0, 2, pl.ds(g, 1), :]
    v_new = new_ref[0, 3, pl.ds(g, 1), :]
    kw_new = new_ref[0, 4, pl.ds(g, 1), :]
    vw_new = new_ref[0, 5, pl.ds(g, 1), :]

    r64 = lax.broadcasted_iota(jnp.int32, (SEL_BLOCK, HEAD_DIM), 0)
    ks, vs, kpos = [], [], []
    r_lane = lax.broadcasted_iota(jnp.int32, (1, SEL_BLOCK), 1)
    for k in range(N_SEL):
        j = idx_ref[(b * G + g) * N_SEL + k]
        is_new = j >= n_past_blk
        ks.append(jnp.where(is_new, jnp.where(r64 == 0, k_new, 0.0), kvb[k][0, 0, :, 0, g, :]))
        vs.append(jnp.where(is_new, jnp.where(r64 == 0, v_new, 0.0), kvb[k][0, 0, :, 1, g, :]))
        kpos.append(jnp.where(j >= 0, j * SEL_BLOCK + r_lane, t + 1))
    ksel = jnp.concatenate(ks, axis=0)
    vsel = jnp.concatenate(vs, axis=0)
    kpos = jnp.concatenate(kpos, axis=1)
    p = _masked_softmax(_hdot_nt(q, ksel), kpos <= t)
    o_slc = _hdot(p, vsel)

    L = win_ref.shape[2]
    wpos = (past - L) + lax.broadcasted_iota(jnp.int32, (1, L), 1)
    ok = (wpos > t - WINDOW) & (wpos >= 0)
    s_w = jnp.where(ok, _hdot_nt(q, win_ref[0, 0, :, 0, g, :]), NEG)
    s_n = jnp.sum(q * kw_new, axis=-1, keepdims=True)
    m = jnp.maximum(jnp.max(s_w, axis=-1, keepdims=True), s_n)
    p_w = jnp.exp(s_w - m) * ok.astype(F32)
    p_n = jnp.exp(s_n - m)
    den = jnp.maximum(jnp.sum(p_w, axis=-1, keepdims=True) + p_n, 1e-30)
    o_win = (_hdot(p_w, win_ref[0, 0, :, 1, g, :]) + p_n * vw_new) / den

    gb = _sigmoid(gate_ref[0, 0])
    o_ref[0, 0] = ocmp_ref[0, 0] * gb[:, 0:1] + o_slc * gb[:, 1:2] + o_win * gb[:, 2:3]


def _nsa_sample_attend(idx, page_table, kv_cache, win_cache, layer, new6, q8, o_cmp, gate8, past):
    bsz, n_pages = page_table.shape
    G = NSA_KV_HEADS
    n_past_blk = past // SEL_BLOCK
    bpp = PAGE_SIZE // SEL_BLOCK
    L = win_cache.shape[2]

    def blk(k):
        def imap(b, g, idx_ref, pt_ref):
            j = jnp.clip(idx_ref[(b * G + g) * N_SEL + k], 0, n_past_blk - 1)
            return (layer, pt_ref[b * n_pages + j // bpp], j % bpp, 1, 0, 0)
        return pl.BlockSpec((1, 1, SEL_BLOCK, 2, G, HEAD_DIM), imap)

    per_bg = pl.BlockSpec((1, 1, 8, HEAD_DIM), lambda b, g, i, p: (b, g, 0, 0))
    grid_spec = pltpu.PrefetchScalarGridSpec(
        num_scalar_prefetch=2,
        grid=(bsz, G),
        in_specs=([blk(k) for k in range(N_SEL)]
                  + [pl.BlockSpec((1, 6, G, HEAD_DIM), lambda b, g, i, p: (b, 0, 0, 0)),
                     pl.BlockSpec((1, 1, L, 2, G, HEAD_DIM), lambda b, g, i, p: (layer, b, 0, 0, 0, 0)),
                     per_bg, per_bg, per_bg]),
        out_specs=per_bg)
    return pl.pallas_call(
        functools.partial(_nsa_sample_attend_kernel, past=past, n_pages=n_pages),
        out_shape=jax.ShapeDtypeStruct((bsz, G, 8, HEAD_DIM), F32),
        grid_spec=grid_spec,
        compiler_params=_cp(("parallel", "arbitrary")),
        name="nsa_sample_attend",
    )(idx, page_table.reshape(-1), *([kv_cache] * N_SEL), new6, win_cache, q8, o_cmp, gate8)


def _prompt_mixers(x2d, batch, seq, norm_mix, w_in, conv_w, a_log, dt_bias, gdn_norm, cmp_pos,
                   cmp_proj, w_out_a, w_out_b):
    w_main = jnp.concatenate([w_in[:, OFF_QKV:OFF_BRAW], w_in[:, OFF_MERGE:],
                              w_in[:, OFF_QB:OFF_GATEB]], axis=1).astype(BF16)
    w_small = jnp.concatenate([w_in[:, OFF_BRAW:OFF_QB], w_in[:, OFF_GATEB:OFF_MERGE],
                               jnp.zeros((D_MODEL, LANE - 2 * GDN_HEADS - 3 * NSA_HEADS), w_in.dtype)],
                              axis=1).astype(BF16)
    proj, small = _proj_prompt(x2d, norm_mix, w_main, w_small)
    o_a, s_new = _gdn_prompt(proj, small, conv_w, _head_params(a_log, dt_bias), gdn_norm, batch, seq)
    comp = _compress_prompt(proj, cmp_pos, cmp_proj, batch, seq)
    o_b = _nsa_prompt(proj, small, comp, batch, seq)
    y = _merge(o_a, o_b, w_out_a.astype(BF16), w_out_b.astype(BF16), proj, CB_MERGE,
               CB_MERGE + D_MODEL // LANE, False)
    G = NSA_KV_HEADS
    kv6 = proj[:, CB_KVB * LANE:].reshape(batch, seq, 6, G, HEAD_DIM)
    conv_new = proj[:, :CONV_DIM].reshape(batch, seq, CONV_DIM)[:, seq - (CONV_W - 1):]
    return y, kv6[:, :, :4], kv6[:, max(seq - WINDOW, 0):, 4:], s_new, conv_new


def _sample_mixers(x2d, norm_mix, w_in, conv_w, a_log, dt_bias, gdn_norm, cmp_pos, cmp_proj,
                   w_out_a, w_out_b, kv_cache, win_cache, layer, state, conv_buf, page_table):
    bsz = x2d.shape[0]
    win_buf = win_cache[layer]
    G = NSA_KV_HEADS
    past = page_table.shape[1] * PAGE_SIZE
    ps = _proj_sample(x2d, norm_mix, w_in)
    qkv, z = ps[:, OFF_QKV:OFF_Z], ps[:, OFF_Z:OFF_BRAW]
    b_raw, a_raw = ps[:, OFF_BRAW:OFF_ARAW], ps[:, OFF_ARAW:OFF_QB]
    q_b, kv_b = ps[:, OFF_QB:OFF_KVB], ps[:, OFF_KVB:OFF_GATEB]
    gate_b, merge_raw = ps[:, OFF_GATEB:OFF_MERGE], ps[:, OFF_MERGE:]

    ext = jnp.concatenate([conv_buf.astype(F32), qkv[:, None]], axis=1)
    o_a, s_new = _gdn_sample(ext.reshape(bsz, CONV_W, 3 * GDN_HEADS, LANE), conv_w, z, b_raw, a_raw,
                             a_log, dt_bias, gdn_norm, state.astype(F32))

    w4 = cmp_pos.astype(F32).reshape(4, CMP_STRIDE, HEAD_DIM)
    sums = _page_sums(kv_cache, layer, page_table, w4)
    new6 = kv_b.reshape(bsz, 6, G, HEAD_DIM)
    q8 = jnp.pad(q_b.reshape(bsz, G, NSA_HPG, HEAD_DIM), ((0, 0), (0, 0), (0, 8 - NSA_HPG), (0, 0)))
    o_cmp, idx8 = _nsa_sample_select(sums, new6, w4, cmp_proj.astype(F32), q8, past)
    gate8 = jnp.pad(gate_b.reshape(bsz, G, NSA_HPG, 3),
                    ((0, 0), (0, 0), (0, 8 - NSA_HPG), (0, LANE - 3)))
    o_b8 = _nsa_sample_attend(idx8[:, :G, :N_SEL].reshape(-1), page_table, kv_cache, win_cache,
                              layer, new6, q8, o_cmp, gate8, past)
    o_b = o_b8[:, :, :NSA_HPG].reshape(bsz, NSA_HEADS * HEAD_DIM)

    y = _merge(o_a.reshape(bsz, GDN_HEADS * GDN_DV), o_b, w_out_a, w_out_b, merge_raw, 0,
               D_MODEL // LANE, True)
    kv6 = kv_b.reshape(bsz, 1, 6, G, HEAD_DIM)
    win_ext = jnp.concatenate([win_buf.astype(F32), kv6[:, :, 4:]], axis=1)
    keep = min(WINDOW, past + 1)
    return y, kv6[:, :, :4], win_ext[:, win_ext.shape[1] - keep:], s_new, ext[:, 1:]


def kernel(x_prompt, x_sample, cache_nsa_kv, cache_nsa_win, state_gdn, state_conv, page_table, norm_mix, w_in, conv_w, gdn_a_log, gdn_dt_bias, gdn_norm, w_out_a, cmp_pos, cmp_proj, w_out_b, w_out, norm_ffn, w_router, b_router, w_gu, b_gu, w_down, b_down, norm_final):
    assert w_in.shape[0] == 1 and x_sample.shape[1] == 1, "one layer, one new token per sequence"
    bp, seq, _ = x_prompt.shape
    bs = x_sample.shape[0]
    xp = x_prompt.reshape(bp * seq, D_MODEL)
    xs = x_sample.reshape(bs, D_MODEL)
    l = 0
    y_p, kv_p, win_p, gdn_p, conv_p = _prompt_mixers(
        xp, bp, seq, norm_mix[l], w_in[l], conv_w[l], gdn_a_log[l], gdn_dt_bias[l], gdn_norm[l],
        cmp_pos[l], cmp_proj[l], w_out_a[l], w_out_b[l])
    y_s, kv_s, win_s, gdn_s, conv_s = _sample_mixers(
        xs, norm_mix[l], w_in[l], conv_w[l], gdn_a_log[l], gdn_dt_bias[l], gdn_norm[l], cmp_pos[l],
        cmp_proj[l], w_out_a[l], w_out_b[l], cache_nsa_kv, cache_nsa_win, l, state_gdn[l],
        state_conv[l], page_table)

    wr = jnp.pad(w_router[l].astype(F32), ((0, 0), (0, LANE - N_EXPERTS)))
    br = jnp.pad(b_router[l].astype(F32), (0, LANE - N_EXPERTS)).reshape(1, LANE)
    h_p, hn_p, ids_p, gate_p = _outproj_router(xp, y_p, w_out[l].astype(BF16), norm_ffn[l], wr, br, False)
    h_s, hn_s, ids_s, gate_s = _outproj_router(xs, y_s, w_out[l], norm_ffn[l], wr, br, True)
    out_p, out_s = _moe(hn_p, hn_s, h_p, h_s, ids_p, gate_p, ids_s, gate_s, w_gu[l], b_gu[l],
                        w_down[l], b_down[l], norm_final)
    return (out_p.reshape(x_prompt.shape), out_s.reshape(x_sample.shape),
            kv_p[None], kv_s[None], win_p[None], win_s[None],
            gdn_p[None].astype(state_gdn.dtype), gdn_s[None].astype(state_gdn.dtype),
            conv_p[None], conv_s[None])
```

```python
import functools
import math

import jax
import jax.numpy as jnp
from jax import lax
from jax.experimental import pallas as pl
from jax.experimental.pallas import tpu as pltpu

F32 = jnp.float32
BF16 = jnp.bfloat16
HIGHEST = lax.Precision.HIGHEST

D_MODEL = 2048
GDN_HEADS = 16
GDN_DK = 128
GDN_DV = 128
CONV_W = 4
CONV_DIM = GDN_HEADS * (2 * GDN_DK + GDN_DV)
NSA_HEADS = 16
NSA_KV_HEADS = 4
HEAD_DIM = 128
NSA_HPG = NSA_HEADS // NSA_KV_HEADS
CMP_LEN = 32
CMP_STRIDE = 16
SEL_BLOCK = 64
N_SEL = 16
WINDOW = 512
PAGE_SIZE = 128
N_EXPERTS = 32
TOP_K = 4
D_FF = 2048
SWIGLU_LIMIT = 7.0
SWIGLU_ALPHA = 1.702
EPS = 1e-6
NEG = -1e30
FORCE = 1e6

IN_SPLITS = (CONV_DIM, GDN_HEADS * GDN_DV, GDN_HEADS, GDN_HEADS, NSA_HEADS * HEAD_DIM,
             6 * NSA_KV_HEADS * HEAD_DIM, 3 * NSA_HEADS, 2 * D_MODEL)
_OFF = [0]
for _s in IN_SPLITS:
    _OFF.append(_OFF[-1] + _s)
(OFF_QKV, OFF_Z, OFF_BRAW, OFF_ARAW, OFF_QB, OFF_KVB, OFF_GATEB, OFF_MERGE, N_IN) = _OFF

LANE = 128
CB_Q, CB_K, CB_V = 0, GDN_HEADS, 2 * GDN_HEADS
CB_Z = CONV_DIM // LANE
CB_MERGE = CB_Z + GDN_HEADS
CB_QB = CB_MERGE + 2 * D_MODEL // LANE
CB_KVB = CB_QB + NSA_HEADS
N_MAIN = (CB_KVB + 6 * NSA_KV_HEADS) * LANE
SM_BRAW, SM_ARAW, SM_GATE = 0, GDN_HEADS, 2 * GDN_HEADS

GDN_CHUNK = 128
GDN_HB = 8
NSA_TQ = 256
NSA_KT = 1024
MOE_BLK = 256
VMEM_LIMIT = 60000 * 1024


def _cp(sem, vmem=VMEM_LIMIT):
    return pltpu.CompilerParams(dimension_semantics=sem, vmem_limit_bytes=vmem)


def _lane_pick(x, lane):
    ids = lax.broadcasted_iota(jnp.int32, x.shape, 1)
    return jnp.sum(jnp.where(ids == lane, x, 0.0), axis=1, keepdims=True)


def _sigmoid(x):
    return 1.0 / (1.0 + jnp.exp(-x))


def _silu(x):
    return x * _sigmoid(x)


def _softplus(x):
    return jnp.maximum(x, 0.0) + jnp.log(1.0 + jnp.exp(-jnp.abs(x)))


def _proj_kernel(x_ref, g_ref, w_ref, ws_ref, o_ref, os_ref, xn_ref):
    @pl.when(pl.program_id(1) == 0)
    def _():
        x = x_ref[...]
        ms = jnp.mean(x * x, axis=-1, keepdims=True)
        xn = (x * lax.rsqrt(ms + EPS) * g_ref[...]).astype(BF16)
        xn_ref[...] = xn
        os_ref[...] = jnp.dot(xn, ws_ref[...], preferred_element_type=F32)

    o_ref[...] = jnp.dot(xn_ref[...], w_ref[...], preferred_element_type=F32)


def _proj_prompt(x, norm_g, w_main, w_small):
    n = x.shape[0]
    tm = min(1024, n)
    tn = 512
    return pl.pallas_call(
        _proj_kernel,
        out_shape=(jax.ShapeDtypeStruct((n, N_MAIN), F32), jax.ShapeDtypeStruct((n, LANE), F32)),
        grid=(n // tm, N_MAIN // tn),
        in_specs=[pl.BlockSpec((tm, D_MODEL), lambda i, j: (i, 0)),
                  pl.BlockSpec((1, D_MODEL), lambda i, j: (0, 0)),
                  pl.BlockSpec((D_MODEL, tn), lambda i, j: (0, j)),
                  pl.BlockSpec((D_MODEL, LANE), lambda i, j: (0, 0))],
        out_specs=(pl.BlockSpec((tm, tn), lambda i, j: (i, j)),
                   pl.BlockSpec((tm, LANE), lambda i, j: (i, 0))),
        scratch_shapes=[pltpu.VMEM((tm, D_MODEL), BF16)],
        compiler_params=_cp(("parallel", "arbitrary")),
        name="proj_prompt",
    )(x, norm_g.reshape(1, D_MODEL), w_main, w_small)


def _shift_rows(x, prev8, s):
    xs = pltpu.roll(x, s, 0)
    ps = pltpu.roll(prev8, s, 0)
    row = lax.broadcasted_iota(jnp.int32, prev8.shape, 0)
    top = jnp.where(row < s, ps, xs[0:8])
    return jnp.concatenate([top, xs[8:]], axis=0)


def _conv_silu(x, prev8, cw):
    y = x * cw[CONV_W - 1:CONV_W]
    for s in range(1, CONV_W):
        y = y + _shift_rows(x, prev8, s) * cw[CONV_W - 1 - s:CONV_W - s]
    return _silu(y)


def _l2norm(x):
    return x * lax.rsqrt(jnp.sum(x * x, axis=-1, keepdims=True) + EPS)


def _cumsum_rows(x):
    c = x.shape[0]
    row = lax.broadcasted_iota(jnp.int32, x.shape, 0)
    s = 1
    while s < c:
        x = x + jnp.where(row >= s, pltpu.roll(x, s, 0), 0.0)
        s *= 2
    return x


def _bdot(a, b):
    return jnp.dot(a.astype(BF16), b.astype(BF16), preferred_element_type=F32)


def _bdot_nt(a, b):
    return lax.dot_general(a.astype(BF16), b.astype(BF16), (((1,), (1,)), ((), ())),
                           preferred_element_type=F32)


def _bdot_tn(a, b):
    return lax.dot_general(a.astype(BF16), b.astype(BF16), (((0,), (0,)), ((), ())),
                           preferred_element_type=F32)


INV_BASE = 16


def _unit_lower_inverses(mats, ri, ci):
    c = mats[0].shape[0]
    same = lambda s: (ri // s) == (ci // s)
    eye = jnp.where(ri == ci, 1.0, 0.0)
    blk = same(INV_BASE)
    ps = [jnp.where(blk, a, 0.0) for a in mats]
    ts = [eye - p for p in ps]
    n = 2
    while n < INV_BASE:
        ps = [_bdot(p, p) for p in ps]
        ts = [t + _bdot(t, p) for t, p in zip(ts, ps)]
        n *= 2
    s = INV_BASE
    while s < c:
        off = same(2 * s) & jnp.logical_not(same(s))
        tm = [_bdot(t, jnp.where(off, a, 0.0)) for t, a in zip(ts, mats)]
        ts = [t - _bdot(x, t) for t, x in zip(ts, tm)]
        s *= 2
    return ts


def _gdn_kernel(q_ref, k_ref, v_ref, z_ref, sm_ref, cwq_ref, cwk_ref, cwv_ref, hp_ref, ng_ref,
                o_ref, s_out_ref, s_ref, pq_ref, pk_ref, pv_ref):
    hg = pl.program_id(1)
    c = pl.program_id(2)
    C = GDN_CHUNK

    @pl.when(c == 0)
    def _():
        s_ref[...] = jnp.zeros_like(s_ref)
        pq_ref[...] = jnp.zeros_like(pq_ref)
        pk_ref[...] = jnp.zeros_like(pk_ref)
        pv_ref[...] = jnp.zeros_like(pv_ref)

    xq, xk, xv = q_ref[...], k_ref[...], v_ref[...]
    yq = _conv_silu(xq, pq_ref[...], cwq_ref[...])
    yk = _conv_silu(xk, pk_ref[...], cwk_ref[...])
    yv = _conv_silu(xv, pv_ref[...], cwv_ref[...])
    pq_ref[...] = xq[C - 8:]
    pk_ref[...] = xk[C - 8:]
    pv_ref[...] = xv[C - 8:]

    sm = sm_ref[...]
    hp = hp_ref[...]
    g_all = -jnp.exp(hp[0:1]) * _softplus(sm + hp[1:2])
    G_all = _cumsum_rows(g_all)
    ri = lax.broadcasted_iota(jnp.int32, (C, C), 0)
    ci = lax.broadcasted_iota(jnp.int32, (C, C), 1)
    z = _silu(z_ref[...])
    HS = range(GDN_HB)
    cols = [slice(i * LANE, (i + 1) * LANE) for i in HS]
    q = [_l2norm(yq[:, cols[i]]) * (GDN_DK ** -0.5) for i in HS]
    k = [_l2norm(yk[:, cols[i]]) for i in HS]
    v = [yv[:, cols[i]] for i in HS]
    beta = [_sigmoid(_lane_pick(sm, SM_BRAW + hg * GDN_HB + i)) for i in HS]
    G = [_lane_pick(G_all, SM_ARAW + hg * GDN_HB + i) for i in HS]
    eG = [jnp.exp(G[i]) for i in HS]
    g_last = [G[i][C - 1:C] for i in HS]
    Gb = [jnp.broadcast_to(G[i], (C, C)) for i in HS]
    decay = [jnp.exp(jnp.where(ri >= ci, Gb[i] - Gb[i].T, NEG)) for i in HS]
    kb = [k[i] * beta[i] for i in HS]
    kk = [_bdot_nt(kb[i], k[i]) for i in HS]
    qk = [_bdot_nt(q[i], k[i]) for i in HS]
    a_mat = [jnp.where(ri > ci, kk[i] * decay[i], 0.0) for i in HS]
    t_inv = _unit_lower_inverses(a_mat, ri, ci)
    rhs = [jnp.concatenate([v[i] * beta[i], kb[i] * eG[i]], axis=1) for i in HS]
    x = [_bdot(t_inv[i], rhs[i]) for i in HS]
    S = [s_ref[i] for i in HS]
    q_s = [_bdot(q[i] * eG[i], S[i]) for i in HS]
    w_s = [_bdot(x[i][:, GDN_DV:], S[i]) for i in HS]
    v_new = [x[i][:, :GDN_DV] - w_s[i] for i in HS]
    o = [q_s[i] + _bdot(qk[i] * decay[i], v_new[i]) for i in HS]
    kv = [_bdot_tn(k[i] * jnp.exp(g_last[i] - G[i]), v_new[i]) for i in HS]
    s_new = [S[i] * jnp.exp(g_last[i]) + kv[i] for i in HS]
    for i in HS:
        s_ref[i] = s_new[i]
        on = o[i] * lax.rsqrt(jnp.mean(o[i] * o[i], axis=-1, keepdims=True) + EPS) * ng_ref[...]
        o_ref[:, cols[i]] = (on * z[:, cols[i]]).astype(o_ref.dtype)

    @pl.when(c == pl.num_programs(2) - 1)
    def _():
        for i in HS:
            s_out_ref[0, i] = s_new[i]


def _head_params(a_log, dt_bias):
    hp = jnp.zeros((8, LANE), F32)
    hp = hp.at[0, SM_ARAW:SM_ARAW + GDN_HEADS].set(a_log.astype(F32))
    return hp.at[1, SM_ARAW:SM_ARAW + GDN_HEADS].set(dt_bias.astype(F32))


def _gdn_prompt(proj, small, conv_w, head_par, norm_g, batch, seq):
    C = GDN_CHUNK
    nc = seq // C
    H = GDN_HEADS
    HB = GDN_HB
    W = HB * LANE

    def rows(cb):
        return pl.BlockSpec((C, W), lambda b, h, c, cb=cb: (b * nc + c, cb // HB + h))

    def cw(cb):
        return pl.BlockSpec((CONV_W, W), lambda b, h, c, cb=cb: (0, cb // HB + h))

    return pl.pallas_call(
        _gdn_kernel,
        out_shape=(jax.ShapeDtypeStruct((batch * seq, H * GDN_DV), BF16),
                   jax.ShapeDtypeStruct((batch, H, GDN_DK, GDN_DV), F32)),
        grid=(batch, H // HB, nc),
        in_specs=[rows(CB_Q), rows(CB_K), rows(CB_V), rows(CB_Z),
                  pl.BlockSpec((C, LANE), lambda b, h, c: (b * nc + c, 0)),
                  cw(CB_Q), cw(CB_K), cw(CB_V),
                  pl.BlockSpec((8, LANE), lambda b, h, c: (0, 0)),
                  pl.BlockSpec((1, GDN_DV), lambda b, h, c: (0, 0))],
        out_specs=(pl.BlockSpec((C, W), lambda b, h, c: (b * nc + c, h)),
                   pl.BlockSpec((1, HB, GDN_DK, GDN_DV), lambda b, h, c: (b, h, 0, 0))),
        scratch_shapes=[pltpu.VMEM((HB, GDN_DK, GDN_DV), F32), pltpu.VMEM((8, W), F32),
                        pltpu.VMEM((8, W), F32), pltpu.VMEM((8, W), F32)],
        compiler_params=_cp(("parallel", "parallel", "arbitrary")),
        name="gdn_prompt",
    )(proj, proj, proj, proj, small, conv_w, conv_w, conv_w, head_par, norm_g.reshape(1, GDN_DV))


def _cmp_kernel(x_ref, w_ref, p_ref, o_ref):
    t = x_ref.shape[0]
    nch = t // CMP_STRIDE
    x3 = x_ref[...].reshape(nch, CMP_STRIDE, HEAD_DIM)
    w = w_ref[0]
    s0 = jnp.sum(x3 * w[None, :CMP_STRIDE], axis=1)
    s1 = jnp.sum(x3 * w[None, CMP_STRIDE:], axis=1)
    row = lax.broadcasted_iota(jnp.int32, (nch, HEAD_DIM), 0)
    pooled = jnp.where(row < nch - 1, s0 + pltpu.roll(s1, nch - 1, 0), 0.0)
    o_ref[0, 0, 0] = jnp.dot(pooled, p_ref[0], precision=HIGHEST, preferred_element_type=F32)


def _compress_prompt(proj, cmp_pos, cmp_proj, batch, seq):
    nch = seq // CMP_STRIDE
    return pl.pallas_call(
        _cmp_kernel,
        out_shape=jax.ShapeDtypeStruct((batch, 2, NSA_KV_HEADS, nch, HEAD_DIM), F32),
        grid=(batch, 2, NSA_KV_HEADS),
        in_specs=[pl.BlockSpec((seq, LANE), lambda b, e, g: (b, CB_KVB + e * NSA_KV_HEADS + g)),
                  pl.BlockSpec((1, CMP_LEN, HEAD_DIM), lambda b, e, g: (e, 0, 0)),
                  pl.BlockSpec((1, HEAD_DIM, HEAD_DIM), lambda b, e, g: (e, 0, 0))],
        out_specs=pl.BlockSpec((1, 1, 1, nch, HEAD_DIM), lambda b, e, g: (b, e, g, 0, 0)),
        compiler_params=_cp(("parallel", "parallel", "parallel")),
        name="nsa_compress_prompt",
    )(proj, cmp_pos, cmp_proj)


def _masked_softmax(s, mask):
    s = jnp.where(mask, s, NEG)
    p = jnp.exp(s - jnp.max(s, axis=-1, keepdims=True)) * mask.astype(F32)
    return p / jnp.maximum(jnp.sum(p, axis=-1, keepdims=True), 1e-30)


def _select_blocks_t(score_t):
    nj = score_t.shape[0]
    sub = 8
    groups = [score_t[v:v + sub] for v in range(0, nj, sub)]
    j_in = lax.broadcasted_iota(jnp.int32, groups[0].shape, 0)
    cnts = [jnp.zeros(g.shape, jnp.int32) for g in groups]
    for jp in range(nj):
        row = score_t[jp:jp + 1, :]
        for v, g in enumerate(groups):
            if v * sub + sub - 1 <= jp:
                ahead = jnp.where(row > g, 1, 0)
            elif v * sub > jp:
                ahead = jnp.where(row >= g, 1, 0)
            else:
                ahead = jnp.where(j_in + v * sub > jp, jnp.where(row >= g, 1, 0),
                                  jnp.where(row > g, 1, 0))
            cnts[v] = cnts[v] + ahead
    cnt = jnp.concatenate(cnts, axis=0)
    return (cnt < N_SEL) & (score_t > NEG / 2)


def _nsa_kernel(q0_ref, q1_ref, q2_ref, q3_ref, sm_ref, ck_ref, cv_ref, ks_ref, vs_ref, kw_ref,
                vw_ref, o_ref, acc_ref, m_ref, l_ref, kvb_ref):
    g = pl.program_id(1)
    qi = pl.program_id(2)
    TQ = q0_ref.shape[0]
    R4 = NSA_HPG * TQ
    T = ks_ref.shape[0]
    KT = min(NSA_KT, T)
    qs = qi * TQ

    @pl.when(qi == 0)
    def _():
        for i, r in enumerate((ks_ref, vs_ref, kw_ref, vw_ref)):
            kvb_ref[i] = r[...].astype(BF16)

    ksb_ref, vsb_ref, kwb_ref, vwb_ref = (kvb_ref.at[i] for i in range(4))
    q4 = jnp.concatenate([q0_ref[...], q1_ref[...], q2_ref[...], q3_ref[...]], axis=0)
    q4 = (q4 * (HEAD_DIM ** -0.5)).astype(BF16)
    t4 = qs + lax.rem(lax.broadcasted_iota(jnp.int32, (R4, 1), 0), TQ)
    t1 = qs + lax.broadcasted_iota(jnp.int32, (TQ, 1), 0)

    ck = ck_ref[0, 0, 0]
    nc = ck.shape[0]
    s = _bdot_nt(q4, ck)
    cmp_end = CMP_STRIDE * lax.broadcasted_iota(jnp.int32, (1, nc), 1) + (CMP_LEN - 1)
    p = _masked_softmax(s, cmp_end <= t4)
    o_cmp = _bdot(p, cv_ref[0, 0, 0])

    n_slc = T // SEL_BLOCK
    nsp = max(n_slc, LANE)
    psum = p[0:TQ]
    for j in range(1, NSA_HPG):
        psum = psum + p[j * TQ:(j + 1) * TQ]
    ci = CMP_STRIDE * lax.broadcasted_iota(jnp.int32, (nc, nsp), 0)
    bj = SEL_BLOCK * lax.broadcasted_iota(jnp.int32, (nc, nsp), 1)
    cmap = ((ci < bj + SEL_BLOCK) & (ci + CMP_LEN > bj)).astype(F32)
    imp = jnp.dot(psum, cmap, precision=HIGHEST, preferred_element_type=F32)
    imp_t = imp.T[:n_slc]
    j_idx = lax.broadcasted_iota(jnp.int32, (n_slc, TQ), 0)
    jt = lax.shift_right_logical(qs + lax.broadcasted_iota(jnp.int32, (1, TQ), 1),
                                 int(math.log2(SEL_BLOCK)))
    force = (j_idx == 0) | (j_idx == jt) | (j_idx == jt - 1)
    score_t = jnp.where(j_idx > jt, NEG, jnp.where(force, FORCE, imp_t))
    sel_t = jnp.where(_select_blocks_t(score_t), 1.0, 0.0)

    acc_ref[...] = jnp.zeros_like(acc_ref)
    m_ref[...] = jnp.full_like(m_ref, NEG)
    l_ref[...] = jnp.zeros_like(l_ref)
    n_kt = (qs + TQ - 1) // KT + 1

    def slc_body(kt, carry):
        k0 = pl.multiple_of(kt * KT, KT)
        kb = ksb_ref[pl.ds(k0, KT), :]
        vb = vsb_ref[pl.ds(k0, KT), :]
        s = _bdot_nt(q4, kb)
        kpos = k0 + lax.broadcasted_iota(jnp.int32, (1, KT), 1)
        eb = lax.broadcasted_iota(jnp.int32, (n_slc, KT), 0) == lax.shift_right_logical(
            k0 + lax.broadcasted_iota(jnp.int32, (n_slc, KT), 1), int(math.log2(SEL_BLOCK)))
        selx = _bdot_tn(sel_t, jnp.where(eb, 1.0, 0.0))
        bias = jnp.where((selx > 0.5) & (kpos <= t1), 0.0, NEG)
        s = s + jnp.concatenate([bias] * NSA_HPG, axis=0)
        m_old = m_ref[...]
        m_new = jnp.maximum(m_old, jnp.max(s, axis=-1, keepdims=True))
        pe = jnp.exp(s - m_new)
        alpha = jnp.exp(m_old - m_new)
        l_ref[...] = alpha * l_ref[...] + jnp.sum(pe, axis=-1, keepdims=True)
        acc_ref[...] = alpha * acc_ref[...] + _bdot(pe, vb)
        m_ref[...] = m_new
        return carry

    lax.fori_loop(0, n_kt, slc_body, 0)
    o_slc = acc_ref[...] / jnp.maximum(l_ref[...], 1e-30)

    WK = min(WINDOW + TQ, T)
    w0 = pl.multiple_of(jnp.minimum(jnp.maximum(qs - WINDOW, 0), T - WK), TQ)
    s = _bdot_nt(q4, kwb_ref[pl.ds(w0, WK), :])
    wpos = w0 + lax.broadcasted_iota(jnp.int32, (1, WK), 1)
    p = _masked_softmax(s, (wpos <= t4) & (wpos > t4 - WINDOW))
    o_win = _bdot(p, vwb_ref[pl.ds(w0, WK), :])

    gates = _sigmoid(sm_ref[...])
    for j in range(NSA_HPG):
        lane0 = SM_GATE + (g * NSA_HPG + j) * 3
        rows = slice(j * TQ, (j + 1) * TQ)
        ob = (o_cmp[rows] * _lane_pick(gates, lane0) + o_slc[rows] * _lane_pick(gates, lane0 + 1)
              + o_win[rows] * _lane_pick(gates, lane0 + 2))
        o_ref[:, j * HEAD_DIM:(j + 1) * HEAD_DIM] = ob.astype(o_ref.dtype)


def _nsa_prompt(proj, small, comp, batch, seq):
    TQ = min(NSA_TQ, seq)
    nq = seq // TQ
    nch = seq // CMP_STRIDE
    G = NSA_KV_HEADS

    def qspec(j):
        return pl.BlockSpec((TQ, LANE), lambda b, g, i, j=j: (b * nq + i, CB_QB + g * NSA_HPG + j))

    def kvspec(e):
        return pl.BlockSpec((seq, LANE), lambda b, g, i, e=e: (b, CB_KVB + e * G + g))

    def cspec(e):
        return pl.BlockSpec((1, 1, 1, nch, HEAD_DIM), lambda b, g, i, e=e: (b, e, g, 0, 0))

    R4 = NSA_HPG * TQ
    return pl.pallas_call(
        _nsa_kernel,
        out_shape=jax.ShapeDtypeStruct((batch * seq, NSA_HEADS * HEAD_DIM), BF16),
        grid=(batch, G, nq),
        in_specs=[qspec(0), qspec(1), qspec(2), qspec(3),
                  pl.BlockSpec((TQ, LANE), lambda b, g, i: (b * nq + i, 0)),
                  cspec(0), cspec(1), kvspec(2), kvspec(3), kvspec(4), kvspec(5)],
        out_specs=pl.BlockSpec((TQ, NSA_HPG * HEAD_DIM), lambda b, g, i: (b * nq + i, g)),
        scratch_shapes=[pltpu.VMEM((R4, HEAD_DIM), F32), pltpu.VMEM((R4, 1), F32),
                        pltpu.VMEM((R4, 1), F32), pltpu.VMEM((4, seq, HEAD_DIM), BF16)],
        compiler_params=_cp(("parallel", "parallel", "arbitrary")),
        name="nsa_prompt",
    )(proj, proj, proj, proj, small, comp, comp, proj, proj, proj, proj)


def _dot(a, b, exact):
    if exact:
        return jnp.dot(a.astype(F32), b.astype(F32), precision=HIGHEST, preferred_element_type=F32)
    return jnp.dot(a.astype(BF16), b.astype(BF16), preferred_element_type=F32)


def _merge_kernel(oa_ref, ob_ref, wa_ref, wb_ref, m0_ref, m1_ref, y_ref, *, exact):
    ya = _dot(oa_ref[...], wa_ref[...], exact)
    yb = _dot(ob_ref[...], wb_ref[...], exact)
    y = _sigmoid(m0_ref[...]) * ya + _sigmoid(m1_ref[...]) * yb
    y_ref[...] = y.astype(y_ref.dtype)


def _merge(o_a, o_b, w_a, w_b, gates, m0_cb, m1_cb, exact):
    n = o_a.shape[0]
    tm = min(1024, n)
    tn = 512
    cpb = tn // LANE
    return pl.pallas_call(
        functools.partial(_merge_kernel, exact=exact),
        out_shape=jax.ShapeDtypeStruct((n, D_MODEL), F32 if exact else BF16),
        grid=(n // tm, D_MODEL // tn),
        in_specs=[pl.BlockSpec((tm, D_MODEL), lambda i, j: (i, 0)),
                  pl.BlockSpec((tm, D_MODEL), lambda i, j: (i, 0)),
                  pl.BlockSpec((D_MODEL, tn), lambda i, j: (0, j)),
                  pl.BlockSpec((D_MODEL, tn), lambda i, j: (0, j)),
                  pl.BlockSpec((tm, tn), lambda i, j: (i, m0_cb // cpb + j)),
                  pl.BlockSpec((tm, tn), lambda i, j: (i, m1_cb // cpb + j))],
        out_specs=pl.BlockSpec((tm, tn), lambda i, j: (i, j)),
        compiler_params=_cp(("parallel", "parallel")),
        name="merge_exact" if exact else "merge",
    )(o_a, o_b, w_a, w_b, gates, gates)


def _top4(logits):
    lane = lax.broadcasted_iota(jnp.int32, logits.shape, 1)
    ids = jnp.full(logits.shape, -1, jnp.int32)
    vals = jnp.full(logits.shape, NEG, F32)
    cur = logits
    for k in range(TOP_K):
        m = jnp.max(cur, axis=1, keepdims=True)
        idx = jnp.min(jnp.where(cur == m, lane, LANE), axis=1, keepdims=True)
        ids = jnp.where(lane == k, idx, ids)
        vals = jnp.where(lane == k, m, vals)
        cur = jnp.where(lane == idx, 2 * NEG, cur)
    e = jnp.where(lane < TOP_K, jnp.exp(vals - jnp.max(vals, axis=1, keepdims=True)), 0.0)
    return ids, e / jnp.sum(e, axis=1, keepdims=True)


def _outproj_kernel(x_ref, y_ref, w_ref, gn_ref, wr_ref, br_ref, h_ref, hn_ref, ids_ref, gate_ref,
                    *, exact):
    h = x_ref[...] + _dot(y_ref[...], w_ref[...], exact)
    h_ref[...] = h
    hn = h * lax.rsqrt(jnp.mean(h * h, axis=-1, keepdims=True) + EPS) * gn_ref[...]
    hn_ref[...] = hn
    logits = jnp.dot(hn, wr_ref[...], precision=HIGHEST, preferred_element_type=F32) + br_ref[...]
    lane = lax.broadcasted_iota(jnp.int32, logits.shape, 1)
    ids, gate = _top4(jnp.where(lane < N_EXPERTS, logits, NEG))
    ids_ref[...] = ids
    gate_ref[...] = gate


def _outproj_router(x, y, w_out, norm_g, w_router, b_router, exact):
    n = x.shape[0]
    tm = min(256, n)
    row = lambda i: (i, 0)
    fixed = lambda i: (0, 0)
    return pl.pallas_call(
        functools.partial(_outproj_kernel, exact=exact),
        out_shape=(jax.ShapeDtypeStruct((n, D_MODEL), F32), jax.ShapeDtypeStruct((n, D_MODEL), F32),
                   jax.ShapeDtypeStruct((n, LANE), jnp.int32), jax.ShapeDtypeStruct((n, LANE), F32)),
        grid=(n // tm,),
        in_specs=[pl.BlockSpec((tm, D_MODEL), row), pl.BlockSpec((tm, D_MODEL), row),
                  pl.BlockSpec((D_MODEL, D_MODEL), fixed), pl.BlockSpec((1, D_MODEL), fixed),
                  pl.BlockSpec((D_MODEL, LANE), fixed), pl.BlockSpec((1, LANE), fixed)],
        out_specs=(pl.BlockSpec((tm, D_MODEL), row), pl.BlockSpec((tm, D_MODEL), row),
                   pl.BlockSpec((tm, LANE), row), pl.BlockSpec((tm, LANE), row)),
        compiler_params=_cp(("parallel",)),
        name="outproj_router_exact" if exact else "outproj_router",
    )(x, y, w_out, norm_g.reshape(1, D_MODEL), w_router, b_router)


def _rank_kernel(ids_ref, rank_ref, cnt_ref, carry_ref):
    i = pl.program_id(0)

    @pl.when(i == 0)
    def _():
        carry_ref[...] = jnp.zeros_like(carry_ref)

    ids = ids_ref[...]
    tm = ids.shape[0]
    lane = lax.broadcasted_iota(jnp.int32, ids.shape, 1)
    cols = [jnp.sum(jnp.where(lane == k, ids, 0), axis=1, keepdims=True) for k in range(TOP_K)]
    mask = jnp.zeros(ids.shape, F32)
    for k in range(TOP_K):
        mask = mask + (lane == cols[k]).astype(F32)
    r = lax.broadcasted_iota(jnp.int32, (tm, tm), 0)
    c = lax.broadcasted_iota(jnp.int32, (tm, tm), 1)
    before = jnp.dot((r > c).astype(BF16), mask.astype(BF16), preferred_element_type=F32)
    pos = before + carry_ref[0:1]
    rank = jnp.zeros(ids.shape, F32)
    for k in range(TOP_K):
        rk = jnp.sum(jnp.where(lane == cols[k], pos, 0.0), axis=1, keepdims=True)
        rank = jnp.where(lane == k, rk, rank)
    rank_ref[...] = rank.astype(jnp.int32)
    total = carry_ref[0:1] + jnp.sum(mask, axis=0, keepdims=True)
    carry_ref[...] = jnp.broadcast_to(total, carry_ref.shape)
    cnt_ref[...] = jnp.broadcast_to(total, cnt_ref.shape).astype(jnp.int32)


def _expert_ranks(ids):
    n = ids.shape[0]
    tm = min(256, n)
    return pl.pallas_call(
        _rank_kernel,
        out_shape=(jax.ShapeDtypeStruct((n, LANE), jnp.int32), jax.ShapeDtypeStruct((8, LANE), jnp.int32)),
        grid=(n // tm,),
        in_specs=[pl.BlockSpec((tm, LANE), lambda i: (i, 0))],
        out_specs=(pl.BlockSpec((tm, LANE), lambda i: (i, 0)), pl.BlockSpec((8, LANE), lambda i: (0, 0))),
        scratch_shapes=[pltpu.VMEM((8, LANE), F32)],
        compiler_params=_cp(("arbitrary",)),
        name="expert_ranks",
    )(ids)


def _scatter_kernel(slot_ref, x_ref, xs_in_ref, xs_ref, sem):
    del xs_in_ref
    tm = x_ref.shape[0]

    def copy(r, k):
        return pltpu.make_async_copy(x_ref.at[pl.ds(r, 1)],
                                     xs_ref.at[pl.ds(slot_ref[0, 0, r * TOP_K + k], 1)], sem)

    def start(r, c):
        for k in range(TOP_K):
            copy(r, k).start()
        return c

    def wait(r, c):
        for k in range(TOP_K):
            copy(r, k).wait()
        return c

    lax.fori_loop(0, tm, start, 0)
    lax.fori_loop(0, tm, wait, 0)


def _scatter_rows(x, slots, xs):
    n = x.shape[0]
    tm = min(256, n)
    nt = n // tm
    return pl.pallas_call(
        _scatter_kernel,
        out_shape=jax.ShapeDtypeStruct(xs.shape, xs.dtype),
        grid=(nt,),
        in_specs=[pl.BlockSpec((1, 1, tm * TOP_K), lambda i: (i, 0, 0), memory_space=pltpu.SMEM),
                  pl.BlockSpec((tm, D_MODEL), lambda i: (i, 0)),
                  pl.BlockSpec(memory_space=pl.ANY)],
        out_specs=pl.BlockSpec(memory_space=pl.ANY),
        scratch_shapes=[pltpu.SemaphoreType.DMA(())],
        input_output_aliases={2: 0},
        compiler_params=_cp(("arbitrary",)),
        name="moe_scatter",
    )(slots.reshape(nt, 1, tm * TOP_K), x, xs)


def _stream_expert_weights(be_ref, nu_ref, ne_ref, nj_ref, cnt_ref, tile_copies, consume):
    j = pl.program_id(0)
    b = pl.program_id(1)
    step = j * pl.num_programs(1) + b
    used = b < nu_ref[0]
    e = be_ref[b]
    first = used & ((b == 0) | (e != be_ref[jnp.maximum(b - 1, 0)]))

    @pl.when(step == 0)
    def _():
        cnt_ref[0] = 0
        for c in tile_copies(e, j, 0):
            c.start()

    @pl.when(first)
    def _():
        slot = lax.rem(cnt_ref[0], 2)
        for c in tile_copies(e, j, slot):
            c.wait()
        consume(slot)
        ne = ne_ref[step]

        @pl.when(ne >= 0)
        def _():
            for c in tile_copies(ne, nj_ref[step], 1 - slot):
                c.start()

        cnt_ref[0] = cnt_ref[0] + 1

    return used


def _next_tiles(blk_e, n_used, n_col_tiles):
    nb = blk_e.shape[0]
    used = jnp.arange(nb) < n_used[0]
    prev = jnp.concatenate([jnp.full((1,), -1, blk_e.dtype), blk_e[:-1]])
    first = jnp.tile(used & (blk_e != prev), n_col_tiles)
    n = n_col_tiles * nb
    key = jnp.where(first, jnp.arange(n), n)
    nxt = jnp.concatenate([lax.cummin(key, axis=0, reverse=True)[1:], jnp.full((1,), n, key.dtype)])
    ok = nxt < n
    nxt = jnp.minimum(nxt, n - 1)
    return (jnp.where(ok, blk_e[nxt % nb], -1).astype(jnp.int32),
            jnp.where(ok, nxt // nb, 0).astype(jnp.int32))


def _moe_up_kernel(be_ref, nu_ref, ne_ref, nj_ref, x_ref, w_hbm, bg_ref, bu_ref, o_ref,
                   wbuf_ref, wgb_ref, wub_ref, cnt_ref, sem):
    tf = wgb_ref.shape[1]
    nf = pl.num_programs(0)

    def tile_copies(e, j, slot):
        return [pltpu.make_async_copy(
            w_hbm.at[e, :, pl.ds(pl.multiple_of((p * nf + j) * tf, tf), tf)],
            wbuf_ref.at[slot, p], sem.at[slot]) for p in range(2)]

    def consume(slot):
        wgb_ref[...] = wbuf_ref[slot, 0].astype(BF16)
        wub_ref[...] = wbuf_ref[slot, 1].astype(BF16)

    used = _stream_expert_weights(be_ref, nu_ref, ne_ref, nj_ref, cnt_ref, tile_copies, consume)

    @pl.when(used)
    def _():
        x = x_ref[...].astype(BF16)
        g = jnp.dot(x, wgb_ref[...], preferred_element_type=F32) + bg_ref[0]
        u = jnp.dot(x, wub_ref[...], preferred_element_type=F32) + bu_ref[0]
        g = jnp.minimum(g, SWIGLU_LIMIT)
        u = jnp.clip(u, -SWIGLU_LIMIT, SWIGLU_LIMIT)
        o_ref[...] = ((u + 1.0) * g * _sigmoid(SWIGLU_ALPHA * g)).astype(o_ref.dtype)

    @pl.when(jnp.logical_not(used))
    def _():
        o_ref[...] = jnp.zeros_like(o_ref)


def _moe_up(xs, blk_e, n_used, w_gu, b_gu, tf=1024):
    n_blocks = xs.shape[0] // MOE_BLK
    nf = D_FF // tf
    next_e, next_j = _next_tiles(blk_e, n_used, nf)
    grid_spec = pltpu.PrefetchScalarGridSpec(
        num_scalar_prefetch=4,
        grid=(nf, n_blocks),
        in_specs=[pl.BlockSpec((MOE_BLK, D_MODEL), lambda j, b, be, *_: (b, 0)),
                  pl.BlockSpec(memory_space=pl.ANY),
                  pl.BlockSpec((1, 1, tf), lambda j, b, be, *_: (be[b], 0, j)),
                  pl.BlockSpec((1, 1, tf), lambda j, b, be, *_: (be[b], 0, nf + j))],
        out_specs=pl.BlockSpec((MOE_BLK, tf), lambda j, b, be, *_: (b, j)),
        scratch_shapes=[pltpu.VMEM((2, 2, D_MODEL, tf), F32), pltpu.VMEM((D_MODEL, tf), BF16),
                        pltpu.VMEM((D_MODEL, tf), BF16), pltpu.SMEM((1,), jnp.int32),
                        pltpu.SemaphoreType.DMA((2,))])
    return pl.pallas_call(
        _moe_up_kernel,
        out_shape=jax.ShapeDtypeStruct((xs.shape[0], D_FF), BF16),
        grid_spec=grid_spec,
        compiler_params=_cp(("arbitrary", "arbitrary")),
        name="moe_up",
    )(blk_e, n_used, next_e, next_j, xs, w_gu, b_gu.reshape(N_EXPERTS, 1, 2 * D_FF),
      b_gu.reshape(N_EXPERTS, 1, 2 * D_FF))


def _moe_down_kernel(be_ref, nu_ref, ne_ref, nj_ref, h_ref, w_hbm, b_ref, o_ref,
                     wbuf_ref, wb_ref, cnt_ref, sem):
    tn = wb_ref.shape[1]

    def tile_copies(e, j, slot):
        return [pltpu.make_async_copy(w_hbm.at[e, :, pl.ds(pl.multiple_of(j * tn, tn), tn)],
                                      wbuf_ref.at[slot], sem.at[slot])]

    def consume(slot):
        wb_ref[...] = wbuf_ref[slot].astype(BF16)

    used = _stream_expert_weights(be_ref, nu_ref, ne_ref, nj_ref, cnt_ref, tile_copies, consume)

    @pl.when(used)
    def _():
        o_ref[...] = jnp.dot(h_ref[...], wb_ref[...], preferred_element_type=F32) + b_ref[0]

    @pl.when(jnp.logical_not(used))
    def _():
        o_ref[...] = jnp.zeros_like(o_ref)


def _moe_down(hid, blk_e, n_used, w_down, b_down, tn=D_MODEL):
    n_blocks = hid.shape[0] // MOE_BLK
    nn = D_MODEL // tn
    next_e, next_j = _next_tiles(blk_e, n_used, nn)
    grid_spec = pltpu.PrefetchScalarGridSpec(
        num_scalar_prefetch=4,
        grid=(nn, n_blocks),
        in_specs=[pl.BlockSpec((MOE_BLK, D_FF), lambda j, b, be, *_: (b, 0)),
                  pl.BlockSpec(memory_space=pl.ANY),
                  pl.BlockSpec((1, 1, tn), lambda j, b, be, *_: (be[b], 0, j))],
        out_specs=pl.BlockSpec((MOE_BLK, tn), lambda j, b, be, *_: (b, j)),
        scratch_shapes=[pltpu.VMEM((2, D_FF, tn), F32), pltpu.VMEM((D_FF, tn), BF16),
                        pltpu.SMEM((1,), jnp.int32), pltpu.SemaphoreType.DMA((2,))])
    return pl.pallas_call(
        _moe_down_kernel,
        out_shape=jax.ShapeDtypeStruct((hid.shape[0], D_MODEL), F32),
        grid_spec=grid_spec,
        compiler_params=_cp(("arbitrary", "arbitrary")),
        name="moe_down",
    )(blk_e, n_used, next_e, next_j, hid, w_down, b_down.reshape(N_EXPERTS, 1, D_MODEL))


def _combine_kernel(slot_ref, h_ref, gate_ref, gn_ref, y_ref, o_ref, buf_ref, sem):
    tm = h_ref.shape[0]

    def copy(r, k):
        return pltpu.make_async_copy(y_ref.at[pl.ds(slot_ref[0, 0, r * TOP_K + k], 1)],
                                     buf_ref.at[k, pl.ds(r, 1)], sem)

    def start(r, c):
        for k in range(TOP_K):
            copy(r, k).start()
        return c

    def wait(r, c):
        for k in range(TOP_K):
            copy(r, k).wait()
        return c

    lax.fori_loop(0, tm, start, 0)
    lax.fori_loop(0, tm, wait, 0)
    gate = gate_ref[...]
    acc = h_ref[...]
    for k in range(TOP_K):
        acc = acc + gate[:, k:k + 1] * buf_ref[k]
    o_ref[...] = acc * lax.rsqrt(jnp.mean(acc * acc, axis=-1, keepdims=True) + EPS) * gn_ref[...]


def _combine(h, gate, slots, y, norm_g):
    n = h.shape[0]
    tm = min(256, n)
    nt = n // tm
    return pl.pallas_call(
        _combine_kernel,
        out_shape=jax.ShapeDtypeStruct((n, D_MODEL), F32),
        grid=(nt,),
        in_specs=[pl.BlockSpec((1, 1, tm * TOP_K), lambda i: (i, 0, 0), memory_space=pltpu.SMEM),
                  pl.BlockSpec((tm, D_MODEL), lambda i: (i, 0)),
                  pl.BlockSpec((tm, LANE), lambda i: (i, 0)),
                  pl.BlockSpec((1, D_MODEL), lambda i: (0, 0)),
                  pl.BlockSpec(memory_space=pl.ANY)],
        out_specs=pl.BlockSpec((tm, D_MODEL), lambda i: (i, 0)),
        scratch_shapes=[pltpu.VMEM((TOP_K, tm, D_MODEL), F32), pltpu.SemaphoreType.DMA(())],
        compiler_params=_cp(("arbitrary",)),
        name="moe_combine",
    )(slots.reshape(nt, 1, tm * TOP_K), h, gate, norm_g.reshape(1, D_MODEL), y)


def _moe_plan(ids_p, ids_s):
    n_p, n_s = ids_p.shape[0], ids_s.shape[0]
    pad = (-n_s) % 256
    ids = jnp.concatenate([ids_p, ids_s, jnp.full((pad, LANE), -1, jnp.int32)], axis=0)
    rank, cnt = _expert_ranks(ids)
    counts = cnt[0, :N_EXPERTS]
    padded = (counts + MOE_BLK - 1) // MOE_BLK * MOE_BLK
    pad_end = jnp.cumsum(padded)
    pad_start = pad_end - padded
    n_pairs = (n_p + n_s) * TOP_K
    n_blocks = -(-n_pairs // MOE_BLK) + N_EXPERTS
    blk_start = jnp.arange(n_blocks, dtype=jnp.int32) * MOE_BLK
    blk_e = jnp.minimum(jnp.sum((pad_end[None, :] <= blk_start[:, None]).astype(jnp.int32), axis=1),
                        N_EXPERTS - 1)
    n_used = (pad_end[-1:] // MOE_BLK).astype(jnp.int32)
    e = ids[:n_p + n_s, :TOP_K]
    slot = (pad_start[e] + rank[:n_p + n_s, :TOP_K]).astype(jnp.int32)
    return slot[:n_p], slot[n_p:], blk_e, n_used, n_blocks


def _moe(hn_p, hn_s, h_p, h_s, ids_p, gate_p, ids_s, gate_s, w_gu, b_gu, w_down, b_down, norm_final):
    slot_p, slot_s, blk_e, n_used, n_blocks = _moe_plan(ids_p, ids_s)
    xs = jnp.zeros((n_blocks * MOE_BLK, D_MODEL), F32)
    xs = _scatter_rows(hn_p, slot_p, xs)
    xs = _scatter_rows(hn_s, slot_s, xs)
    hid = _moe_up(xs, blk_e, n_used, w_gu, b_gu)
    y = _moe_down(hid, blk_e, n_used, w_down, b_down)
    return (_combine(h_p, gate_p, slot_p, y, norm_final),
            _combine(h_s, gate_s, slot_s, y, norm_final))


def _proj_sample_kernel(x_ref, g_ref, w_ref, o_ref):
    x = x_ref[...]
    xn = x * lax.rsqrt(jnp.mean(x * x, axis=-1, keepdims=True) + EPS) * g_ref[...]
    o_ref[...] = jnp.dot(xn, w_ref[...], precision=HIGHEST, preferred_element_type=F32)


def _proj_sample(x, norm_g, w_in):
    n = x.shape[0]
    tn = 1024
    return pl.pallas_call(
        _proj_sample_kernel,
        out_shape=jax.ShapeDtypeStruct((n, N_IN), F32),
        grid=(pl.cdiv(N_IN, tn),),
        in_specs=[pl.BlockSpec((n, D_MODEL), lambda j: (0, 0)),
                  pl.BlockSpec((1, D_MODEL), lambda j: (0, 0)),
                  pl.BlockSpec((D_MODEL, tn), lambda j: (0, j))],
        out_specs=pl.BlockSpec((n, tn), lambda j: (0, j)),
        compiler_params=_cp(("parallel",)),
        name="proj_sample",
    )(x, norm_g.reshape(1, D_MODEL), w_in)


def _gdn_sample_kernel(ext_ref, cw_ref, z_ref, br_ref, ar_ref, al_ref, dt_ref, ng_ref, s_ref,
                       o_ref, so_ref):
    H = GDN_HEADS
    y = ext_ref[0, 0] * cw_ref[0]
    for w in range(1, CONV_W):
        y = y + ext_ref[0, w] * cw_ref[w]
    y = _silu(y)
    q = _l2norm(y[0:H]) * (GDN_DK ** -0.5)
    k = _l2norm(y[H:2 * H])
    v = y[2 * H:3 * H]
    beta = _sigmoid(br_ref[0])
    eg = jnp.exp(-jnp.exp(al_ref[...]) * _softplus(ar_ref[0] + dt_ref[...]))
    qk = jnp.sum(q * k, axis=-1, keepdims=True)
    qT = q.T
    kT = k.T
    rows = []
    for h in range(H):
        S = s_ref[0, h]
        kc = kT[:, h:h + 1]
        qc = qT[:, h:h + 1]
        e = eg[h:h + 1]
        k_s = jnp.sum(S * kc, axis=0, keepdims=True)
        q_s = jnp.sum(S * qc, axis=0, keepdims=True)
        v_new = beta[h:h + 1] * (v[h:h + 1] - e * k_s)
        rows.append(e * q_s + qk[h:h + 1] * v_new)
        so_ref[0, h] = e * S + kc * v_new
    o = jnp.concatenate(rows, axis=0)
    on = o * lax.rsqrt(jnp.mean(o * o, axis=-1, keepdims=True) + EPS) * ng_ref[...]
    o_ref[0] = on * _silu(z_ref[0])


def _gdn_sample(ext, conv_w, z, b_raw, a_raw, a_log, dt_bias, norm_g, state):
    bsz = ext.shape[0]
    H = GDN_HEADS
    col = lambda a: a.reshape(bsz, H, 1)
    hcol = lambda a: a.astype(F32).reshape(H, 1)
    return pl.pallas_call(
        _gdn_sample_kernel,
        out_shape=(jax.ShapeDtypeStruct((bsz, H, GDN_DV), F32),
                   jax.ShapeDtypeStruct(state.shape, F32)),
        grid=(bsz,),
        in_specs=[pl.BlockSpec((1, CONV_W, 3 * H, LANE), lambda b: (b, 0, 0, 0)),
                  pl.BlockSpec((CONV_W, 3 * H, LANE), lambda b: (0, 0, 0)),
                  pl.BlockSpec((1, H, GDN_DV), lambda b: (b, 0, 0)),
                  pl.BlockSpec((1, H, 1), lambda b: (b, 0, 0)),
                  pl.BlockSpec((1, H, 1), lambda b: (b, 0, 0)),
                  pl.BlockSpec((H, 1), lambda b: (0, 0)),
                  pl.BlockSpec((H, 1), lambda b: (0, 0)),
                  pl.BlockSpec((1, GDN_DV), lambda b: (0, 0)),
                  pl.BlockSpec((1, H, GDN_DK, GDN_DV), lambda b: (b, 0, 0, 0))],
        out_specs=(pl.BlockSpec((1, H, GDN_DV), lambda b: (b, 0, 0)),
                   pl.BlockSpec((1, H, GDN_DK, GDN_DV), lambda b: (b, 0, 0, 0))),
        compiler_params=_cp(("parallel",)),
        name="gdn_sample",
    )(ext, conv_w.reshape(CONV_W, 3 * H, LANE), z.reshape(bsz, H, GDN_DV), col(b_raw), col(a_raw),
      hcol(a_log), hcol(dt_bias), norm_g.reshape(1, GDN_DV), state)


PAGES_PER_STEP = 8


def _page_sums_kernel(pt_ref, *refs):
    del pt_ref
    x_refs, w_ref, o_ref = refs[:-2], refs[-2], refs[-1]
    nch = PAGE_SIZE // CMP_STRIDE
    G = NSA_KV_HEADS
    for i, x_ref in enumerate(x_refs):
        for e in range(2):
            x4 = x_ref[0, 0, :, e].reshape(nch, CMP_STRIDE, G, HEAD_DIM)
            for half in range(2):
                w = w_ref[2 * e + half][None, :, None, :]
                o_ref[0, 2 * e + half, i * nch:(i + 1) * nch] = jnp.sum(x4 * w, axis=1)


def _page_sums(kv_cache, layer, page_table, w4):
    bsz, n_pages = page_table.shape
    G = NSA_KV_HEADS
    nch = PAGE_SIZE // CMP_STRIDE
    pps = math.gcd(PAGES_PER_STEP, n_pages)

    def page(i):
        return pl.BlockSpec((1, 1, PAGE_SIZE, 2, G, HEAD_DIM),
                            lambda b, p, pt: (layer, pt[b * n_pages + p * pps + i], 0, 0, 0, 0))

    grid_spec = pltpu.PrefetchScalarGridSpec(
        num_scalar_prefetch=1,
        grid=(bsz, n_pages // pps),
        in_specs=[page(i) for i in range(pps)]
        + [pl.BlockSpec((4, CMP_STRIDE, HEAD_DIM), lambda b, p, pt: (0, 0, 0))],
        out_specs=pl.BlockSpec((1, 4, pps * nch, G, HEAD_DIM), lambda b, p, pt: (b, 0, p, 0, 0)))
    return pl.pallas_call(
        _page_sums_kernel,
        out_shape=jax.ShapeDtypeStruct((bsz, 4, n_pages * nch, G, HEAD_DIM), F32),
        grid_spec=grid_spec,
        compiler_params=_cp(("parallel", "arbitrary")),
        name="nsa_page_sums",
    )(page_table.reshape(-1), *([kv_cache] * pps), w4)


def _row_to_col(r):
    n = r.shape[1]
    eye = lax.broadcasted_iota(jnp.int32, (n, n), 0) == lax.broadcasted_iota(jnp.int32, (n, n), 1)
    return jnp.sum(jnp.where(eye, jnp.broadcast_to(r, (n, n)), 0), axis=1, keepdims=True)


def _hdot(a, b):
    return jnp.dot(a, b, precision=HIGHEST, preferred_element_type=F32)


def _hdot_nt(a, b):
    return lax.dot_general(a, b, (((1,), (1,)), ((), ())), precision=HIGHEST,
                           preferred_element_type=F32)


def _nsa_sample_select_kernel(sums_ref, new_ref, w_ref, proj_ref, q_ref, ocmp_ref, idx_ref, *, past):
    G = NSA_KV_HEADS
    nch = sums_ref.shape[2]
    row = lax.broadcasted_iota(jnp.int32, (nch, HEAD_DIM), 0)

    def pooled(e, g):
        new_s1 = new_ref[0, e, g:g + 1, :] * w_ref[2 * e + 1, 0:1]
        s1 = sums_ref[0, 2 * e + 1, :, g, :]
        nxt = jnp.where(row < nch - 1, pltpu.roll(s1, nch - 1, 0), new_s1)
        return sums_ref[0, 2 * e, :, g, :] + nxt

    n_slc = past // SEL_BLOCK + 1
    nsp = -(-n_slc // LANE) * LANE
    t = past
    cmp_end = CMP_STRIDE * lax.broadcasted_iota(jnp.int32, (1, nch), 1) + (CMP_LEN - 1)
    ci = CMP_STRIDE * lax.broadcasted_iota(jnp.int32, (nch, nsp), 0)
    bj = SEL_BLOCK * lax.broadcasted_iota(jnp.int32, (nch, nsp), 1)
    cmap = ((ci < bj + SEL_BLOCK) & (ci + CMP_LEN > bj)).astype(F32)
    j_idx = lax.broadcasted_iota(jnp.int32, (1, nsp), 1)
    jt = t // SEL_BLOCK
    force = (j_idx == 0) | (j_idx == jt) | (j_idx == jt - 1)
    i_idx = lax.broadcasted_iota(jnp.int32, (nsp, nsp), 0)
    jj_idx = lax.broadcasted_iota(jnp.int32, (nsp, nsp), 1)
    lane = lax.broadcasted_iota(jnp.int32, (1, LANE), 1)
    idx_rows = []
    for g in range(G):
        ck = _hdot(pooled(0, g), proj_ref[0])
        cv = _hdot(pooled(1, g), proj_ref[1])
        q = q_ref[0, g] * (HEAD_DIM ** -0.5)
        p = _masked_softmax(_hdot_nt(q, ck), cmp_end <= t)
        ocmp_ref[0, g] = _hdot(p, cv)
        psum = jnp.sum(p[0:NSA_HPG], axis=0, keepdims=True)
        imp = _hdot(psum, cmap)
        score = jnp.where(j_idx > jt, NEG, jnp.where(force, FORCE, imp))
        sc = _row_to_col(score)
        ahead = (sc > score) | ((sc == score) & (i_idx < jj_idx))
        cnt = jnp.sum(ahead.astype(jnp.int32), axis=0, keepdims=True)
        sel = (cnt < N_SEL) & (score > NEG / 2)
        selc = _row_to_col(sel.astype(jnp.int32))
        before = jnp.sum(jnp.where(i_idx < jj_idx, selc, 0), axis=0, keepdims=True)
        out = jnp.full((1, LANE), -1, jnp.int32)
        for k in range(N_SEL):
            hit = sel & (before == k)
            jk = jnp.sum(jnp.where(hit, j_idx + 1, 0), axis=1, keepdims=True) - 1
            out = jnp.where(lane == k, jk, out)
        idx_rows.append(out)
    idx_ref[0] = jnp.concatenate(idx_rows + [jnp.full((8 - G, LANE), -1, jnp.int32)], axis=0)


def _nsa_sample_select(sums, new6, w4, cmp_proj, q8, past):
    bsz = sums.shape[0]
    G = NSA_KV_HEADS
    return pl.pallas_call(
        functools.partial(_nsa_sample_select_kernel, past=past),
        out_shape=(jax.ShapeDtypeStruct((bsz, G, 8, HEAD_DIM), F32),
                   jax.ShapeDtypeStruct((bsz, 8, LANE), jnp.int32)),
        grid=(bsz,),
        in_specs=[pl.BlockSpec((1,) + sums.shape[1:], lambda b: (b, 0, 0, 0, 0)),
                  pl.BlockSpec((1, 6, G, HEAD_DIM), lambda b: (b, 0, 0, 0)),
                  pl.BlockSpec((4, CMP_STRIDE, HEAD_DIM), lambda b: (0, 0, 0)),
                  pl.BlockSpec((2, HEAD_DIM, HEAD_DIM), lambda b: (0, 0, 0)),
                  pl.BlockSpec((1, G, 8, HEAD_DIM), lambda b: (b, 0, 0, 0))],
        out_specs=(pl.BlockSpec((1, G, 8, HEAD_DIM), lambda b: (b, 0, 0, 0)),
                   pl.BlockSpec((1, 8, LANE), lambda b: (b, 0, 0))),
        compiler_params=_cp(("parallel",)),
        name="nsa_sample_select",
    )(sums, new6, w4, cmp_proj, q8)


def _nsa_sample_attend_kernel(idx_ref, pt_ref, *refs, past, n_pages):
    del pt_ref
    kvb = refs[0:N_SEL]
    new_ref, win_ref, q_ref, ocmp_ref, gate_ref, o_ref = refs[N_SEL:]
    b = pl.program_id(0)
    g = pl.program_id(1)
    G = NSA_KV_HEADS
    t = past
    n_past_blk = past // SEL_BLOCK
    q = q_ref[0, 0] * (HEAD_DIM ** -0.5)
    k_new = new_ref[0, 2, pl.ds(g, 1), :]
    v_new = new_ref[0, 3, pl.ds(g, 1), :]
    kw_new = new_ref[0, 4, pl.ds(g, 1), :]
    vw_new = new_ref[0, 5, pl.ds(g, 1), :]

    r64 = lax.broadcasted_iota(jnp.int32, (SEL_BLOCK, HEAD_DIM), 0)
    ks, vs, kpos = [], [], []
    r_lane = lax.broadcasted_iota(jnp.int32, (1, SEL_BLOCK), 1)
    for k in range(N_SEL):
        j = idx_ref[(b * G + g) * N_SEL + k]
        is_new = j >= n_past_blk
        ks.append(jnp.where(is_new, jnp.where(r64 == 0, k_new, 0.0), kvb[k][0, 0, :, 0, g, :]))
        vs.append(jnp.where(is_new, jnp.where(r64 == 0, v_new, 0.0), kvb[k][0, 0, :, 1, g, :]))
        kpos.append(jnp.where(j >= 0, j * SEL_BLOCK + r_lane, t + 1))
    ksel = jnp.concatenate(ks, axis=0)
    vsel = jnp.concatenate(vs, axis=0)
    kpos = jnp.concatenate(kpos, axis=1)
    p = _masked_softmax(_hdot_nt(q, ksel), kpos <= t)
    o_slc = _hdot(p, vsel)

    L = win_ref.shape[2]
    wpos = (past - L) + lax.broadcasted_iota(jnp.int32, (1, L), 1)
    ok = (wpos > t - WINDOW) & (wpos >= 0)
    s_w = jnp.where(ok, _hdot_nt(q, win_ref[0, 0, :, 0, g, :]), NEG)
    s_n = jnp.sum(q * kw_new, axis=-1, keepdims=True)
    m = jnp.maximum(jnp.max(s_w, axis=-1, keepdims=True), s_n)
    p_w = jnp.exp(s_w - m) * ok.astype(F32)
    p_n = jnp.exp(s_n - m)
    den = jnp.maximum(jnp.sum(p_w, axis=-1, keepdims=True) + p_n, 1e-30)
    o_win = (_hdot(p_w, win_ref[0, 0, :, 1, g, :]) + p_n * vw_new) / den

    gb = _sigmoid(gate_ref[0, 0])
    o_ref[0, 0] = ocmp_ref[0, 0] * gb[:, 0:1] + o_slc * gb[:, 1:2] + o_win * gb[:, 2:3]


def _nsa_sample_attend(idx, page_table, kv_cache, win_cache, layer, new6, q8, o_cmp, gate8, past):
    bsz, n_pages = page_table.shape
    G = NSA_KV_HEADS
    n_past_blk = past // SEL_BLOCK
    bpp = PAGE_SIZE // SEL_BLOCK
    L = win_cache.shape[2]

    def blk(k):
        def imap(b, g, idx_ref, pt_ref):
            j = jnp.clip(idx_ref[(b * G + g) * N_SEL + k], 0, n_past_blk - 1)
            return (layer, pt_ref[b * n_pages + j // bpp], j % bpp, 1, 0, 0)
        return pl.BlockSpec((1, 1, SEL_BLOCK, 2, G, HEAD_DIM), imap)

    per_bg = pl.BlockSpec((1, 1, 8, HEAD_DIM), lambda b, g, i, p: (b, g, 0, 0))
    grid_spec = pltpu.PrefetchScalarGridSpec(
        num_scalar_prefetch=2,
        grid=(bsz, G),
        in_specs=([blk(k) for k in range(N_SEL)]
                  + [pl.BlockSpec((1, 6, G, HEAD_DIM), lambda b, g, i, p: (b, 0, 0, 0)),
                     pl.BlockSpec((1, 1, L, 2, G, HEAD_DIM), lambda b, g, i, p: (layer, b, 0, 0, 0, 0)),
                     per_bg, per_bg, per_bg]),
        out_specs=per_bg)
    return pl.pallas_call(
        functools.partial(_nsa_sample_attend_kernel, past=past, n_pages=n_pages),
        out_shape=jax.ShapeDtypeStruct((bsz, G, 8, HEAD_DIM), F32),
        grid_spec=grid_spec,
        compiler_params=_cp(("parallel", "arbitrary")),
        name="nsa_sample_attend",
    )(idx, page_table.reshape(-1), *([kv_cache] * N_SEL), new6, win_cache, q8, o_cmp, gate8)


def _prompt_mixers(x2d, batch, seq, norm_mix, w_in, conv_w, a_log, dt_bias, gdn_norm, cmp_pos,
                   cmp_proj, w_out_a, w_out_b):
    w_main = jnp.concatenate([w_in[:, OFF_QKV:OFF_BRAW], w_in[:, OFF_MERGE:],
                              w_in[:, OFF_QB:OFF_GATEB]], axis=1).astype(BF16)
    w_small = jnp.concatenate([w_in[:, OFF_BRAW:OFF_QB], w_in[:, OFF_GATEB:OFF_MERGE],
                               jnp.zeros((D_MODEL, LANE - 2 * GDN_HEADS - 3 * NSA_HEADS), w_in.dtype)],
                              axis=1).astype(BF16)
    proj, small = _proj_prompt(x2d, norm_mix, w_main, w_small)
    o_a, s_new = _gdn_prompt(proj, small, conv_w, _head_params(a_log, dt_bias), gdn_norm, batch, seq)
    comp = _compress_prompt(proj, cmp_pos, cmp_proj, batch, seq)
    o_b = _nsa_prompt(proj, small, comp, batch, seq)
    y = _merge(o_a, o_b, w_out_a.astype(BF16), w_out_b.astype(BF16), proj, CB_MERGE,
               CB_MERGE + D_MODEL // LANE, False)
    G = NSA_KV_HEADS
    kv6 = proj[:, CB_KVB * LANE:].reshape(batch, seq, 6, G, HEAD_DIM)
    conv_new = proj.reshape(batch, seq, N_MAIN)[:, seq - (CONV_W - 1):, :CONV_DIM]
    return y, kv6[:, :, :4], kv6[:, max(seq - WINDOW, 0):, 4:], s_new, conv_new


def _sample_mixers(x2d, norm_mix, w_in, conv_w, a_log, dt_bias, gdn_norm, cmp_pos, cmp_proj,
                   w_out_a, w_out_b, kv_cache, win_cache, layer, state, conv_buf, page_table):
    bsz = x2d.shape[0]
    win_buf = win_cache[layer]
    G = NSA_KV_HEADS
    past = page_table.shape[1] * PAGE_SIZE
    ps = _proj_sample(x2d, norm_mix, w_in)
    qkv, z = ps[:, OFF_QKV:OFF_Z], ps[:, OFF_Z:OFF_BRAW]
    b_raw, a_raw = ps[:, OFF_BRAW:OFF_ARAW], ps[:, OFF_ARAW:OFF_QB]
    q_b, kv_b = ps[:, OFF_QB:OFF_KVB], ps[:, OFF_KVB:OFF_GATEB]
    gate_b, merge_raw = ps[:, OFF_GATEB:OFF_MERGE], ps[:, OFF_MERGE:]

    ext = jnp.concatenate([conv_buf.astype(F32), qkv[:, None]], axis=1)
    o_a, s_new = _gdn_sample(ext.reshape(bsz, CONV_W, 3 * GDN_HEADS, LANE), conv_w, z, b_raw, a_raw,
                             a_log, dt_bias, gdn_norm, state.astype(F32))

    w4 = cmp_pos.astype(F32).reshape(4, CMP_STRIDE, HEAD_DIM)
    sums = _page_sums(kv_cache, layer, page_table, w4)
    new6 = kv_b.reshape(bsz, 6, G, HEAD_DIM)
    q8 = jnp.pad(q_b.reshape(bsz, G, NSA_HPG, HEAD_DIM), ((0, 0), (0, 0), (0, 8 - NSA_HPG), (0, 0)))
    o_cmp, idx8 = _nsa_sample_select(sums, new6, w4, cmp_proj.astype(F32), q8, past)
    gate8 = jnp.pad(gate_b.reshape(bsz, G, NSA_HPG, 3),
                    ((0, 0), (0, 0), (0, 8 - NSA_HPG), (0, LANE - 3)))
    o_b8 = _nsa_sample_attend(idx8[:, :G, :N_SEL].reshape(-1), page_table, kv_cache, win_cache,
                              layer, new6, q8, o_cmp, gate8, past)
    o_b = o_b8[:, :, :NSA_HPG].reshape(bsz, NSA_HEADS * HEAD_DIM)

    y = _merge(o_a.reshape(bsz, GDN_HEADS * GDN_DV), o_b, w_out_a, w_out_b, merge_raw, 0,
               D_MODEL // LANE, True)
    kv6 = kv_b.reshape(bsz, 1, 6, G, HEAD_DIM)
    win_ext = jnp.concatenate([win_buf.astype(F32), kv6[:, :, 4:]], axis=1)
    keep = min(WINDOW, past + 1)
    return y, kv6[:, :, :4], win_ext[:, win_ext.shape[1] - keep:], s_new, ext[:, 1:]


def kernel(x_prompt, x_sample, cache_nsa_kv, cache_nsa_win, state_gdn, state_conv, page_table, norm_mix, w_in, conv_w, gdn_a_log, gdn_dt_bias, gdn_norm, w_out_a, cmp_pos, cmp_proj, w_out_b, w_out, norm_ffn, w_router, b_router, w_gu, b_gu, w_down, b_down, norm_final):
    assert w_in.shape[0] == 1 and x_sample.shape[1] == 1, "one layer, one new token per sequence"
    bp, seq, _ = x_prompt.shape
    bs = x_sample.shape[0]
    xp = x_prompt.reshape(bp * seq, D_MODEL)
    xs = x_sample.reshape(bs, D_MODEL)
    l = 0
    y_p, kv_p, win_p, gdn_p, conv_p = _prompt_mixers(
        xp, bp, seq, norm_mix[l], w_in[l], conv_w[l], gdn_a_log[l], gdn_dt_bias[l], gdn_norm[l],
        cmp_pos[l], cmp_proj[l], w_out_a[l], w_out_b[l])
    y_s, kv_s, win_s, gdn_s, conv_s = _sample_mixers(
        xs, norm_mix[l], w_in[l], conv_w[l], gdn_a_log[l], gdn_dt_bias[l], gdn_norm[l], cmp_pos[l],
        cmp_proj[l], w_out_a[l], w_out_b[l], cache_nsa_kv, cache_nsa_win, l, state_gdn[l],
        state_conv[l], page_table)

    wr = jnp.pad(w_router[l].astype(F32), ((0, 0), (0, LANE - N_EXPERTS)))
    br = jnp.pad(b_router[l].astype(F32), (0, LANE - N_EXPERTS)).reshape(1, LANE)
    h_p, hn_p, ids_p, gate_p = _outproj_router(xp, y_p, w_out[l].astype(BF16), norm_ffn[l], wr, br, False)
    h_s, hn_s, ids_s, gate_s = _outproj_router(xs, y_s, w_out[l], norm_ffn[l], wr, br, True)
    out_p, out_s = _moe(hn_p, hn_s, h_p, h_s, ids_p, gate_p, ids_s, gate_s, w_gu[l], b_gu[l],
                        w_down[l], b_down[l], norm_final)
    return (out_p.reshape(x_prompt.shape), out_s.reshape(x_sample.shape),
            kv_p[None], kv_s[None], win_p[None], win_s[None],
            gdn_p[None].astype(state_gdn.dtype), gdn_s[None].astype(state_gdn.dtype),
            conv_p[None], conv_s[None])
```

```python
import functools
import math

import jax
import jax.numpy as jnp
from jax import lax
from jax.experimental import pallas as pl
from jax.experimental.pallas import tpu as pltpu

F32 = jnp.float32
BF16 = jnp.bfloat16
HIGHEST = lax.Precision.HIGHEST

D_MODEL = 2048
GDN_HEADS = 16
GDN_DK = 128
GDN_DV = 128
CONV_W = 4
CONV_DIM = GDN_HEADS * (2 * GDN_DK + GDN_DV)
NSA_HEADS = 16
NSA_KV_HEADS = 4
HEAD_DIM = 128
NSA_HPG = NSA_HEADS // NSA_KV_HEADS
CMP_LEN = 32
CMP_STRIDE = 16
SEL_BLOCK = 64
N_SEL = 16
WINDOW = 512
PAGE_SIZE = 128
N_EXPERTS = 32
TOP_K = 4
D_FF = 2048
SWIGLU_LIMIT = 7.0
SWIGLU_ALPHA = 1.702
EPS = 1e-6
NEG = -1e30
FORCE = 1e6

IN_SPLITS = (CONV_DIM, GDN_HEADS * GDN_DV, GDN_HEADS, GDN_HEADS, NSA_HEADS * HEAD_DIM,
             6 * NSA_KV_HEADS * HEAD_DIM, 3 * NSA_HEADS, 2 * D_MODEL)
_OFF = [0]
for _s in IN_SPLITS:
    _OFF.append(_OFF[-1] + _s)
(OFF_QKV, OFF_Z, OFF_BRAW, OFF_ARAW, OFF_QB, OFF_KVB, OFF_GATEB, OFF_MERGE, N_IN) = _OFF

LANE = 128
CB_Q, CB_K, CB_V = 0, GDN_HEADS, 2 * GDN_HEADS
CB_Z = CONV_DIM // LANE
CB_MERGE = CB_Z + GDN_HEADS
CB_QB = CB_MERGE + 2 * D_MODEL // LANE
CB_KVB = CB_QB + NSA_HEADS
N_MAIN = (CB_KVB + 6 * NSA_KV_HEADS) * LANE
SM_BRAW, SM_ARAW, SM_GATE = 0, GDN_HEADS, 2 * GDN_HEADS

GDN_CHUNK = 128
GDN_HB = 8
NSA_TQ = 256
NSA_KT = 1024
MOE_BLK = 256
VMEM_LIMIT = 60000 * 1024


def _cp(sem, vmem=VMEM_LIMIT):
    return pltpu.CompilerParams(dimension_semantics=sem, vmem_limit_bytes=vmem)


def _lane_pick(x, lane):
    ids = lax.broadcasted_iota(jnp.int32, x.shape, 1)
    return jnp.sum(jnp.where(ids == lane, x, 0.0), axis=1, keepdims=True)


def _sigmoid(x):
    return 1.0 / (1.0 + jnp.exp(-x))


def _silu(x):
    return x * _sigmoid(x)


def _softplus(x):
    return jnp.maximum(x, 0.0) + jnp.log(1.0 + jnp.exp(-jnp.abs(x)))


def _proj_kernel(x_ref, g_ref, w_ref, ws_ref, o_ref, os_ref, xn_ref):
    @pl.when(pl.program_id(1) == 0)
    def _():
        x = x_ref[...]
        ms = jnp.mean(x * x, axis=-1, keepdims=True)
        xn = (x * lax.rsqrt(ms + EPS) * g_ref[...]).astype(BF16)
        xn_ref[...] = xn
        os_ref[...] = jnp.dot(xn, ws_ref[...], preferred_element_type=F32)

    o_ref[...] = jnp.dot(xn_ref[...], w_ref[...], preferred_element_type=F32)


REPACK_TN = 512
_RUNS = ((0, OFF_QKV), (OFF_BRAW - OFF_QKV, OFF_MERGE),
         (OFF_BRAW - OFF_QKV + N_IN - OFF_MERGE, OFF_QB))


def _repack_kernel(wt_hbm, o_ref, os_ref, buf_ref, sbuf_ref, sem, ssem):
    j = pl.program_id(0)
    nj = pl.num_programs(0)
    starts = [r[0] // REPACK_TN for r in _RUNS]

    def rows_copy(jj, slot):
        row = jj * REPACK_TN + (_RUNS[0][1] - _RUNS[0][0])
        for r in range(1, len(_RUNS)):
            row = jnp.where(jj >= starts[r], jj * REPACK_TN + (_RUNS[r][1] - _RUNS[r][0]), row)
        return pltpu.make_async_copy(wt_hbm.at[pl.ds(pl.multiple_of(row, 8), REPACK_TN), :],
                                     buf_ref.at[slot], sem.at[slot])

    def small_copies():
        return [pltpu.make_async_copy(wt_hbm.at[pl.ds(OFF_BRAW, SM_GATE), :],
                                      sbuf_ref.at[pl.ds(0, SM_GATE), :], ssem),
                pltpu.make_async_copy(wt_hbm.at[pl.ds(OFF_GATEB, 3 * NSA_HEADS), :],
                                      sbuf_ref.at[pl.ds(SM_GATE, 3 * NSA_HEADS), :], ssem)]

    @pl.when(j == 0)
    def _():
        rows_copy(0, 0).start()
        sbuf_ref[...] = jnp.zeros_like(sbuf_ref)
        for c in small_copies():
            c.start()
        for c in small_copies():
            c.wait()
        os_ref[...] = sbuf_ref[...].T.astype(BF16)

    slot = lax.rem(j, 2)

    @pl.when(j + 1 < nj)
    def _():
        rows_copy(j + 1, 1 - slot).start()

    rows_copy(j, slot).wait()
    o_ref[...] = buf_ref[slot].T.astype(BF16)


def _repack_w_in(w_t):
    assert all(r[0] % REPACK_TN == 0 and (r[1] - r[0]) % 8 == 0 for r in _RUNS)
    assert N_MAIN % REPACK_TN == 0 and OFF_QB - OFF_BRAW == SM_GATE
    return pl.pallas_call(
        _repack_kernel,
        out_shape=(jax.ShapeDtypeStruct((D_MODEL, N_MAIN), BF16),
                   jax.ShapeDtypeStruct((D_MODEL, LANE), BF16)),
        grid=(N_MAIN // REPACK_TN,),
        in_specs=[pl.BlockSpec(memory_space=pl.ANY)],
        out_specs=(pl.BlockSpec((D_MODEL, REPACK_TN), lambda j: (0, j)),
                   pl.BlockSpec((D_MODEL, LANE), lambda j: (0, 0))),
        scratch_shapes=[pltpu.VMEM((2, REPACK_TN, D_MODEL), F32), pltpu.VMEM((LANE, D_MODEL), F32),
                        pltpu.SemaphoreType.DMA((2,)), pltpu.SemaphoreType.DMA(())],
        compiler_params=_cp(("arbitrary",)),
        name="repack_w_in",
    )(w_t)


def _proj_prompt(x, norm_g, w_main, w_small):
    n = x.shape[0]
    tm = min(1024, n)
    tn = 512
    return pl.pallas_call(
        _proj_kernel,
        out_shape=(jax.ShapeDtypeStruct((n, N_MAIN), F32), jax.ShapeDtypeStruct((n, LANE), F32)),
        grid=(n // tm, N_MAIN // tn),
        in_specs=[pl.BlockSpec((tm, D_MODEL), lambda i, j: (i, 0)),
                  pl.BlockSpec((1, D_MODEL), lambda i, j: (0, 0)),
                  pl.BlockSpec((D_MODEL, tn), lambda i, j: (0, j)),
                  pl.BlockSpec((D_MODEL, LANE), lambda i, j: (0, 0))],
        out_specs=(pl.BlockSpec((tm, tn), lambda i, j: (i, j)),
                   pl.BlockSpec((tm, LANE), lambda i, j: (i, 0))),
        scratch_shapes=[pltpu.VMEM((tm, D_MODEL), BF16)],
        compiler_params=_cp(("parallel", "arbitrary")),
        name="proj_prompt",
    )(x, norm_g.reshape(1, D_MODEL), w_main, w_small)


def _shift_rows(x, prev8, s):
    xs = pltpu.roll(x, s, 0)
    ps = pltpu.roll(prev8, s, 0)
    row = lax.broadcasted_iota(jnp.int32, prev8.shape, 0)
    top = jnp.where(row < s, ps, xs[0:8])
    return jnp.concatenate([top, xs[8:]], axis=0)


def _conv_silu(x, prev8, cw):
    y = x * cw[CONV_W - 1:CONV_W]
    for s in range(1, CONV_W):
        y = y + _shift_rows(x, prev8, s) * cw[CONV_W - 1 - s:CONV_W - s]
    return _silu(y)


def _l2norm(x):
    return x * lax.rsqrt(jnp.sum(x * x, axis=-1, keepdims=True) + EPS)


def _cumsum_rows(x):
    c = x.shape[0]
    row = lax.broadcasted_iota(jnp.int32, x.shape, 0)
    s = 1
    while s < c:
        x = x + jnp.where(row >= s, pltpu.roll(x, s, 0), 0.0)
        s *= 2
    return x


def _bdot(a, b):
    return jnp.dot(a.astype(BF16), b.astype(BF16), preferred_element_type=F32)


def _bdot_nt(a, b):
    return lax.dot_general(a.astype(BF16), b.astype(BF16), (((1,), (1,)), ((), ())),
                           preferred_element_type=F32)


def _bdot_tn(a, b):
    return lax.dot_general(a.astype(BF16), b.astype(BF16), (((0,), (0,)), ((), ())),
                           preferred_element_type=F32)


INV_BASE = 16


def _unit_lower_inverses(mats, ri, ci):
    c = mats[0].shape[0]
    same = lambda s: (ri // s) == (ci // s)
    eye = jnp.where(ri == ci, 1.0, 0.0)
    blk = same(INV_BASE)
    ps = [jnp.where(blk, a, 0.0) for a in mats]
    ts = [eye - p for p in ps]
    n = 2
    while n < INV_BASE:
        ps = [_bdot(p, p) for p in ps]
        ts = [t + _bdot(t, p) for t, p in zip(ts, ps)]
        n *= 2
    s = INV_BASE
    while s < c:
        off = same(2 * s) & jnp.logical_not(same(s))
        tm = [_bdot(t, jnp.where(off, a, 0.0)) for t, a in zip(ts, mats)]
        ts = [t - _bdot(x, t) for t, x in zip(ts, tm)]
        s *= 2
    return ts


def _gdn_kernel(q_ref, k_ref, v_ref, z_ref, sm_ref, cwq_ref, cwk_ref, cwv_ref, hp_ref, ng_ref,
                o_ref, s_out_ref, s_ref, pq_ref, pk_ref, pv_ref):
    hg = pl.program_id(1)
    c = pl.program_id(2)
    C = GDN_CHUNK

    @pl.when(c == 0)
    def _():
        s_ref[...] = jnp.zeros_like(s_ref)
        pq_ref[...] = jnp.zeros_like(pq_ref)
        pk_ref[...] = jnp.zeros_like(pk_ref)
        pv_ref[...] = jnp.zeros_like(pv_ref)

    xq, xk, xv = q_ref[...], k_ref[...], v_ref[...]
    yq = _conv_silu(xq, pq_ref[...], cwq_ref[...])
    yk = _conv_silu(xk, pk_ref[...], cwk_ref[...])
    yv = _conv_silu(xv, pv_ref[...], cwv_ref[...])
    pq_ref[...] = xq[C - 8:]
    pk_ref[...] = xk[C - 8:]
    pv_ref[...] = xv[C - 8:]

    sm = sm_ref[...]
    hp = hp_ref[...]
    g_all = -jnp.exp(hp[0:1]) * _softplus(sm + hp[1:2])
    G_all = _cumsum_rows(g_all)
    ri = lax.broadcasted_iota(jnp.int32, (C, C), 0)
    ci = lax.broadcasted_iota(jnp.int32, (C, C), 1)
    z = _silu(z_ref[...])
    HS = range(GDN_HB)
    cols = [slice(i * LANE, (i + 1) * LANE) for i in HS]
    q = [_l2norm(yq[:, cols[i]]) * (GDN_DK ** -0.5) for i in HS]
    k = [_l2norm(yk[:, cols[i]]) for i in HS]
    v = [yv[:, cols[i]] for i in HS]
    beta = [_sigmoid(_lane_pick(sm, SM_BRAW + hg * GDN_HB + i)) for i in HS]
    G = [_lane_pick(G_all, SM_ARAW + hg * GDN_HB + i) for i in HS]
    eG = [jnp.exp(G[i]) for i in HS]
    g_last = [G[i][C - 1:C] for i in HS]
    Gb = [jnp.broadcast_to(G[i], (C, C)) for i in HS]
    decay = [jnp.exp(jnp.where(ri >= ci, Gb[i] - Gb[i].T, NEG)) for i in HS]
    kb = [k[i] * beta[i] for i in HS]
    kk = [_bdot_nt(kb[i], k[i]) for i in HS]
    qk = [_bdot_nt(q[i], k[i]) for i in HS]
    a_mat = [jnp.where(ri > ci, kk[i] * decay[i], 0.0) for i in HS]
    t_inv = _unit_lower_inverses(a_mat, ri, ci)
    rhs = [jnp.concatenate([v[i] * beta[i], kb[i] * eG[i]], axis=1) for i in HS]
    x = [_bdot(t_inv[i], rhs[i]) for i in HS]
    S = [s_ref[i] for i in HS]
    q_s = [_bdot(q[i] * eG[i], S[i]) for i in HS]
    w_s = [_bdot(x[i][:, GDN_DV:], S[i]) for i in HS]
    v_new = [x[i][:, :GDN_DV] - w_s[i] for i in HS]
    o = [q_s[i] + _bdot(qk[i] * decay[i], v_new[i]) for i in HS]
    kv = [_bdot_tn(k[i] * jnp.exp(g_last[i] - G[i]), v_new[i]) for i in HS]
    s_new = [S[i] * jnp.exp(g_last[i]) + kv[i] for i in HS]
    for i in HS:
        s_ref[i] = s_new[i]
        on = o[i] * lax.rsqrt(jnp.mean(o[i] * o[i], axis=-1, keepdims=True) + EPS) * ng_ref[...]
        o_ref[:, cols[i]] = (on * z[:, cols[i]]).astype(o_ref.dtype)

    @pl.when(c == pl.num_programs(2) - 1)
    def _():
        for i in HS:
            s_out_ref[0, i] = s_new[i]


def _head_params(a_log, dt_bias):
    hp = jnp.zeros((8, LANE), F32)
    hp = hp.at[0, SM_ARAW:SM_ARAW + GDN_HEADS].set(a_log.astype(F32))
    return hp.at[1, SM_ARAW:SM_ARAW + GDN_HEADS].set(dt_bias.astype(F32))


def _gdn_prompt(proj, small, conv_w, head_par, norm_g, batch, seq):
    C = GDN_CHUNK
    nc = seq // C
    H = GDN_HEADS
    HB = GDN_HB
    W = HB * LANE

    def rows(cb):
        return pl.BlockSpec((C, W), lambda b, h, c, cb=cb: (b * nc + c, cb // HB + h))

    def cw(cb):
        return pl.BlockSpec((CONV_W, W), lambda b, h, c, cb=cb: (0, cb // HB + h))

    return pl.pallas_call(
        _gdn_kernel,
        out_shape=(jax.ShapeDtypeStruct((batch * seq, H * GDN_DV), BF16),
                   jax.ShapeDtypeStruct((batch, H, GDN_DK, GDN_DV), F32)),
        grid=(batch, H // HB, nc),
        in_specs=[rows(CB_Q), rows(CB_K), rows(CB_V), rows(CB_Z),
                  pl.BlockSpec((C, LANE), lambda b, h, c: (b * nc + c, 0)),
                  cw(CB_Q), cw(CB_K), cw(CB_V),
                  pl.BlockSpec((8, LANE), lambda b, h, c: (0, 0)),
                  pl.BlockSpec((1, GDN_DV), lambda b, h, c: (0, 0))],
        out_specs=(pl.BlockSpec((C, W), lambda b, h, c: (b * nc + c, h)),
                   pl.BlockSpec((1, HB, GDN_DK, GDN_DV), lambda b, h, c: (b, h, 0, 0))),
        scratch_shapes=[pltpu.VMEM((HB, GDN_DK, GDN_DV), F32), pltpu.VMEM((8, W), F32),
                        pltpu.VMEM((8, W), F32), pltpu.VMEM((8, W), F32)],
        compiler_params=_cp(("parallel", "parallel", "arbitrary")),
        name="gdn_prompt",
    )(proj, proj, proj, proj, small, conv_w, conv_w, conv_w, head_par, norm_g.reshape(1, GDN_DV))


def _cmp_kernel(x_ref, w_ref, p_ref, o_ref):
    t = x_ref.shape[0]
    nch = t // CMP_STRIDE
    x3 = x_ref[...].reshape(nch, CMP_STRIDE, HEAD_DIM)
    w = w_ref[0]
    s0 = jnp.sum(x3 * w[None, :CMP_STRIDE], axis=1)
    s1 = jnp.sum(x3 * w[None, CMP_STRIDE:], axis=1)
    row = lax.broadcasted_iota(jnp.int32, (nch, HEAD_DIM), 0)
    pooled = jnp.where(row < nch - 1, s0 + pltpu.roll(s1, nch - 1, 0), 0.0)
    o_ref[0, 0, 0] = jnp.dot(pooled, p_ref[0], precision=HIGHEST, preferred_element_type=F32)


def _compress_prompt(proj, cmp_pos, cmp_proj, batch, seq):
    nch = seq // CMP_STRIDE
    return pl.pallas_call(
        _cmp_kernel,
        out_shape=jax.ShapeDtypeStruct((batch, 2, NSA_KV_HEADS, nch, HEAD_DIM), F32),
        grid=(batch, 2, NSA_KV_HEADS),
        in_specs=[pl.BlockSpec((seq, LANE), lambda b, e, g: (b, CB_KVB + e * NSA_KV_HEADS + g)),
                  pl.BlockSpec((1, CMP_LEN, HEAD_DIM), lambda b, e, g: (e, 0, 0)),
                  pl.BlockSpec((1, HEAD_DIM, HEAD_DIM), lambda b, e, g: (e, 0, 0))],
        out_specs=pl.BlockSpec((1, 1, 1, nch, HEAD_DIM), lambda b, e, g: (b, e, g, 0, 0)),
        compiler_params=_cp(("parallel", "parallel", "parallel")),
        name="nsa_compress_prompt",
    )(proj, cmp_pos, cmp_proj)


def _masked_softmax(s, mask):
    s = jnp.where(mask, s, NEG)
    p = jnp.exp(s - jnp.max(s, axis=-1, keepdims=True)) * mask.astype(F32)
    return p / jnp.maximum(jnp.sum(p, axis=-1, keepdims=True), 1e-30)


def _select_blocks_t(score_t):
    nj = score_t.shape[0]
    sub = 8
    groups = [score_t[v:v + sub] for v in range(0, nj, sub)]
    j_in = lax.broadcasted_iota(jnp.int32, groups[0].shape, 0)
    cnts = [jnp.zeros(g.shape, jnp.int32) for g in groups]
    for jp in range(nj):
        row = score_t[jp:jp + 1, :]
        for v, g in enumerate(groups):
            if v * sub + sub - 1 <= jp:
                ahead = jnp.where(row > g, 1, 0)
            elif v * sub > jp:
                ahead = jnp.where(row >= g, 1, 0)
            else:
                ahead = jnp.where(j_in + v * sub > jp, jnp.where(row >= g, 1, 0),
                                  jnp.where(row > g, 1, 0))
            cnts[v] = cnts[v] + ahead
    cnt = jnp.concatenate(cnts, axis=0)
    return (cnt < N_SEL) & (score_t > NEG / 2)


def _nsa_kernel(q0_ref, q1_ref, q2_ref, q3_ref, sm_ref, ck_ref, cv_ref, ks_ref, vs_ref, kw_ref,
                vw_ref, o_ref, acc_ref, m_ref, l_ref, kvb_ref, s_ref):
    g = pl.program_id(1)
    qi = pl.program_id(2)
    TQ = q0_ref.shape[0]
    R4 = NSA_HPG * TQ
    T = ks_ref.shape[0]
    KT = min(NSA_KT, T)
    qs = qi * TQ

    @pl.when(qi == 0)
    def _():
        for i, r in enumerate((ks_ref, vs_ref, kw_ref, vw_ref)):
            kvb_ref[i] = r[...].astype(BF16)

    ksb_ref, vsb_ref, kwb_ref, vwb_ref = (kvb_ref.at[i] for i in range(4))
    q4 = jnp.concatenate([q0_ref[...], q1_ref[...], q2_ref[...], q3_ref[...]], axis=0)
    q4 = (q4 * (HEAD_DIM ** -0.5)).astype(BF16)
    t4 = qs + lax.rem(lax.broadcasted_iota(jnp.int32, (R4, 1), 0), TQ)
    t1 = qs + lax.broadcasted_iota(jnp.int32, (TQ, 1), 0)

    ck = ck_ref[0, 0, 0]
    nc = ck.shape[0]
    WK = min(WINDOW + TQ, T)
    w0 = pl.multiple_of(jnp.minimum(jnp.maximum(qs - WINDOW, 0), T - WK), TQ)
    s_cmp = _bdot_nt(q4, ck)
    s_win = _bdot_nt(q4, kwb_ref[pl.ds(w0, WK), :])
    s_ref[...] = _bdot_nt(q4, ksb_ref[pl.ds(0, KT), :])

    cmp_end = CMP_STRIDE * lax.broadcasted_iota(jnp.int32, (1, nc), 1) + (CMP_LEN - 1)
    p = _masked_softmax(s_cmp, cmp_end <= t4)
    o_cmp = _bdot(p, cv_ref[0, 0, 0])

    n_slc = T // SEL_BLOCK
    nsp = max(n_slc, LANE)
    psum = p[0:TQ]
    for j in range(1, NSA_HPG):
        psum = psum + p[j * TQ:(j + 1) * TQ]
    ci = CMP_STRIDE * lax.broadcasted_iota(jnp.int32, (nc, nsp), 0)
    bj = SEL_BLOCK * lax.broadcasted_iota(jnp.int32, (nc, nsp), 1)
    cmap = ((ci < bj + SEL_BLOCK) & (ci + CMP_LEN > bj)).astype(F32)
    imp = jnp.dot(psum, cmap, precision=HIGHEST, preferred_element_type=F32)
    imp_t = imp.T[:n_slc]
    j_idx = lax.broadcasted_iota(jnp.int32, (n_slc, TQ), 0)
    jt = lax.shift_right_logical(qs + lax.broadcasted_iota(jnp.int32, (1, TQ), 1),
                                 int(math.log2(SEL_BLOCK)))
    force = (j_idx == 0) | (j_idx == jt) | (j_idx == jt - 1)
    score_t = jnp.where(j_idx > jt, NEG, jnp.where(force, FORCE, imp_t))
    sel_t = jnp.where(_select_blocks_t(score_t), 1.0, 0.0)

    wpos = w0 + lax.broadcasted_iota(jnp.int32, (1, WK), 1)
    p = _masked_softmax(s_win, (wpos <= t4) & (wpos > t4 - WINDOW))
    o_win = _bdot(p, vwb_ref[pl.ds(w0, WK), :])

    acc_ref[...] = jnp.zeros_like(acc_ref)
    m_ref[...] = jnp.full_like(m_ref, NEG)
    l_ref[...] = jnp.zeros_like(l_ref)
    n_kt = (qs + TQ - 1) // KT + 1

    def slc_body(kt, carry):
        k0 = pl.multiple_of(kt * KT, KT)
        vb = vsb_ref[pl.ds(k0, KT), :]
        s = s_ref[...]
        k1 = pl.multiple_of(jnp.minimum(kt + 1, T // KT - 1) * KT, KT)
        s_next = _bdot_nt(q4, ksb_ref[pl.ds(k1, KT), :])
        kpos = k0 + lax.broadcasted_iota(jnp.int32, (1, KT), 1)
        eb = lax.broadcasted_iota(jnp.int32, (n_slc, KT), 0) == lax.shift_right_logical(
            k0 + lax.broadcasted_iota(jnp.int32, (n_slc, KT), 1), int(math.log2(SEL_BLOCK)))
        selx = _bdot_tn(sel_t, jnp.where(eb, 1.0, 0.0))
        bias = jnp.where((selx > 0.5) & (kpos <= t1), 0.0, NEG)
        s = s + jnp.concatenate([bias] * NSA_HPG, axis=0)
        m_old = m_ref[...]
        m_new = jnp.maximum(m_old, jnp.max(s, axis=-1, keepdims=True))
        pe = jnp.exp(s - m_new)
        alpha = jnp.exp(m_old - m_new)
        l_ref[...] = alpha * l_ref[...] + jnp.sum(pe, axis=-1, keepdims=True)
        acc_ref[...] = alpha * acc_ref[...] + _bdot(pe, vb)
        m_ref[...] = m_new
        s_ref[...] = s_next
        return carry

    lax.fori_loop(0, n_kt, slc_body, 0)
    o_slc = acc_ref[...] / jnp.maximum(l_ref[...], 1e-30)

    gates = _sigmoid(sm_ref[...])
    for j in range(NSA_HPG):
        lane0 = SM_GATE + (g * NSA_HPG + j) * 3
        rows = slice(j * TQ, (j + 1) * TQ)
        ob = (o_cmp[rows] * _lane_pick(gates, lane0) + o_slc[rows] * _lane_pick(gates, lane0 + 1)
              + o_win[rows] * _lane_pick(gates, lane0 + 2))
        o_ref[:, j * HEAD_DIM:(j + 1) * HEAD_DIM] = ob.astype(o_ref.dtype)


def _nsa_prompt(proj, small, comp, batch, seq):
    TQ = min(NSA_TQ, seq)
    nq = seq // TQ
    nch = seq // CMP_STRIDE
    G = NSA_KV_HEADS

    def qspec(j):
        return pl.BlockSpec((TQ, LANE), lambda b, g, i, j=j: (b * nq + i, CB_QB + g * NSA_HPG + j))

    def kvspec(e):
        return pl.BlockSpec((seq, LANE), lambda b, g, i, e=e: (b, CB_KVB + e * G + g))

    def cspec(e):
        return pl.BlockSpec((1, 1, 1, nch, HEAD_DIM), lambda b, g, i, e=e: (b, e, g, 0, 0))

    R4 = NSA_HPG * TQ
    return pl.pallas_call(
        _nsa_kernel,
        out_shape=jax.ShapeDtypeStruct((batch * seq, NSA_HEADS * HEAD_DIM), BF16),
        grid=(batch, G, nq),
        in_specs=[qspec(0), qspec(1), qspec(2), qspec(3),
                  pl.BlockSpec((TQ, LANE), lambda b, g, i: (b * nq + i, 0)),
                  cspec(0), cspec(1), kvspec(2), kvspec(3), kvspec(4), kvspec(5)],
        out_specs=pl.BlockSpec((TQ, NSA_HPG * HEAD_DIM), lambda b, g, i: (b * nq + i, g)),
        scratch_shapes=[pltpu.VMEM((R4, HEAD_DIM), F32), pltpu.VMEM((R4, 1), F32),
                        pltpu.VMEM((R4, 1), F32), pltpu.VMEM((4, seq, HEAD_DIM), BF16),
                        pltpu.VMEM((R4, min(NSA_KT, seq)), F32)],
        compiler_params=_cp(("parallel", "parallel", "arbitrary")),
        name="nsa_prompt",
    )(proj, proj, proj, proj, small, comp, comp, proj, proj, proj, proj)


def _dot(a, b, exact):
    if exact:
        return jnp.dot(a.astype(F32), b.astype(F32), precision=HIGHEST, preferred_element_type=F32)
    return jnp.dot(a.astype(BF16), b.astype(BF16), preferred_element_type=F32)


def _merge_kernel(oa_ref, ob_ref, wa_ref, wb_ref, m0_ref, m1_ref, y_ref, *, exact):
    ya = _dot(oa_ref[...], wa_ref[...], exact)
    yb = _dot(ob_ref[...], wb_ref[...], exact)
    y = _sigmoid(m0_ref[...]) * ya + _sigmoid(m1_ref[...]) * yb
    y_ref[...] = y.astype(y_ref.dtype)


def _merge(o_a, o_b, w_a, w_b, gates, m0_cb, m1_cb, exact):
    n = o_a.shape[0]
    tm = min(1024, n)
    tn = 512
    cpb = tn // LANE
    return pl.pallas_call(
        functools.partial(_merge_kernel, exact=exact),
        out_shape=jax.ShapeDtypeStruct((n, D_MODEL), F32 if exact else BF16),
        grid=(n // tm, D_MODEL // tn),
        in_specs=[pl.BlockSpec((tm, D_MODEL), lambda i, j: (i, 0)),
                  pl.BlockSpec((tm, D_MODEL), lambda i, j: (i, 0)),
                  pl.BlockSpec((D_MODEL, tn), lambda i, j: (0, j)),
                  pl.BlockSpec((D_MODEL, tn), lambda i, j: (0, j)),
                  pl.BlockSpec((tm, tn), lambda i, j: (i, m0_cb // cpb + j)),
                  pl.BlockSpec((tm, tn), lambda i, j: (i, m1_cb // cpb + j))],
        out_specs=pl.BlockSpec((tm, tn), lambda i, j: (i, j)),
        compiler_params=_cp(("parallel", "parallel")),
        name="merge_exact" if exact else "merge",
    )(o_a, o_b, w_a, w_b, gates, gates)


def _top4(logits):
    lane = lax.broadcasted_iota(jnp.int32, logits.shape, 1)
    ids = jnp.full(logits.shape, -1, jnp.int32)
    vals = jnp.full(logits.shape, NEG, F32)
    cur = logits
    for k in range(TOP_K):
        m = jnp.max(cur, axis=1, keepdims=True)
        idx = jnp.min(jnp.where(cur == m, lane, LANE), axis=1, keepdims=True)
        ids = jnp.where(lane == k, idx, ids)
        vals = jnp.where(lane == k, m, vals)
        cur = jnp.where(lane == idx, 2 * NEG, cur)
    e = jnp.where(lane < TOP_K, jnp.exp(vals - jnp.max(vals, axis=1, keepdims=True)), 0.0)
    return ids, e / jnp.sum(e, axis=1, keepdims=True)


def _outproj_kernel(x_ref, y_ref, w_ref, gn_ref, wr_ref, br_ref, h_ref, hn_ref, ids_ref, gate_ref,
                    *, exact):
    h = x_ref[...] + _dot(y_ref[...], w_ref[...], exact)
    h_ref[...] = h
    hn = h * lax.rsqrt(jnp.mean(h * h, axis=-1, keepdims=True) + EPS) * gn_ref[...]
    hn_ref[...] = hn
    logits = jnp.dot(hn, wr_ref[...], precision=HIGHEST, preferred_element_type=F32) + br_ref[...]
    lane = lax.broadcasted_iota(jnp.int32, logits.shape, 1)
    ids, gate = _top4(jnp.where(lane < N_EXPERTS, logits, NEG))
    ids_ref[...] = ids
    gate_ref[...] = gate


def _outproj_router(x, y, w_out, norm_g, w_router, b_router, exact):
    n = x.shape[0]
    tm = min(256, n)
    row = lambda i: (i, 0)
    fixed = lambda i: (0, 0)
    return pl.pallas_call(
        functools.partial(_outproj_kernel, exact=exact),
        out_shape=(jax.ShapeDtypeStruct((n, D_MODEL), F32), jax.ShapeDtypeStruct((n, D_MODEL), F32),
                   jax.ShapeDtypeStruct((n, LANE), jnp.int32), jax.ShapeDtypeStruct((n, LANE), F32)),
        grid=(n // tm,),
        in_specs=[pl.BlockSpec((tm, D_MODEL), row), pl.BlockSpec((tm, D_MODEL), row),
                  pl.BlockSpec((D_MODEL, D_MODEL), fixed), pl.BlockSpec((1, D_MODEL), fixed),
                  pl.BlockSpec((D_MODEL, LANE), fixed), pl.BlockSpec((1, LANE), fixed)],
        out_specs=(pl.BlockSpec((tm, D_MODEL), row), pl.BlockSpec((tm, D_MODEL), row),
                   pl.BlockSpec((tm, LANE), row), pl.BlockSpec((tm, LANE), row)),
        compiler_params=_cp(("parallel",)),
        name="outproj_router_exact" if exact else "outproj_router",
    )(x, y, w_out, norm_g.reshape(1, D_MODEL), w_router, b_router)


def _rank_kernel(ids_ref, rank_ref, cnt_ref, carry_ref):
    i = pl.program_id(0)

    @pl.when(i == 0)
    def _():
        carry_ref[...] = jnp.zeros_like(carry_ref)

    ids = ids_ref[...]
    tm = ids.shape[0]
    lane = lax.broadcasted_iota(jnp.int32, ids.shape, 1)
    cols = [jnp.sum(jnp.where(lane == k, ids, 0), axis=1, keepdims=True) for k in range(TOP_K)]
    mask = jnp.zeros(ids.shape, F32)
    for k in range(TOP_K):
        mask = mask + (lane == cols[k]).astype(F32)
    r = lax.broadcasted_iota(jnp.int32, (tm, tm), 0)
    c = lax.broadcasted_iota(jnp.int32, (tm, tm), 1)
    before = jnp.dot((r > c).astype(BF16), mask.astype(BF16), preferred_element_type=F32)
    pos = before + carry_ref[0:1]
    rank = jnp.zeros(ids.shape, F32)
    for k in range(TOP_K):
        rk = jnp.sum(jnp.where(lane == cols[k], pos, 0.0), axis=1, keepdims=True)
        rank = jnp.where(lane == k, rk, rank)
    rank_ref[...] = rank.astype(jnp.int32)
    total = carry_ref[0:1] + jnp.sum(mask, axis=0, keepdims=True)
    carry_ref[...] = jnp.broadcast_to(total, carry_ref.shape)
    cnt_ref[...] = jnp.broadcast_to(total, cnt_ref.shape).astype(jnp.int32)


def _expert_ranks(ids):
    n = ids.shape[0]
    tm = min(256, n)
    return pl.pallas_call(
        _rank_kernel,
        out_shape=(jax.ShapeDtypeStruct((n, LANE), jnp.int32), jax.ShapeDtypeStruct((8, LANE), jnp.int32)),
        grid=(n // tm,),
        in_specs=[pl.BlockSpec((tm, LANE), lambda i: (i, 0))],
        out_specs=(pl.BlockSpec((tm, LANE), lambda i: (i, 0)), pl.BlockSpec((8, LANE), lambda i: (0, 0))),
        scratch_shapes=[pltpu.VMEM((8, LANE), F32)],
        compiler_params=_cp(("arbitrary",)),
        name="expert_ranks",
    )(ids)


def _scatter_kernel(slot_ref, x_ref, xs_in_ref, xs_ref, sem):
    del xs_in_ref
    tm = x_ref.shape[0]

    def copy(r, k):
        return pltpu.make_async_copy(x_ref.at[pl.ds(r, 1)],
                                     xs_ref.at[pl.ds(slot_ref[0, 0, r * TOP_K + k], 1)], sem)

    def start(r, c):
        for k in range(TOP_K):
            copy(r, k).start(priority=k % 2)
        return c

    def wait(r, c):
        for k in range(TOP_K):
            copy(r, k).wait()
        return c

    lax.fori_loop(0, tm, start, 0)
    lax.fori_loop(0, tm, wait, 0)


def _scatter_rows(x, slots, xs):
    n = x.shape[0]
    tm = min(256, n)
    nt = n // tm
    return pl.pallas_call(
        _scatter_kernel,
        out_shape=jax.ShapeDtypeStruct(xs.shape, xs.dtype),
        grid=(nt,),
        in_specs=[pl.BlockSpec((1, 1, tm * TOP_K), lambda i: (i, 0, 0), memory_space=pltpu.SMEM),
                  pl.BlockSpec((tm, D_MODEL), lambda i: (i, 0)),
                  pl.BlockSpec(memory_space=pl.ANY)],
        out_specs=pl.BlockSpec(memory_space=pl.ANY),
        scratch_shapes=[pltpu.SemaphoreType.DMA(())],
        input_output_aliases={2: 0},
        compiler_params=_cp(("arbitrary",)),
        name="moe_scatter",
    )(slots.reshape(nt, 1, tm * TOP_K), x, xs)


def _stream_expert_weights(be_ref, nu_ref, ne_ref, nj_ref, cnt_ref, tile_copies, consume):
    j = pl.program_id(0)
    b = pl.program_id(1)
    step = j * pl.num_programs(1) + b
    used = b < nu_ref[0]
    e = be_ref[b]
    first = used & ((b == 0) | (e != be_ref[jnp.maximum(b - 1, 0)]))

    @pl.when(step == 0)
    def _():
        cnt_ref[0] = 0
        for c in tile_copies(e, j, 0):
            c.start()

    @pl.when(first)
    def _():
        slot = lax.rem(cnt_ref[0], 2)
        for c in tile_copies(e, j, slot):
            c.wait()
        consume(slot)
        ne = ne_ref[step]

        @pl.when(ne >= 0)
        def _():
            for c in tile_copies(ne, nj_ref[step], 1 - slot):
                c.start()

        cnt_ref[0] = cnt_ref[0] + 1

    return used


def _next_tiles(blk_e, n_used, n_col_tiles):
    nb = blk_e.shape[0]
    used = jnp.arange(nb) < n_used[0]
    prev = jnp.concatenate([jnp.full((1,), -1, blk_e.dtype), blk_e[:-1]])
    first = jnp.tile(used & (blk_e != prev), n_col_tiles)
    n = n_col_tiles * nb
    key = jnp.where(first, jnp.arange(n), n)
    nxt = jnp.concatenate([lax.cummin(key, axis=0, reverse=True)[1:], jnp.full((1,), n, key.dtype)])
    ok = nxt < n
    nxt = jnp.minimum(nxt, n - 1)
    return (jnp.where(ok, blk_e[nxt % nb], -1).astype(jnp.int32),
            jnp.where(ok, nxt // nb, 0).astype(jnp.int32))


def _moe_up_kernel(be_ref, nu_ref, ne_ref, nj_ref, x_ref, w_hbm, bg_ref, bu_ref, o_ref,
                   wbuf_ref, wgb_ref, wub_ref, cnt_ref, sem):
    tf = wgb_ref.shape[1]
    nf = pl.num_programs(0)

    def tile_copies(e, j, slot):
        return [pltpu.make_async_copy(
            w_hbm.at[e, :, pl.ds(pl.multiple_of((p * nf + j) * tf, tf), tf)],
            wbuf_ref.at[slot, p], sem.at[slot]) for p in range(2)]

    def consume(slot):
        wgb_ref[...] = wbuf_ref[slot, 0].astype(BF16)
        wub_ref[...] = wbuf_ref[slot, 1].astype(BF16)

    used = _stream_expert_weights(be_ref, nu_ref, ne_ref, nj_ref, cnt_ref, tile_copies, consume)

    @pl.when(used)
    def _():
        x = x_ref[...].astype(BF16)
        g = jnp.dot(x, wgb_ref[...], preferred_element_type=F32) + bg_ref[0]
        u = jnp.dot(x, wub_ref[...], preferred_element_type=F32) + bu_ref[0]
        g = jnp.minimum(g, SWIGLU_LIMIT)
        u = jnp.clip(u, -SWIGLU_LIMIT, SWIGLU_LIMIT)
        o_ref[...] = ((u + 1.0) * g * _sigmoid(SWIGLU_ALPHA * g)).astype(o_ref.dtype)

    @pl.when(jnp.logical_not(used))
    def _():
        o_ref[...] = jnp.zeros_like(o_ref)


def _moe_up(xs, blk_e, n_used, w_gu, b_gu, tf=1024):
    n_blocks = xs.shape[0] // MOE_BLK
    nf = D_FF // tf
    next_e, next_j = _next_tiles(blk_e, n_used, nf)
    grid_spec = pltpu.PrefetchScalarGridSpec(
        num_scalar_prefetch=4,
        grid=(nf, n_blocks),
        in_specs=[pl.BlockSpec((MOE_BLK, D_MODEL), lambda j, b, be, *_: (b, 0)),
                  pl.BlockSpec(memory_space=pl.ANY),
                  pl.BlockSpec((1, 1, tf), lambda j, b, be, *_: (be[b], 0, j)),
                  pl.BlockSpec((1, 1, tf), lambda j, b, be, *_: (be[b], 0, nf + j))],
        out_specs=pl.BlockSpec((MOE_BLK, tf), lambda j, b, be, *_: (b, j)),
        scratch_shapes=[pltpu.VMEM((2, 2, D_MODEL, tf), F32), pltpu.VMEM((D_MODEL, tf), BF16),
                        pltpu.VMEM((D_MODEL, tf), BF16), pltpu.SMEM((1,), jnp.int32),
                        pltpu.SemaphoreType.DMA((2,))])
    return pl.pallas_call(
        _moe_up_kernel,
        out_shape=jax.ShapeDtypeStruct((xs.shape[0], D_FF), BF16),
        grid_spec=grid_spec,
        compiler_params=_cp(("arbitrary", "arbitrary")),
        name="moe_up",
    )(blk_e, n_used, next_e, next_j, xs, w_gu, b_gu.reshape(N_EXPERTS, 1, 2 * D_FF),
      b_gu.reshape(N_EXPERTS, 1, 2 * D_FF))


def _moe_down_kernel(be_ref, nu_ref, ne_ref, nj_ref, h_ref, w_hbm, b_ref, o_ref,
                     wbuf_ref, wb_ref, cnt_ref, sem):
    tn = wb_ref.shape[1]

    def tile_copies(e, j, slot):
        return [pltpu.make_async_copy(w_hbm.at[e, :, pl.ds(pl.multiple_of(j * tn, tn), tn)],
                                      wbuf_ref.at[slot], sem.at[slot])]

    def consume(slot):
        wb_ref[...] = wbuf_ref[slot].astype(BF16)

    used = _stream_expert_weights(be_ref, nu_ref, ne_ref, nj_ref, cnt_ref, tile_copies, consume)

    @pl.when(used)
    def _():
        o_ref[...] = jnp.dot(h_ref[...], wb_ref[...], preferred_element_type=F32) + b_ref[0]

    @pl.when(jnp.logical_not(used))
    def _():
        o_ref[...] = jnp.zeros_like(o_ref)


def _moe_down(hid, blk_e, n_used, w_down, b_down, tn=D_MODEL):
    n_blocks = hid.shape[0] // MOE_BLK
    nn = D_MODEL // tn
    next_e, next_j = _next_tiles(blk_e, n_used, nn)
    grid_spec = pltpu.PrefetchScalarGridSpec(
        num_scalar_prefetch=4,
        grid=(nn, n_blocks),
        in_specs=[pl.BlockSpec((MOE_BLK, D_FF), lambda j, b, be, *_: (b, 0)),
                  pl.BlockSpec(memory_space=pl.ANY),
                  pl.BlockSpec((1, 1, tn), lambda j, b, be, *_: (be[b], 0, j))],
        out_specs=pl.BlockSpec((MOE_BLK, tn), lambda j, b, be, *_: (b, j)),
        scratch_shapes=[pltpu.VMEM((2, D_FF, tn), F32), pltpu.VMEM((D_FF, tn), BF16),
                        pltpu.SMEM((1,), jnp.int32), pltpu.SemaphoreType.DMA((2,))])
    return pl.pallas_call(
        _moe_down_kernel,
        out_shape=jax.ShapeDtypeStruct((hid.shape[0], D_MODEL), F32),
        grid_spec=grid_spec,
        compiler_params=_cp(("arbitrary", "arbitrary")),
        name="moe_down",
    )(blk_e, n_used, next_e, next_j, hid, w_down, b_down.reshape(N_EXPERTS, 1, D_MODEL))


def _combine_kernel(slot_ref, h_ref, gate_ref, gn_ref, y_ref, o_ref, buf_ref, sem):
    tm = h_ref.shape[0]

    def copy(r, k):
        return pltpu.make_async_copy(y_ref.at[pl.ds(slot_ref[0, 0, r * TOP_K + k], 1)],
                                     buf_ref.at[k, pl.ds(r, 1)], sem)

    def start(r, c):
        for k in range(TOP_K):
            copy(r, k).start(priority=k % 2)
        return c

    def wait(r, c):
        for k in range(TOP_K):
            copy(r, k).wait()
        return c

    lax.fori_loop(0, tm, start, 0)
    lax.fori_loop(0, tm, wait, 0)
    gate = gate_ref[...]
    acc = h_ref[...]
    for k in range(TOP_K):
        acc = acc + gate[:, k:k + 1] * buf_ref[k]
    o_ref[...] = acc * lax.rsqrt(jnp.mean(acc * acc, axis=-1, keepdims=True) + EPS) * gn_ref[...]


def _combine(h, gate, slots, y, norm_g):
    n = h.shape[0]
    tm = min(256, n)
    nt = n // tm
    return pl.pallas_call(
        _combine_kernel,
        out_shape=jax.ShapeDtypeStruct((n, D_MODEL), F32),
        grid=(nt,),
        in_specs=[pl.BlockSpec((1, 1, tm * TOP_K), lambda i: (i, 0, 0), memory_space=pltpu.SMEM),
                  pl.BlockSpec((tm, D_MODEL), lambda i: (i, 0)),
                  pl.BlockSpec((tm, LANE), lambda i: (i, 0)),
                  pl.BlockSpec((1, D_MODEL), lambda i: (0, 0)),
                  pl.BlockSpec(memory_space=pl.ANY)],
        out_specs=pl.BlockSpec((tm, D_MODEL), lambda i: (i, 0)),
        scratch_shapes=[pltpu.VMEM((TOP_K, tm, D_MODEL), F32), pltpu.SemaphoreType.DMA(())],
        compiler_params=_cp(("arbitrary",)),
        name="moe_combine",
    )(slots.reshape(nt, 1, tm * TOP_K), h, gate, norm_g.reshape(1, D_MODEL), y)


def _moe_plan(ids_p, ids_s):
    n_p, n_s = ids_p.shape[0], ids_s.shape[0]
    pad = (-n_s) % 256
    ids = jnp.concatenate([ids_p, ids_s, jnp.full((pad, LANE), -1, jnp.int32)], axis=0)
    rank, cnt = _expert_ranks(ids)
    counts = cnt[0, :N_EXPERTS]
    padded = (counts + MOE_BLK - 1) // MOE_BLK * MOE_BLK
    pad_end = jnp.cumsum(padded)
    pad_start = pad_end - padded
    n_pairs = (n_p + n_s) * TOP_K
    n_blocks = -(-n_pairs // MOE_BLK) + N_EXPERTS
    blk_start = jnp.arange(n_blocks, dtype=jnp.int32) * MOE_BLK
    blk_e = jnp.minimum(jnp.sum((pad_end[None, :] <= blk_start[:, None]).astype(jnp.int32), axis=1),
                        N_EXPERTS - 1)
    n_used = (pad_end[-1:] // MOE_BLK).astype(jnp.int32)
    e = ids[:n_p + n_s, :TOP_K]
    slot = (pad_start[e] + rank[:n_p + n_s, :TOP_K]).astype(jnp.int32)
    return slot[:n_p], slot[n_p:], blk_e, n_used, n_blocks


def _moe(hn_p, hn_s, h_p, h_s, ids_p, gate_p, ids_s, gate_s, w_gu, b_gu, w_down, b_down, norm_final):
    slot_p, slot_s, blk_e, n_used, n_blocks = _moe_plan(ids_p, ids_s)
    xs = jnp.zeros((n_blocks * MOE_BLK, D_MODEL), F32)
    xs = _scatter_rows(hn_p, slot_p, xs)
    xs = _scatter_rows(hn_s, slot_s, xs)
    hid = _moe_up(xs, blk_e, n_used, w_gu, b_gu)
    y = _moe_down(hid, blk_e, n_used, w_down, b_down)
    return (_combine(h_p, gate_p, slot_p, y, norm_final),
            _combine(h_s, gate_s, slot_s, y, norm_final))


def _proj_sample_kernel(x_ref, g_ref, w_ref, o_ref):
    x = x_ref[...]
    xn = x * lax.rsqrt(jnp.mean(x * x, axis=-1, keepdims=True) + EPS) * g_ref[...]
    o_ref[...] = lax.dot_general(xn, w_ref[...], (((1,), (1,)), ((), ())), precision=HIGHEST,
                                 preferred_element_type=F32)


def _proj_sample(x, norm_g, w_t):
    n = x.shape[0]
    tn = 1024
    return pl.pallas_call(
        _proj_sample_kernel,
        out_shape=jax.ShapeDtypeStruct((n, N_IN), F32),
        grid=(pl.cdiv(N_IN, tn),),
        in_specs=[pl.BlockSpec((n, D_MODEL), lambda j: (0, 0)),
                  pl.BlockSpec((1, D_MODEL), lambda j: (0, 0)),
                  pl.BlockSpec((tn, D_MODEL), lambda j: (j, 0))],
        out_specs=pl.BlockSpec((n, tn), lambda j: (0, j)),
        compiler_params=_cp(("parallel",)),
        name="proj_sample",
    )(x, norm_g.reshape(1, D_MODEL), w_t)


def _gdn_sample_kernel(ext_ref, cw_ref, z_ref, br_ref, ar_ref, al_ref, dt_ref, ng_ref, s_ref,
                       o_ref, so_ref):
    H = GDN_HEADS
    y = ext_ref[0, 0] * cw_ref[0]
    for w in range(1, CONV_W):
        y = y + ext_ref[0, w] * cw_ref[w]
    y = _silu(y)
    q = _l2norm(y[0:H]) * (GDN_DK ** -0.5)
    k = _l2norm(y[H:2 * H])
    v = y[2 * H:3 * H]
    beta = _sigmoid(br_ref[0])
    eg = jnp.exp(-jnp.exp(al_ref[...]) * _softplus(ar_ref[0] + dt_ref[...]))
    qk = jnp.sum(q * k, axis=-1, keepdims=True)
    qT = q.T
    kT = k.T
    rows = []
    for h in range(H):
        S = s_ref[0, h]
        kc = kT[:, h:h + 1]
        qc = qT[:, h:h + 1]
        e = eg[h:h + 1]
        k_s = jnp.sum(S * kc, axis=0, keepdims=True)
        q_s = jnp.sum(S * qc, axis=0, keepdims=True)
        v_new = beta[h:h + 1] * (v[h:h + 1] - e * k_s)
        rows.append(e * q_s + qk[h:h + 1] * v_new)
        so_ref[0, h] = e * S + kc * v_new
    o = jnp.concatenate(rows, axis=0)
    on = o * lax.rsqrt(jnp.mean(o * o, axis=-1, keepdims=True) + EPS) * ng_ref[...]
    o_ref[0] = on * _silu(z_ref[0])


def _gdn_sample(ext, conv_w, z, b_raw, a_raw, a_log, dt_bias, norm_g, state):
    bsz = ext.shape[0]
    H = GDN_HEADS
    col = lambda a: a.reshape(bsz, H, 1)
    hcol = lambda a: a.astype(F32).reshape(H, 1)
    return pl.pallas_call(
        _gdn_sample_kernel,
        out_shape=(jax.ShapeDtypeStruct((bsz, H, GDN_DV), F32),
                   jax.ShapeDtypeStruct(state.shape, F32)),
        grid=(bsz,),
        in_specs=[pl.BlockSpec((1, CONV_W, 3 * H, LANE), lambda b: (b, 0, 0, 0)),
                  pl.BlockSpec((CONV_W, 3 * H, LANE), lambda b: (0, 0, 0)),
                  pl.BlockSpec((1, H, GDN_DV), lambda b: (b, 0, 0)),
                  pl.BlockSpec((1, H, 1), lambda b: (b, 0, 0)),
                  pl.BlockSpec((1, H, 1), lambda b: (b, 0, 0)),
                  pl.BlockSpec((H, 1), lambda b: (0, 0)),
                  pl.BlockSpec((H, 1), lambda b: (0, 0)),
                  pl.BlockSpec((1, GDN_DV), lambda b: (0, 0)),
                  pl.BlockSpec((1, H, GDN_DK, GDN_DV), lambda b: (b, 0, 0, 0))],
        out_specs=(pl.BlockSpec((1, H, GDN_DV), lambda b: (b, 0, 0)),
                   pl.BlockSpec((1, H, GDN_DK, GDN_DV), lambda b: (b, 0, 0, 0))),
        compiler_params=_cp(("parallel",)),
        name="gdn_sample",
    )(ext, conv_w.reshape(CONV_W, 3 * H, LANE), z.reshape(bsz, H, GDN_DV), col(b_raw), col(a_raw),
      hcol(a_log), hcol(dt_bias), norm_g.reshape(1, GDN_DV), state)


PAGES_PER_STEP = 8


def _page_sums_kernel(pt_ref, *refs):
    del pt_ref
    x_refs, w_ref, o_ref = refs[:-2], refs[-2], refs[-1]
    nch = PAGE_SIZE // CMP_STRIDE
    G = NSA_KV_HEADS
    for i, x_ref in enumerate(x_refs):
        for e in range(2):
            x4 = x_ref[0, 0, :, e].reshape(nch, CMP_STRIDE, G, HEAD_DIM)
            for half in range(2):
                w = w_ref[2 * e + half][None, :, None, :]
                o_ref[0, 2 * e + half, i * nch:(i + 1) * nch] = jnp.sum(x4 * w, axis=1)


def _page_sums(kv_cache, layer, page_table, w4):
    bsz, n_pages = page_table.shape
    G = NSA_KV_HEADS
    nch = PAGE_SIZE // CMP_STRIDE
    pps = math.gcd(PAGES_PER_STEP, n_pages)

    def page(i):
        return pl.BlockSpec((1, 1, PAGE_SIZE, 2, G, HEAD_DIM),
                            lambda b, p, pt: (layer, pt[b * n_pages + p * pps + i], 0, 0, 0, 0))

    grid_spec = pltpu.PrefetchScalarGridSpec(
        num_scalar_prefetch=1,
        grid=(bsz, n_pages // pps),
        in_specs=[page(i) for i in range(pps)]
        + [pl.BlockSpec((4, CMP_STRIDE, HEAD_DIM), lambda b, p, pt: (0, 0, 0))],
        out_specs=pl.BlockSpec((1, 4, pps * nch, G, HEAD_DIM), lambda b, p, pt: (b, 0, p, 0, 0)))
    return pl.pallas_call(
        _page_sums_kernel,
        out_shape=jax.ShapeDtypeStruct((bsz, 4, n_pages * nch, G, HEAD_DIM), F32),
        grid_spec=grid_spec,
        compiler_params=_cp(("parallel", "arbitrary")),
        name="nsa_page_sums",
    )(page_table.reshape(-1), *([kv_cache] * pps), w4)


def _row_to_col(r):
    n = r.shape[1]
    eye = lax.broadcasted_iota(jnp.int32, (n, n), 0) == lax.broadcasted_iota(jnp.int32, (n, n), 1)
    return jnp.sum(jnp.where(eye, jnp.broadcast_to(r, (n, n)), 0), axis=1, keepdims=True)


def _hdot(a, b):
    return jnp.dot(a, b, precision=HIGHEST, preferred_element_type=F32)


def _hdot_nt(a, b):
    return lax.dot_general(a, b, (((1,), (1,)), ((), ())), precision=HIGHEST,
                           preferred_element_type=F32)


def _nsa_sample_select_kernel(sums_ref, new_ref, w_ref, proj_ref, q_ref, ocmp_ref, idx_ref, *, past):
    G = NSA_KV_HEADS
    nch = sums_ref.shape[2]
    row = lax.broadcasted_iota(jnp.int32, (nch, HEAD_DIM), 0)

    def pooled(e, g):
        new_s1 = new_ref[0, e, g:g + 1, :] * w_ref[2 * e + 1, 0:1]
        s1 = sums_ref[0, 2 * e + 1, :, g, :]
        nxt = jnp.where(row < nch - 1, pltpu.roll(s1, nch - 1, 0), new_s1)
        return sums_ref[0, 2 * e, :, g, :] + nxt

    n_slc = past // SEL_BLOCK + 1
    nsp = -(-n_slc // LANE) * LANE
    t = past
    cmp_end = CMP_STRIDE * lax.broadcasted_iota(jnp.int32, (1, nch), 1) + (CMP_LEN - 1)
    ci = CMP_STRIDE * lax.broadcasted_iota(jnp.int32, (nch, nsp), 0)
    bj = SEL_BLOCK * lax.broadcasted_iota(jnp.int32, (nch, nsp), 1)
    cmap = ((ci < bj + SEL_BLOCK) & (ci + CMP_LEN > bj)).astype(F32)
    j_idx = lax.broadcasted_iota(jnp.int32, (1, nsp), 1)
    jt = t // SEL_BLOCK
    force = (j_idx == 0) | (j_idx == jt) | (j_idx == jt - 1)
    i_idx = lax.broadcasted_iota(jnp.int32, (nsp, nsp), 0)
    jj_idx = lax.broadcasted_iota(jnp.int32, (nsp, nsp), 1)
    lane = lax.broadcasted_iota(jnp.int32, (1, LANE), 1)
    idx_rows = []
    for g in range(G):
        q = q_ref[0, g] * (HEAD_DIM ** -0.5)
        p = _masked_softmax(_hdot_nt(_hdot_nt(q, proj_ref[0]), pooled(0, g)), cmp_end <= t)
        ocmp_ref[0, g] = _hdot(_hdot(p, pooled(1, g)), proj_ref[1])
        psum = jnp.sum(p[0:NSA_HPG], axis=0, keepdims=True)
        imp = _hdot(psum, cmap)
        score = jnp.where(j_idx > jt, NEG, jnp.where(force, FORCE, imp))
        sc = _row_to_col(score)
        ahead = (sc > score) | ((sc == score) & (i_idx < jj_idx))
        cnt = jnp.sum(ahead.astype(jnp.int32), axis=0, keepdims=True)
        sel = (cnt < N_SEL) & (score > NEG / 2)
        selc = _row_to_col(sel.astype(jnp.int32))
        before = jnp.sum(jnp.where(i_idx < jj_idx, selc, 0), axis=0, keepdims=True)
        out = jnp.full((1, LANE), -1, jnp.int32)
        for k in range(N_SEL):
            hit = sel & (before == k)
            jk = jnp.sum(jnp.where(hit, j_idx + 1, 0), axis=1, keepdims=True) - 1
            out = jnp.where(lane == k, jk, out)
        idx_rows.append(out)
    idx_ref[0] = jnp.concatenate(idx_rows + [jnp.full((8 - G, LANE), -1, jnp.int32)], axis=0)


def _nsa_sample_select(sums, new6, w4, cmp_proj, q8, past):
    bsz = sums.shape[0]
    G = NSA_KV_HEADS
    return pl.pallas_call(
        functools.partial(_nsa_sample_select_kernel, past=past),
        out_shape=(jax.ShapeDtypeStruct((bsz, G, 8, HEAD_DIM), F32),
                   jax.ShapeDtypeStruct((bsz, 8, LANE), jnp.int32)),
        grid=(bsz,),
        in_specs=[pl.BlockSpec((1,) + sums.shape[1:], lambda b: (b, 0, 0, 0, 0)),
                  pl.BlockSpec((1, 6, G, HEAD_DIM), lambda b: (b, 0, 0, 0)),
                  pl.BlockSpec((4, CMP_STRIDE, HEAD_DIM), lambda b: (0, 0, 0)),
                  pl.BlockSpec((2, HEAD_DIM, HEAD_DIM), lambda b: (0, 0, 0)),
                  pl.BlockSpec((1, G, 8, HEAD_DIM), lambda b: (b, 0, 0, 0))],
        out_specs=(pl.BlockSpec((1, G, 8, HEAD_DIM), lambda b: (b, 0, 0, 0)),
                   pl.BlockSpec((1, 8, LANE), lambda b: (b, 0, 0))),
        compiler_params=_cp(("parallel",)),
        name="nsa_sample_select",
    )(sums, new6, w4, cmp_proj, q8)


def _nsa_sample_attend_kernel(idx_ref, pt_ref, *refs, past, n_pages):
    del pt_ref
    kvb = refs[0:N_SEL]
    new_ref, win_ref, q_ref, ocmp_ref, gate_ref, o_ref = refs[N_SEL:]
    b = pl.program_id(0)
    g = pl.program_id(1)
    G = NSA_KV_HEADS
    t = past
    n_past_blk = past // SEL_BLOCK
    q = q_ref[0, 0] * (HEAD_DIM ** -0.5)
    k_new = new_ref[0, 2, pl.ds(g, 1), :]
    v_new = new_ref[0, 3, pl.ds(g, 1), :]
    kw_new = new_ref[0, 4, pl.ds(g, 1), :]
    vw_new = new_ref[0, 5, pl.ds(g, 1), :]

    r64 = lax.broadcasted_iota(jnp.int32, (SEL_BLOCK, HEAD_DIM), 0)
    ks, vs, kpos = [], [], []
    r_lane = lax.broadcasted_iota(jnp.int32, (1, SEL_BLOCK), 1)
    for k in range(N_SEL):
        j = idx_ref[(b * G + g) * N_SEL + k]
        is_new = j >= n_past_blk
        ks.append(jnp.where(is_new, jnp.where(r64 == 0, k_new, 0.0), kvb[k][0, 0, :, 0, g, :]))
        vs.append(jnp.where(is_new, jnp.where(r64 == 0, v_new, 0.0), kvb[k][0, 0, :, 1, g, :]))
        kpos.append(jnp.where(j >= 0, j * SEL_BLOCK + r_lane, t + 1))
    ksel = jnp.concatenate(ks, axis=0)
    vsel = jnp.concatenate(vs, axis=0)
    kpos = jnp.concatenate(kpos, axis=1)
    p = _masked_softmax(_hdot_nt(q, ksel), kpos <= t)
    o_slc = _hdot(p, vsel)

    L = win_ref.shape[2]
    wpos = (past - L) + lax.broadcasted_iota(jnp.int32, (1, L), 1)
    ok = (wpos > t - WINDOW) & (wpos >= 0)
    s_w = jnp.where(ok, _hdot_nt(q, win_ref[0, 0, :, 0, g, :]), NEG)
    s_n = jnp.sum(q * kw_new, axis=-1, keepdims=True)
    m = jnp.maximum(jnp.max(s_w, axis=-1, keepdims=True), s_n)
    p_w = jnp.exp(s_w - m) * ok.astype(F32)
    p_n = jnp.exp(s_n - m)
    den = jnp.maximum(jnp.sum(p_w, axis=-1, keepdims=True) + p_n, 1e-30)
    o_win = (_hdot(p_w, win_ref[0, 0, :, 1, g, :]) + p_n * vw_new) / den

    gb = _sigmoid(gate_ref[0, 0])
    o_ref[0, 0] = ocmp_ref[0, 0] * gb[:, 0:1] + o_slc * gb[:, 1:2] + o_win * gb[:, 2:3]


def _nsa_sample_attend(idx, page_table, kv_cache, win_cache, layer, new6, q8, o_cmp, gate8, past):
    bsz, n_pages = page_table.shape
    G = NSA_KV_HEADS
    n_past_blk = past // SEL_BLOCK
    bpp = PAGE_SIZE // SEL_BLOCK
    L = win_cache.shape[2]

    def blk(k):
        def imap(b, g, idx_ref, pt_ref):
            j = jnp.clip(idx_ref[(b * G + g) * N_SEL + k], 0, n_past_blk - 1)
            return (layer, pt_ref[b * n_pages + j // bpp], j % bpp, 1, 0, 0)
        return pl.BlockSpec((1, 1, SEL_BLOCK, 2, G, HEAD_DIM), imap)

    per_bg = pl.BlockSpec((1, 1, 8, HEAD_DIM), lambda b, g, i, p: (b, g, 0, 0))
    grid_spec = pltpu.PrefetchScalarGridSpec(
        num_scalar_prefetch=2,
        grid=(bsz, G),
        in_specs=([blk(k) for k in range(N_SEL)]
                  + [pl.BlockSpec((1, 6, G, HEAD_DIM), lambda b, g, i, p: (b, 0, 0, 0)),
                     pl.BlockSpec((1, 1, L, 2, G, HEAD_DIM), lambda b, g, i, p: (layer, b, 0, 0, 0, 0)),
                     per_bg, per_bg, per_bg]),
        out_specs=per_bg)
    return pl.pallas_call(
        functools.partial(_nsa_sample_attend_kernel, past=past, n_pages=n_pages),
        out_shape=jax.ShapeDtypeStruct((bsz, G, 8, HEAD_DIM), F32),
        grid_spec=grid_spec,
        compiler_params=_cp(("parallel", "arbitrary")),
        name="nsa_sample_attend",
    )(idx, page_table.reshape(-1), *([kv_cache] * N_SEL), new6, win_cache, q8, o_cmp, gate8)


def _prompt_mixers(x2d, batch, seq, norm_mix, w_in, conv_w, a_log, dt_bias, gdn_norm, cmp_pos,
                   cmp_proj, w_out_a, w_out_b):
    w_main, w_small = _repack_w_in(w_in)
    proj, small = _proj_prompt(x2d, norm_mix, w_main, w_small)
    o_a, s_new = _gdn_prompt(proj, small, conv_w, _head_params(a_log, dt_bias), gdn_norm, batch, seq)
    comp = _compress_prompt(proj, cmp_pos, cmp_proj, batch, seq)
    o_b = _nsa_prompt(proj, small, comp, batch, seq)
    y = _merge(o_a, o_b, w_out_a.astype(BF16), w_out_b.astype(BF16), proj, CB_MERGE,
               CB_MERGE + D_MODEL // LANE, False)
    G = NSA_KV_HEADS
    kv6 = proj[:, CB_KVB * LANE:].reshape(batch, seq, 6, G, HEAD_DIM)
    conv_new = proj.reshape(batch, seq, N_MAIN)[:, seq - (CONV_W - 1):, :CONV_DIM]
    return y, kv6[:, :, :4], kv6[:, max(seq - WINDOW, 0):, 4:], s_new, conv_new


def _sample_mixers(x2d, norm_mix, w_in, conv_w, a_log, dt_bias, gdn_norm, cmp_pos, cmp_proj,
                   w_out_a, w_out_b, kv_cache, win_cache, layer, state, conv_buf, page_table):
    bsz = x2d.shape[0]
    win_buf = win_cache[layer]
    G = NSA_KV_HEADS
    past = page_table.shape[1] * PAGE_SIZE
    ps = _proj_sample(x2d, norm_mix, w_in)
    qkv, z = ps[:, OFF_QKV:OFF_Z], ps[:, OFF_Z:OFF_BRAW]
    b_raw, a_raw = ps[:, OFF_BRAW:OFF_ARAW], ps[:, OFF_ARAW:OFF_QB]
    q_b, kv_b = ps[:, OFF_QB:OFF_KVB], ps[:, OFF_KVB:OFF_GATEB]
    gate_b, merge_raw = ps[:, OFF_GATEB:OFF_MERGE], ps[:, OFF_MERGE:]

    ext = jnp.concatenate([conv_buf.astype(F32), qkv[:, None]], axis=1)
    o_a, s_new = _gdn_sample(ext.reshape(bsz, CONV_W, 3 * GDN_HEADS, LANE), conv_w, z, b_raw, a_raw,
                             a_log, dt_bias, gdn_norm, state.astype(F32))

    w4 = cmp_pos.astype(F32).reshape(4, CMP_STRIDE, HEAD_DIM)
    sums = _page_sums(kv_cache, layer, page_table, w4)
    new6 = kv_b.reshape(bsz, 6, G, HEAD_DIM)
    q8 = jnp.pad(q_b.reshape(bsz, G, NSA_HPG, HEAD_DIM), ((0, 0), (0, 0), (0, 8 - NSA_HPG), (0, 0)))
    o_cmp, idx8 = _nsa_sample_select(sums, new6, w4, cmp_proj.astype(F32), q8, past)
    gate8 = jnp.pad(gate_b.reshape(bsz, G, NSA_HPG, 3),
                    ((0, 0), (0, 0), (0, 8 - NSA_HPG), (0, LANE - 3)))
    o_b8 = _nsa_sample_attend(idx8[:, :G, :N_SEL].reshape(-1), page_table, kv_cache, win_cache,
                              layer, new6, q8, o_cmp, gate8, past)
    o_b = o_b8[:, :, :NSA_HPG].reshape(bsz, NSA_HEADS * HEAD_DIM)

    y = _merge(o_a.reshape(bsz, GDN_HEADS * GDN_DV), o_b, w_out_a, w_out_b, merge_raw, 0,
               D_MODEL // LANE, True)
    kv6 = kv_b.reshape(bsz, 1, 6, G, HEAD_DIM)
    win_ext = jnp.concatenate([win_buf.astype(F32), kv6[:, :, 4:]], axis=1)
    keep = min(WINDOW, past + 1)
    return y, kv6[:, :, :4], win_ext[:, win_ext.shape[1] - keep:], s_new, ext[:, 1:]


def kernel(x_prompt, x_sample, cache_nsa_kv, cache_nsa_win, state_gdn, state_conv, page_table, norm_mix, w_in, conv_w, gdn_a_log, gdn_dt_bias, gdn_norm, w_out_a, cmp_pos, cmp_proj, w_out_b, w_out, norm_ffn, w_router, b_router, w_gu, b_gu, w_down, b_down, norm_final):
    assert w_in.shape[0] == 1 and x_sample.shape[1] == 1, "one layer, one new token per sequence"
    bp, seq, _ = x_prompt.shape
    bs = x_sample.shape[0]
    xp = x_prompt.reshape(bp * seq, D_MODEL)
    xs = x_sample.reshape(bs, D_MODEL)
    l = 0
    w_t = jnp.swapaxes(w_in[l], 0, 1)
    y_p, kv_p, win_p, gdn_p, conv_p = _prompt_mixers(
        xp, bp, seq, norm_mix[l], w_t, conv_w[l], gdn_a_log[l], gdn_dt_bias[l], gdn_norm[l],
        cmp_pos[l], cmp_proj[l], w_out_a[l], w_out_b[l])
    y_s, kv_s, win_s, gdn_s, conv_s = _sample_mixers(
        xs, norm_mix[l], w_t, conv_w[l], gdn_a_log[l], gdn_dt_bias[l], gdn_norm[l], cmp_pos[l],
        cmp_proj[l], w_out_a[l], w_out_b[l], cache_nsa_kv, cache_nsa_win, l, state_gdn[l],
        state_conv[l], page_table)

    wr = jnp.pad(w_router[l].astype(F32), ((0, 0), (0, LANE - N_EXPERTS)))
    br = jnp.pad(b_router[l].astype(F32), (0, LANE - N_EXPERTS)).reshape(1, LANE)
    h_p, hn_p, ids_p, gate_p = _outproj_router(xp, y_p, w_out[l].astype(BF16), norm_ffn[l], wr, br, False)
    h_s, hn_s, ids_s, gate_s = _outproj_router(xs, y_s, w_out[l], norm_ffn[l], wr, br, True)
    out_p, out_s = _moe(hn_p, hn_s, h_p, h_s, ids_p, gate_p, ids_s, gate_s, w_gu[l], b_gu[l],
                        w_down[l], b_down[l], norm_final)
    return (out_p.reshape(x_prompt.shape), out_s.reshape(x_sample.shape),
            kv_p[None], kv_s[None], win_p[None], win_s[None],
            gdn_p[None].astype(state_gdn.dtype), gdn_s[None].astype(state_gdn.dtype),
            conv_p[None], conv_s[None])
```

```python
import functools
import math

import jax
import jax.numpy as jnp
from jax import lax
from jax.experimental import pallas as pl
from jax.experimental.pallas import tpu as pltpu

F32 = jnp.float32
BF16 = jnp.bfloat16
HIGHEST = lax.Precision.HIGHEST

D_MODEL = 2048
GDN_HEADS = 16
GDN_DK = 128
GDN_DV = 128
CONV_W = 4
CONV_DIM = GDN_HEADS * (2 * GDN_DK + GDN_DV)
NSA_HEADS = 16
NSA_KV_HEADS = 4
HEAD_DIM = 128
NSA_HPG = NSA_HEADS // NSA_KV_HEADS
CMP_LEN = 32
CMP_STRIDE = 16
SEL_BLOCK = 64
N_SEL = 16
WINDOW = 512
PAGE_SIZE = 128
N_EXPERTS = 32
TOP_K = 4
D_FF = 2048
SWIGLU_LIMIT = 7.0
SWIGLU_ALPHA = 1.702
EPS = 1e-6
NEG = -1e30
FORCE = 1e6

IN_SPLITS = (CONV_DIM, GDN_HEADS * GDN_DV, GDN_HEADS, GDN_HEADS, NSA_HEADS * HEAD_DIM,
             6 * NSA_KV_HEADS * HEAD_DIM, 3 * NSA_HEADS, 2 * D_MODEL)
_OFF = [0]
for _s in IN_SPLITS:
    _OFF.append(_OFF[-1] + _s)
(OFF_QKV, OFF_Z, OFF_BRAW, OFF_ARAW, OFF_QB, OFF_KVB, OFF_GATEB, OFF_MERGE, N_IN) = _OFF

LANE = 128
CB_Q, CB_K, CB_V = 0, GDN_HEADS, 2 * GDN_HEADS
CB_Z = CONV_DIM // LANE
CB_MERGE = CB_Z + GDN_HEADS
CB_QB = CB_MERGE + 2 * D_MODEL // LANE
CB_KVB = CB_QB + NSA_HEADS
N_MAIN = (CB_KVB + 6 * NSA_KV_HEADS) * LANE
SM_BRAW, SM_ARAW, SM_GATE = 0, GDN_HEADS, 2 * GDN_HEADS

GDN_CHUNK = 128
GDN_HB = 8
NSA_TQ = 256
NSA_KT = 1024
MOE_BLK = 256
VMEM_LIMIT = 60000 * 1024


def _cp(sem, vmem=VMEM_LIMIT):
    return pltpu.CompilerParams(dimension_semantics=sem, vmem_limit_bytes=vmem)


def _lane_pick(x, lane):
    ids = lax.broadcasted_iota(jnp.int32, x.shape, 1)
    return jnp.sum(jnp.where(ids == lane, x, 0.0), axis=1, keepdims=True)


def _sigmoid(x):
    return 1.0 / (1.0 + jnp.exp(-x))


def _silu(x):
    return x * _sigmoid(x)


def _softplus(x):
    return jnp.maximum(x, 0.0) + jnp.log(1.0 + jnp.exp(-jnp.abs(x)))


def _proj_kernel(x_ref, g_ref, w_ref, ws_ref, o_ref, os_ref, xn_ref):
    @pl.when(pl.program_id(1) == 0)
    def _():
        x = x_ref[...]
        ms = jnp.mean(x * x, axis=-1, keepdims=True)
        xn = (x * lax.rsqrt(ms + EPS) * g_ref[...]).astype(BF16)
        xn_ref[...] = xn
        os_ref[...] = jnp.dot(xn, ws_ref[...], preferred_element_type=F32)

    o_ref[...] = jnp.dot(xn_ref[...], w_ref[...], preferred_element_type=F32)


REPACK_TN = 512
_RUNS = ((0, OFF_QKV), (OFF_BRAW - OFF_QKV, OFF_MERGE),
         (OFF_BRAW - OFF_QKV + N_IN - OFF_MERGE, OFF_QB))


def _repack_kernel(wt_hbm, o_ref, os_ref, buf_ref, sbuf_ref, sem, ssem):
    j = pl.program_id(0)
    nj = pl.num_programs(0)
    starts = [r[0] // REPACK_TN for r in _RUNS]

    def rows_copy(jj, slot):
        row = jj * REPACK_TN + (_RUNS[0][1] - _RUNS[0][0])
        for r in range(1, len(_RUNS)):
            row = jnp.where(jj >= starts[r], jj * REPACK_TN + (_RUNS[r][1] - _RUNS[r][0]), row)
        return pltpu.make_async_copy(wt_hbm.at[pl.ds(pl.multiple_of(row, 8), REPACK_TN), :],
                                     buf_ref.at[slot], sem.at[slot])

    def small_copies():
        return [pltpu.make_async_copy(wt_hbm.at[pl.ds(OFF_BRAW, SM_GATE), :],
                                      sbuf_ref.at[pl.ds(0, SM_GATE), :], ssem),
                pltpu.make_async_copy(wt_hbm.at[pl.ds(OFF_GATEB, 3 * NSA_HEADS), :],
                                      sbuf_ref.at[pl.ds(SM_GATE, 3 * NSA_HEADS), :], ssem)]

    @pl.when(j == 0)
    def _():
        rows_copy(0, 0).start()
        sbuf_ref[...] = jnp.zeros_like(sbuf_ref)
        for c in small_copies():
            c.start()
        for c in small_copies():
            c.wait()
        os_ref[...] = sbuf_ref[...].T.astype(BF16)

    slot = lax.rem(j, 2)

    @pl.when(j + 1 < nj)
    def _():
        rows_copy(j + 1, 1 - slot).start()

    rows_copy(j, slot).wait()
    o_ref[...] = buf_ref[slot].T.astype(BF16)


def _repack_w_in(w_t):
    assert all(r[0] % REPACK_TN == 0 and (r[1] - r[0]) % 8 == 0 for r in _RUNS)
    assert N_MAIN % REPACK_TN == 0 and OFF_QB - OFF_BRAW == SM_GATE
    return pl.pallas_call(
        _repack_kernel,
        out_shape=(jax.ShapeDtypeStruct((D_MODEL, N_MAIN), BF16),
                   jax.ShapeDtypeStruct((D_MODEL, LANE), BF16)),
        grid=(N_MAIN // REPACK_TN,),
        in_specs=[pl.BlockSpec(memory_space=pl.ANY)],
        out_specs=(pl.BlockSpec((D_MODEL, REPACK_TN), lambda j: (0, j)),
                   pl.BlockSpec((D_MODEL, LANE), lambda j: (0, 0))),
        scratch_shapes=[pltpu.VMEM((2, REPACK_TN, D_MODEL), F32), pltpu.VMEM((LANE, D_MODEL), F32),
                        pltpu.SemaphoreType.DMA((2,)), pltpu.SemaphoreType.DMA(())],
        compiler_params=_cp(("arbitrary",)),
        name="repack_w_in",
    )(w_t)


def _proj_prompt(x, norm_g, w_main, w_small):
    n = x.shape[0]
    tm = min(1024, n)
    tn = 1024
    return pl.pallas_call(
        _proj_kernel,
        out_shape=(jax.ShapeDtypeStruct((n, N_MAIN), F32), jax.ShapeDtypeStruct((n, LANE), F32)),
        grid=(n // tm, N_MAIN // tn),
        in_specs=[pl.BlockSpec((tm, D_MODEL), lambda i, j: (i, 0)),
                  pl.BlockSpec((1, D_MODEL), lambda i, j: (0, 0)),
                  pl.BlockSpec((D_MODEL, tn), lambda i, j: (0, j)),
                  pl.BlockSpec((D_MODEL, LANE), lambda i, j: (0, 0))],
        out_specs=(pl.BlockSpec((tm, tn), lambda i, j: (i, j)),
                   pl.BlockSpec((tm, LANE), lambda i, j: (i, 0))),
        scratch_shapes=[pltpu.VMEM((tm, D_MODEL), BF16)],
        compiler_params=_cp(("parallel", "arbitrary")),
        name="proj_prompt",
    )(x, norm_g.reshape(1, D_MODEL), w_main, w_small)


def _shift_rows(x, prev8, s):
    xs = pltpu.roll(x, s, 0)
    ps = pltpu.roll(prev8, s, 0)
    row = lax.broadcasted_iota(jnp.int32, prev8.shape, 0)
    top = jnp.where(row < s, ps, xs[0:8])
    return jnp.concatenate([top, xs[8:]], axis=0)


def _conv_silu(x, prev8, cw):
    y = x * cw[CONV_W - 1:CONV_W]
    for s in range(1, CONV_W):
        y = y + _shift_rows(x, prev8, s) * cw[CONV_W - 1 - s:CONV_W - s]
    return _silu(y)


def _l2norm(x):
    return x * lax.rsqrt(jnp.sum(x * x, axis=-1, keepdims=True) + EPS)


def _cumsum_rows(x):
    c = x.shape[0]
    row = lax.broadcasted_iota(jnp.int32, x.shape, 0)
    s = 1
    while s < c:
        x = x + jnp.where(row >= s, pltpu.roll(x, s, 0), 0.0)
        s *= 2
    return x


def _bdot(a, b):
    return jnp.dot(a.astype(BF16), b.astype(BF16), preferred_element_type=F32)


def _bdot_nt(a, b):
    return lax.dot_general(a.astype(BF16), b.astype(BF16), (((1,), (1,)), ((), ())),
                           preferred_element_type=F32)


def _bdot_tn(a, b):
    return lax.dot_general(a.astype(BF16), b.astype(BF16), (((0,), (0,)), ((), ())),
                           preferred_element_type=F32)


INV_BASE = 16


def _unit_lower_inverses(mats, ri, ci):
    c = mats[0].shape[0]
    same = lambda s: (ri // s) == (ci // s)
    eye = jnp.where(ri == ci, 1.0, 0.0)
    blk = same(INV_BASE)
    ps = [jnp.where(blk, a, 0.0) for a in mats]
    ts = [eye - p for p in ps]
    n = 2
    while n < INV_BASE:
        ps = [_bdot(p, p) for p in ps]
        ts = [t + _bdot(t, p) for t, p in zip(ts, ps)]
        n *= 2
    s = INV_BASE
    while s < c:
        off = same(2 * s) & jnp.logical_not(same(s))
        tm = [_bdot(t, jnp.where(off, a, 0.0)) for t, a in zip(ts, mats)]
        ts = [t - _bdot(x, t) for t, x in zip(ts, tm)]
        s *= 2
    return ts


def _gdn_kernel(q_ref, k_ref, v_ref, z_ref, sm_ref, cwq_ref, cwk_ref, cwv_ref, hp_ref, ng_ref,
                o_ref, s_out_ref, s_ref, pq_ref, pk_ref, pv_ref):
    hg = pl.program_id(1)
    c = pl.program_id(2)
    C = GDN_CHUNK

    @pl.when(c == 0)
    def _():
        s_ref[...] = jnp.zeros_like(s_ref)
        pq_ref[...] = jnp.zeros_like(pq_ref)
        pk_ref[...] = jnp.zeros_like(pk_ref)
        pv_ref[...] = jnp.zeros_like(pv_ref)

    xq, xk, xv = q_ref[...], k_ref[...], v_ref[...]
    yq = _conv_silu(xq, pq_ref[...], cwq_ref[...])
    yk = _conv_silu(xk, pk_ref[...], cwk_ref[...])
    yv = _conv_silu(xv, pv_ref[...], cwv_ref[...])
    pq_ref[...] = xq[C - 8:]
    pk_ref[...] = xk[C - 8:]
    pv_ref[...] = xv[C - 8:]

    sm = sm_ref[...]
    hp = hp_ref[...]
    g_all = -jnp.exp(hp[0:1]) * _softplus(sm + hp[1:2])
    G_all = _cumsum_rows(g_all)
    ri = lax.broadcasted_iota(jnp.int32, (C, C), 0)
    ci = lax.broadcasted_iota(jnp.int32, (C, C), 1)
    z = _silu(z_ref[...])
    HS = range(GDN_HB)
    cols = [slice(i * LANE, (i + 1) * LANE) for i in HS]
    q = [_l2norm(yq[:, cols[i]]) * (GDN_DK ** -0.5) for i in HS]
    k = [_l2norm(yk[:, cols[i]]) for i in HS]
    v = [yv[:, cols[i]] for i in HS]
    beta = [_sigmoid(_lane_pick(sm, SM_BRAW + hg * GDN_HB + i)) for i in HS]
    G = [_lane_pick(G_all, SM_ARAW + hg * GDN_HB + i) for i in HS]
    eG = [jnp.exp(G[i]) for i in HS]
    g_last = [G[i][C - 1:C] for i in HS]
    Gb = [jnp.broadcast_to(G[i], (C, C)) for i in HS]
    decay = [jnp.exp(jnp.where(ri >= ci, Gb[i] - Gb[i].T, NEG)) for i in HS]
    kb = [k[i] * beta[i] for i in HS]
    kk = [_bdot_nt(kb[i], k[i]) for i in HS]
    qk = [_bdot_nt(q[i], k[i]) for i in HS]
    a_mat = [jnp.where(ri > ci, kk[i] * decay[i], 0.0) for i in HS]
    t_inv = _unit_lower_inverses(a_mat, ri, ci)
    rhs = [jnp.concatenate([v[i] * beta[i], kb[i] * eG[i]], axis=1) for i in HS]
    x = [_bdot(t_inv[i], rhs[i]) for i in HS]
    S = [s_ref[i] for i in HS]
    q_s = [_bdot(q[i] * eG[i], S[i]) for i in HS]
    w_s = [_bdot(x[i][:, GDN_DV:], S[i]) for i in HS]
    v_new = [x[i][:, :GDN_DV] - w_s[i] for i in HS]
    o = [q_s[i] + _bdot(qk[i] * decay[i], v_new[i]) for i in HS]
    kv = [_bdot_tn(k[i] * jnp.exp(g_last[i] - G[i]), v_new[i]) for i in HS]
    s_new = [S[i] * jnp.exp(g_last[i]) + kv[i] for i in HS]
    for i in HS:
        s_ref[i] = s_new[i]
        on = o[i] * lax.rsqrt(jnp.mean(o[i] * o[i], axis=-1, keepdims=True) + EPS) * ng_ref[...]
        o_ref[:, cols[i]] = (on * z[:, cols[i]]).astype(o_ref.dtype)

    @pl.when(c == pl.num_programs(2) - 1)
    def _():
        for i in HS:
            s_out_ref[0, i] = s_new[i]


def _head_params(a_log, dt_bias):
    hp = jnp.zeros((8, LANE), F32)
    hp = hp.at[0, SM_ARAW:SM_ARAW + GDN_HEADS].set(a_log.astype(F32))
    return hp.at[1, SM_ARAW:SM_ARAW + GDN_HEADS].set(dt_bias.astype(F32))


def _gdn_prompt(proj, small, conv_w, head_par, norm_g, batch, seq):
    C = GDN_CHUNK
    nc = seq // C
    H = GDN_HEADS
    HB = GDN_HB
    W = HB * LANE

    def rows(cb):
        return pl.BlockSpec((C, W), lambda b, h, c, cb=cb: (b * nc + c, cb // HB + h))

    def cw(cb):
        return pl.BlockSpec((CONV_W, W), lambda b, h, c, cb=cb: (0, cb // HB + h))

    return pl.pallas_call(
        _gdn_kernel,
        out_shape=(jax.ShapeDtypeStruct((batch * seq, H * GDN_DV), BF16),
                   jax.ShapeDtypeStruct((batch, H, GDN_DK, GDN_DV), F32)),
        grid=(batch, H // HB, nc),
        in_specs=[rows(CB_Q), rows(CB_K), rows(CB_V), rows(CB_Z),
                  pl.BlockSpec((C, LANE), lambda b, h, c: (b * nc + c, 0)),
                  cw(CB_Q), cw(CB_K), cw(CB_V),
                  pl.BlockSpec((8, LANE), lambda b, h, c: (0, 0)),
                  pl.BlockSpec((1, GDN_DV), lambda b, h, c: (0, 0))],
        out_specs=(pl.BlockSpec((C, W), lambda b, h, c: (b * nc + c, h)),
                   pl.BlockSpec((1, HB, GDN_DK, GDN_DV), lambda b, h, c: (b, h, 0, 0))),
        scratch_shapes=[pltpu.VMEM((HB, GDN_DK, GDN_DV), F32), pltpu.VMEM((8, W), F32),
                        pltpu.VMEM((8, W), F32), pltpu.VMEM((8, W), F32)],
        compiler_params=_cp(("parallel", "parallel", "arbitrary")),
        name="gdn_prompt",
    )(proj, proj, proj, proj, small, conv_w, conv_w, conv_w, head_par, norm_g.reshape(1, GDN_DV))


def _cmp_kernel(x_ref, w_ref, p_ref, o_ref):
    t = x_ref.shape[0]
    nch = t // CMP_STRIDE
    x3 = x_ref[...].reshape(nch, CMP_STRIDE, HEAD_DIM)
    w = w_ref[0]
    s0 = jnp.sum(x3 * w[None, :CMP_STRIDE], axis=1)
    s1 = jnp.sum(x3 * w[None, CMP_STRIDE:], axis=1)
    row = lax.broadcasted_iota(jnp.int32, (nch, HEAD_DIM), 0)
    pooled = jnp.where(row < nch - 1, s0 + pltpu.roll(s1, nch - 1, 0), 0.0)
    o_ref[0, 0, 0] = jnp.dot(pooled, p_ref[0], precision=HIGHEST, preferred_element_type=F32)


def _compress_prompt(proj, cmp_pos, cmp_proj, batch, seq):
    nch = seq // CMP_STRIDE
    return pl.pallas_call(
        _cmp_kernel,
        out_shape=jax.ShapeDtypeStruct((batch, 2, NSA_KV_HEADS, nch, HEAD_DIM), F32),
        grid=(batch, 2, NSA_KV_HEADS),
        in_specs=[pl.BlockSpec((seq, LANE), lambda b, e, g: (b, CB_KVB + e * NSA_KV_HEADS + g)),
                  pl.BlockSpec((1, CMP_LEN, HEAD_DIM), lambda b, e, g: (e, 0, 0)),
                  pl.BlockSpec((1, HEAD_DIM, HEAD_DIM), lambda b, e, g: (e, 0, 0))],
        out_specs=pl.BlockSpec((1, 1, 1, nch, HEAD_DIM), lambda b, e, g: (b, e, g, 0, 0)),
        compiler_params=_cp(("parallel", "parallel", "parallel")),
        name="nsa_compress_prompt",
    )(proj, cmp_pos, cmp_proj)


def _masked_softmax(s, mask):
    s = jnp.where(mask, s, NEG)
    p = jnp.exp(s - jnp.max(s, axis=-1, keepdims=True)) * mask.astype(F32)
    return p / jnp.maximum(jnp.sum(p, axis=-1, keepdims=True), 1e-30)


def _select_blocks_t(score_t):
    nj = score_t.shape[0]
    sub = 8
    groups = [score_t[v:v + sub] for v in range(0, nj, sub)]
    j_in = lax.broadcasted_iota(jnp.int32, groups[0].shape, 0)
    cnts = [jnp.zeros(g.shape, jnp.int32) for g in groups]
    for jp in range(nj):
        row = score_t[jp:jp + 1, :]
        for v, g in enumerate(groups):
            if v * sub + sub - 1 <= jp:
                ahead = jnp.where(row > g, 1, 0)
            elif v * sub > jp:
                ahead = jnp.where(row >= g, 1, 0)
            else:
                ahead = jnp.where(j_in + v * sub > jp, jnp.where(row >= g, 1, 0),
                                  jnp.where(row > g, 1, 0))
            cnts[v] = cnts[v] + ahead
    cnt = jnp.concatenate(cnts, axis=0)
    return (cnt < N_SEL) & (score_t > NEG / 2)


def _nsa_kernel(q0_ref, q1_ref, q2_ref, q3_ref, sm_ref, ck_ref, cv_ref, ks_ref, vs_ref, kw_ref,
                vw_ref, o_ref, acc_ref, m_ref, l_ref, kvb_ref, s_ref):
    g = pl.program_id(1)
    qi = pl.program_id(2)
    TQ = q0_ref.shape[0]
    R4 = NSA_HPG * TQ
    T = ks_ref.shape[0]
    KT = min(NSA_KT, T)
    qs = qi * TQ

    @pl.when(qi == 0)
    def _():
        for i, r in enumerate((ks_ref, vs_ref, kw_ref, vw_ref)):
            kvb_ref[i] = r[...].astype(BF16)

    ksb_ref, vsb_ref, kwb_ref, vwb_ref = (kvb_ref.at[i] for i in range(4))
    q4 = jnp.concatenate([q0_ref[...], q1_ref[...], q2_ref[...], q3_ref[...]], axis=0)
    q4 = (q4 * (HEAD_DIM ** -0.5)).astype(BF16)
    t4 = qs + lax.rem(lax.broadcasted_iota(jnp.int32, (R4, 1), 0), TQ)
    t1 = qs + lax.broadcasted_iota(jnp.int32, (TQ, 1), 0)

    ck = ck_ref[0, 0, 0]
    nc = ck.shape[0]
    WK = min(WINDOW + TQ, T)
    w0 = pl.multiple_of(jnp.minimum(jnp.maximum(qs - WINDOW, 0), T - WK), TQ)
    s_cmp = _bdot_nt(q4, ck)
    s_win = _bdot_nt(q4, kwb_ref[pl.ds(w0, WK), :])
    s_ref[...] = _bdot_nt(q4, ksb_ref[pl.ds(0, KT), :])

    cmp_end = CMP_STRIDE * lax.broadcasted_iota(jnp.int32, (1, nc), 1) + (CMP_LEN - 1)
    p = _masked_softmax(s_cmp, cmp_end <= t4)
    o_cmp = _bdot(p, cv_ref[0, 0, 0])

    n_slc = T // SEL_BLOCK
    nsp = max(n_slc, LANE)
    psum = p[0:TQ]
    for j in range(1, NSA_HPG):
        psum = psum + p[j * TQ:(j + 1) * TQ]
    ci = CMP_STRIDE * lax.broadcasted_iota(jnp.int32, (nc, nsp), 0)
    bj = SEL_BLOCK * lax.broadcasted_iota(jnp.int32, (nc, nsp), 1)
    cmap = ((ci < bj + SEL_BLOCK) & (ci + CMP_LEN > bj)).astype(F32)
    imp = jnp.dot(psum, cmap, precision=HIGHEST, preferred_element_type=F32)
    imp_t = imp.T[:n_slc]
    j_idx = lax.broadcasted_iota(jnp.int32, (n_slc, TQ), 0)
    jt = lax.shift_right_logical(qs + lax.broadcasted_iota(jnp.int32, (1, TQ), 1),
                                 int(math.log2(SEL_BLOCK)))
    force = (j_idx == 0) | (j_idx == jt) | (j_idx == jt - 1)
    score_t = jnp.where(j_idx > jt, NEG, jnp.where(force, FORCE, imp_t))
    sel_t = jnp.where(_select_blocks_t(score_t), 1.0, 0.0)

    wpos = w0 + lax.broadcasted_iota(jnp.int32, (1, WK), 1)
    p = _masked_softmax(s_win, (wpos <= t4) & (wpos > t4 - WINDOW))
    o_win = _bdot(p, vwb_ref[pl.ds(w0, WK), :])

    acc_ref[...] = jnp.zeros_like(acc_ref)
    m_ref[...] = jnp.full_like(m_ref, NEG)
    l_ref[...] = jnp.zeros_like(l_ref)
    n_kt = (qs + TQ - 1) // KT + 1

    def slc_body(kt, carry):
        k0 = pl.multiple_of(kt * KT, KT)
        vb = vsb_ref[pl.ds(k0, KT), :]
        s = s_ref[...]
        k1 = pl.multiple_of(jnp.minimum(kt + 1, T // KT - 1) * KT, KT)
        s_next = _bdot_nt(q4, ksb_ref[pl.ds(k1, KT), :])
        kpos = k0 + lax.broadcasted_iota(jnp.int32, (1, KT), 1)
        eb = lax.broadcasted_iota(jnp.int32, (n_slc, KT), 0) == lax.shift_right_logical(
            k0 + lax.broadcasted_iota(jnp.int32, (n_slc, KT), 1), int(math.log2(SEL_BLOCK)))
        selx = _bdot_tn(sel_t, jnp.where(eb, 1.0, 0.0))
        bias = jnp.where((selx > 0.5) & (kpos <= t1), 0.0, NEG)
        s = s + jnp.concatenate([bias] * NSA_HPG, axis=0)
        m_old = m_ref[...]
        m_new = jnp.maximum(m_old, jnp.max(s, axis=-1, keepdims=True))
        pe = jnp.exp(s - m_new)
        alpha = jnp.exp(m_old - m_new)
        l_ref[...] = alpha * l_ref[...] + jnp.sum(pe, axis=-1, keepdims=True)
        acc_ref[...] = alpha * acc_ref[...] + _bdot(pe, vb)
        m_ref[...] = m_new
        s_ref[...] = s_next
        return carry

    lax.fori_loop(0, n_kt, slc_body, 0)
    o_slc = acc_ref[...] / jnp.maximum(l_ref[...], 1e-30)

    gates = _sigmoid(sm_ref[...])
    for j in range(NSA_HPG):
        lane0 = SM_GATE + (g * NSA_HPG + j) * 3
        rows = slice(j * TQ, (j + 1) * TQ)
        ob = (o_cmp[rows] * _lane_pick(gates, lane0) + o_slc[rows] * _lane_pick(gates, lane0 + 1)
              + o_win[rows] * _lane_pick(gates, lane0 + 2))
        o_ref[:, j * HEAD_DIM:(j + 1) * HEAD_DIM] = ob.astype(o_ref.dtype)


def _nsa_prompt(proj, small, comp, batch, seq):
    TQ = min(NSA_TQ, seq)
    nq = seq // TQ
    nch = seq // CMP_STRIDE
    G = NSA_KV_HEADS

    def qspec(j):
        return pl.BlockSpec((TQ, LANE), lambda b, g, i, j=j: (b * nq + i, CB_QB + g * NSA_HPG + j))

    def kvspec(e):
        return pl.BlockSpec((seq, LANE), lambda b, g, i, e=e: (b, CB_KVB + e * G + g))

    def cspec(e):
        return pl.BlockSpec((1, 1, 1, nch, HEAD_DIM), lambda b, g, i, e=e: (b, e, g, 0, 0))

    R4 = NSA_HPG * TQ
    return pl.pallas_call(
        _nsa_kernel,
        out_shape=jax.ShapeDtypeStruct((batch * seq, NSA_HEADS * HEAD_DIM), BF16),
        grid=(batch, G, nq),
        in_specs=[qspec(0), qspec(1), qspec(2), qspec(3),
                  pl.BlockSpec((TQ, LANE), lambda b, g, i: (b * nq + i, 0)),
                  cspec(0), cspec(1), kvspec(2), kvspec(3), kvspec(4), kvspec(5)],
        out_specs=pl.BlockSpec((TQ, NSA_HPG * HEAD_DIM), lambda b, g, i: (b * nq + i, g)),
        scratch_shapes=[pltpu.VMEM((R4, HEAD_DIM), F32), pltpu.VMEM((R4, 1), F32),
                        pltpu.VMEM((R4, 1), F32), pltpu.VMEM((4, seq, HEAD_DIM), BF16),
                        pltpu.VMEM((R4, min(NSA_KT, seq)), F32)],
        compiler_params=_cp(("parallel", "parallel", "arbitrary")),
        name="nsa_prompt",
    )(proj, proj, proj, proj, small, comp, comp, proj, proj, proj, proj)


def _dot(a, b, exact):
    if exact:
        return jnp.dot(a.astype(F32), b.astype(F32), precision=HIGHEST, preferred_element_type=F32)
    return jnp.dot(a.astype(BF16), b.astype(BF16), preferred_element_type=F32)


def _merge_kernel(oa_ref, ob_ref, wa_ref, wb_ref, m0_ref, m1_ref, y_ref, *, exact):
    ya = _dot(oa_ref[...], wa_ref[...], exact)
    yb = _dot(ob_ref[...], wb_ref[...], exact)
    y = _sigmoid(m0_ref[...]) * ya + _sigmoid(m1_ref[...]) * yb
    y_ref[...] = y.astype(y_ref.dtype)


def _merge(o_a, o_b, w_a, w_b, gates, m0_cb, m1_cb, exact):
    n = o_a.shape[0]
    tm = min(1024, n)
    tn = 512
    cpb = tn // LANE
    return pl.pallas_call(
        functools.partial(_merge_kernel, exact=exact),
        out_shape=jax.ShapeDtypeStruct((n, D_MODEL), F32 if exact else BF16),
        grid=(n // tm, D_MODEL // tn),
        in_specs=[pl.BlockSpec((tm, D_MODEL), lambda i, j: (i, 0)),
                  pl.BlockSpec((tm, D_MODEL), lambda i, j: (i, 0)),
                  pl.BlockSpec((D_MODEL, tn), lambda i, j: (0, j)),
                  pl.BlockSpec((D_MODEL, tn), lambda i, j: (0, j)),
                  pl.BlockSpec((tm, tn), lambda i, j: (i, m0_cb // cpb + j)),
                  pl.BlockSpec((tm, tn), lambda i, j: (i, m1_cb // cpb + j))],
        out_specs=pl.BlockSpec((tm, tn), lambda i, j: (i, j)),
        compiler_params=_cp(("parallel", "parallel")),
        name="merge_exact" if exact else "merge",
    )(o_a, o_b, w_a, w_b, gates, gates)


def _top4(logits):
    lane = lax.broadcasted_iota(jnp.int32, logits.shape, 1)
    ids = jnp.full(logits.shape, -1, jnp.int32)
    vals = jnp.full(logits.shape, NEG, F32)
    cur = logits
    for k in range(TOP_K):
        m = jnp.max(cur, axis=1, keepdims=True)
        idx = jnp.min(jnp.where(cur == m, lane, LANE), axis=1, keepdims=True)
        ids = jnp.where(lane == k, idx, ids)
        vals = jnp.where(lane == k, m, vals)
        cur = jnp.where(lane == idx, 2 * NEG, cur)
    e = jnp.where(lane < TOP_K, jnp.exp(vals - jnp.max(vals, axis=1, keepdims=True)), 0.0)
    return ids, e / jnp.sum(e, axis=1, keepdims=True)


def _outproj_kernel(x_ref, y_ref, w_ref, gn_ref, wr_ref, br_ref, h_ref, hn_ref, ids_ref, gate_ref,
                    *, exact):
    h = x_ref[...] + _dot(y_ref[...], w_ref[...], exact)
    h_ref[...] = h
    hn = h * lax.rsqrt(jnp.mean(h * h, axis=-1, keepdims=True) + EPS) * gn_ref[...]
    hn_ref[...] = hn
    logits = jnp.dot(hn, wr_ref[...], precision=HIGHEST, preferred_element_type=F32) + br_ref[...]
    lane = lax.broadcasted_iota(jnp.int32, logits.shape, 1)
    ids, gate = _top4(jnp.where(lane < N_EXPERTS, logits, NEG))
    ids_ref[...] = ids
    gate_ref[...] = gate


def _outproj_router(x, y, w_out, norm_g, w_router, b_router, exact):
    n = x.shape[0]
    tm = min(256, n)
    row = lambda i: (i, 0)
    fixed = lambda i: (0, 0)
    return pl.pallas_call(
        functools.partial(_outproj_kernel, exact=exact),
        out_shape=(jax.ShapeDtypeStruct((n, D_MODEL), F32), jax.ShapeDtypeStruct((n, D_MODEL), F32),
                   jax.ShapeDtypeStruct((n, LANE), jnp.int32), jax.ShapeDtypeStruct((n, LANE), F32)),
        grid=(n // tm,),
        in_specs=[pl.BlockSpec((tm, D_MODEL), row), pl.BlockSpec((tm, D_MODEL), row),
                  pl.BlockSpec((D_MODEL, D_MODEL), fixed), pl.BlockSpec((1, D_MODEL), fixed),
                  pl.BlockSpec((D_MODEL, LANE), fixed), pl.BlockSpec((1, LANE), fixed)],
        out_specs=(pl.BlockSpec((tm, D_MODEL), row), pl.BlockSpec((tm, D_MODEL), row),
                   pl.BlockSpec((tm, LANE), row), pl.BlockSpec((tm, LANE), row)),
        compiler_params=_cp(("parallel",)),
        name="outproj_router_exact" if exact else "outproj_router",
    )(x, y, w_out, norm_g.reshape(1, D_MODEL), w_router, b_router)


def _rank_kernel(ids_ref, rank_ref, cnt_ref, carry_ref):
    i = pl.program_id(0)

    @pl.when(i == 0)
    def _():
        carry_ref[...] = jnp.zeros_like(carry_ref)

    ids = ids_ref[...]
    tm = ids.shape[0]
    lane = lax.broadcasted_iota(jnp.int32, ids.shape, 1)
    cols = [jnp.sum(jnp.where(lane == k, ids, 0), axis=1, keepdims=True) for k in range(TOP_K)]
    mask = jnp.zeros(ids.shape, F32)
    for k in range(TOP_K):
        mask = mask + (lane == cols[k]).astype(F32)
    r = lax.broadcasted_iota(jnp.int32, (tm, tm), 0)
    c = lax.broadcasted_iota(jnp.int32, (tm, tm), 1)
    before = jnp.dot((r > c).astype(BF16), mask.astype(BF16), preferred_element_type=F32)
    pos = before + carry_ref[0:1]
    rank = jnp.zeros(ids.shape, F32)
    for k in range(TOP_K):
        rk = jnp.sum(jnp.where(lane == cols[k], pos, 0.0), axis=1, keepdims=True)
        rank = jnp.where(lane == k, rk, rank)
    rank_ref[...] = rank.astype(jnp.int32)
    total = carry_ref[0:1] + jnp.sum(mask, axis=0, keepdims=True)
    carry_ref[...] = jnp.broadcast_to(total, carry_ref.shape)
    cnt_ref[...] = jnp.broadcast_to(total, cnt_ref.shape).astype(jnp.int32)


def _expert_ranks(ids):
    n = ids.shape[0]
    tm = min(256, n)
    return pl.pallas_call(
        _rank_kernel,
        out_shape=(jax.ShapeDtypeStruct((n, LANE), jnp.int32), jax.ShapeDtypeStruct((8, LANE), jnp.int32)),
        grid=(n // tm,),
        in_specs=[pl.BlockSpec((tm, LANE), lambda i: (i, 0))],
        out_specs=(pl.BlockSpec((tm, LANE), lambda i: (i, 0)), pl.BlockSpec((8, LANE), lambda i: (0, 0))),
        scratch_shapes=[pltpu.VMEM((8, LANE), F32)],
        compiler_params=_cp(("arbitrary",)),
        name="expert_ranks",
    )(ids)


def _scatter_kernel(slot_ref, x_ref, xs_in_ref, xs_ref, sem):
    del xs_in_ref
    tm = x_ref.shape[0]

    def copy(r, k):
        return pltpu.make_async_copy(x_ref.at[pl.ds(r, 1)],
                                     xs_ref.at[pl.ds(slot_ref[0, 0, r * TOP_K + k], 1)], sem)

    def start(r, c):
        for k in range(TOP_K):
            copy(r, k).start(priority=k % 2)
        return c

    def wait(r, c):
        for k in range(TOP_K):
            copy(r, k).wait()
        return c

    lax.fori_loop(0, tm, start, 0)
    lax.fori_loop(0, tm, wait, 0)


def _scatter_rows(x, slots, xs):
    n = x.shape[0]
    tm = min(256, n)
    nt = n // tm
    return pl.pallas_call(
        _scatter_kernel,
        out_shape=jax.ShapeDtypeStruct(xs.shape, xs.dtype),
        grid=(nt,),
        in_specs=[pl.BlockSpec((1, 1, tm * TOP_K), lambda i: (i, 0, 0), memory_space=pltpu.SMEM),
                  pl.BlockSpec((tm, D_MODEL), lambda i: (i, 0)),
                  pl.BlockSpec(memory_space=pl.ANY)],
        out_specs=pl.BlockSpec(memory_space=pl.ANY),
        scratch_shapes=[pltpu.SemaphoreType.DMA(())],
        input_output_aliases={2: 0},
        compiler_params=_cp(("arbitrary",)),
        name="moe_scatter",
    )(slots.reshape(nt, 1, tm * TOP_K), x, xs)


def _stream_expert_weights(be_ref, nu_ref, ne_ref, nj_ref, cnt_ref, tile_copies, consume):
    j = pl.program_id(0)
    b = pl.program_id(1)
    step = j * pl.num_programs(1) + b
    used = b < nu_ref[0]
    e = be_ref[b]
    first = used & ((b == 0) | (e != be_ref[jnp.maximum(b - 1, 0)]))

    @pl.when(step == 0)
    def _():
        cnt_ref[0] = 0
        for c in tile_copies(e, j, 0):
            c.start()

    @pl.when(first)
    def _():
        slot = lax.rem(cnt_ref[0], 2)
        for c in tile_copies(e, j, slot):
            c.wait()
        consume(slot)
        ne = ne_ref[step]

        @pl.when(ne >= 0)
        def _():
            for c in tile_copies(ne, nj_ref[step], 1 - slot):
                c.start()

        cnt_ref[0] = cnt_ref[0] + 1

    return used


def _next_tiles(blk_e, n_used, n_col_tiles):
    nb = blk_e.shape[0]
    used = jnp.arange(nb) < n_used[0]
    prev = jnp.concatenate([jnp.full((1,), -1, blk_e.dtype), blk_e[:-1]])
    first = jnp.tile(used & (blk_e != prev), n_col_tiles)
    n = n_col_tiles * nb
    key = jnp.where(first, jnp.arange(n), n)
    nxt = jnp.concatenate([lax.cummin(key, axis=0, reverse=True)[1:], jnp.full((1,), n, key.dtype)])
    ok = nxt < n
    nxt = jnp.minimum(nxt, n - 1)
    return (jnp.where(ok, blk_e[nxt % nb], -1).astype(jnp.int32),
            jnp.where(ok, nxt // nb, 0).astype(jnp.int32))


def _moe_up_kernel(be_ref, nu_ref, ne_ref, nj_ref, x_ref, w_hbm, bg_ref, bu_ref, o_ref,
                   wbuf_ref, wgb_ref, wub_ref, cnt_ref, sem):
    tf = wgb_ref.shape[1]
    nf = pl.num_programs(0)

    def tile_copies(e, j, slot):
        return [pltpu.make_async_copy(
            w_hbm.at[e, :, pl.ds(pl.multiple_of((p * nf + j) * tf, tf), tf)],
            wbuf_ref.at[slot, p], sem.at[slot]) for p in range(2)]

    def consume(slot):
        wgb_ref[...] = wbuf_ref[slot, 0].astype(BF16)
        wub_ref[...] = wbuf_ref[slot, 1].astype(BF16)

    used = _stream_expert_weights(be_ref, nu_ref, ne_ref, nj_ref, cnt_ref, tile_copies, consume)

    @pl.when(used)
    def _():
        x = x_ref[...].astype(BF16)
        g = jnp.dot(x, wgb_ref[...], preferred_element_type=F32) + bg_ref[0]
        u = jnp.dot(x, wub_ref[...], preferred_element_type=F32) + bu_ref[0]
        g = jnp.minimum(g, SWIGLU_LIMIT)
        u = jnp.clip(u, -SWIGLU_LIMIT, SWIGLU_LIMIT)
        o_ref[...] = ((u + 1.0) * g * _sigmoid(SWIGLU_ALPHA * g)).astype(o_ref.dtype)

    @pl.when(jnp.logical_not(used))
    def _():
        o_ref[...] = jnp.zeros_like(o_ref)


def _moe_up(xs, blk_e, n_used, w_gu, b_gu, tf=1024):
    n_blocks = xs.shape[0] // MOE_BLK
    nf = D_FF // tf
    next_e, next_j = _next_tiles(blk_e, n_used, nf)
    grid_spec = pltpu.PrefetchScalarGridSpec(
        num_scalar_prefetch=4,
        grid=(nf, n_blocks),
        in_specs=[pl.BlockSpec((MOE_BLK, D_MODEL), lambda j, b, be, *_: (b, 0)),
                  pl.BlockSpec(memory_space=pl.ANY),
                  pl.BlockSpec((1, 1, tf), lambda j, b, be, *_: (be[b], 0, j)),
                  pl.BlockSpec((1, 1, tf), lambda j, b, be, *_: (be[b], 0, nf + j))],
        out_specs=pl.BlockSpec((MOE_BLK, tf), lambda j, b, be, *_: (b, j)),
        scratch_shapes=[pltpu.VMEM((2, 2, D_MODEL, tf), F32), pltpu.VMEM((D_MODEL, tf), BF16),
                        pltpu.VMEM((D_MODEL, tf), BF16), pltpu.SMEM((1,), jnp.int32),
                        pltpu.SemaphoreType.DMA((2,))])
    return pl.pallas_call(
        _moe_up_kernel,
        out_shape=jax.ShapeDtypeStruct((xs.shape[0], D_FF), BF16),
        grid_spec=grid_spec,
        compiler_params=_cp(("arbitrary", "arbitrary")),
        name="moe_up",
    )(blk_e, n_used, next_e, next_j, xs, w_gu, b_gu.reshape(N_EXPERTS, 1, 2 * D_FF),
      b_gu.reshape(N_EXPERTS, 1, 2 * D_FF))


def _moe_down_kernel(be_ref, nu_ref, ne_ref, nj_ref, h_ref, w_hbm, b_ref, o_ref,
                     wbuf_ref, wb_ref, cnt_ref, sem):
    tn = wb_ref.shape[1]

    def tile_copies(e, j, slot):
        return [pltpu.make_async_copy(w_hbm.at[e, :, pl.ds(pl.multiple_of(j * tn, tn), tn)],
                                      wbuf_ref.at[slot], sem.at[slot])]

    def consume(slot):
        wb_ref[...] = wbuf_ref[slot].astype(BF16)

    used = _stream_expert_weights(be_ref, nu_ref, ne_ref, nj_ref, cnt_ref, tile_copies, consume)

    @pl.when(used)
    def _():
        o_ref[...] = jnp.dot(h_ref[...], wb_ref[...], preferred_element_type=F32) + b_ref[0]

    @pl.when(jnp.logical_not(used))
    def _():
        o_ref[...] = jnp.zeros_like(o_ref)


def _moe_down(hid, blk_e, n_used, w_down, b_down, tn=D_MODEL):
    n_blocks = hid.shape[0] // MOE_BLK
    nn = D_MODEL // tn
    next_e, next_j = _next_tiles(blk_e, n_used, nn)
    grid_spec = pltpu.PrefetchScalarGridSpec(
        num_scalar_prefetch=4,
        grid=(nn, n_blocks),
        in_specs=[pl.BlockSpec((MOE_BLK, D_FF), lambda j, b, be, *_: (b, 0)),
                  pl.BlockSpec(memory_space=pl.ANY),
                  pl.BlockSpec((1, 1, tn), lambda j, b, be, *_: (be[b], 0, j))],
        out_specs=pl.BlockSpec((MOE_BLK, tn), lambda j, b, be, *_: (b, j)),
        scratch_shapes=[pltpu.VMEM((2, D_FF, tn), F32), pltpu.VMEM((D_FF, tn), BF16),
                        pltpu.SMEM((1,), jnp.int32), pltpu.SemaphoreType.DMA((2,))])
    return pl.pallas_call(
        _moe_down_kernel,
        out_shape=jax.ShapeDtypeStruct((hid.shape[0], D_MODEL), F32),
        grid_spec=grid_spec,
        compiler_params=_cp(("arbitrary", "arbitrary")),
        name="moe_down",
    )(blk_e, n_used, next_e, next_j, hid, w_down, b_down.reshape(N_EXPERTS, 1, D_MODEL))


def _combine_kernel(slot_ref, h_ref, gate_ref, gn_ref, y_ref, o_ref, buf_ref, sem):
    tm = h_ref.shape[0]

    def copy(r, k):
        return pltpu.make_async_copy(y_ref.at[pl.ds(slot_ref[0, 0, r * TOP_K + k], 1)],
                                     buf_ref.at[k, pl.ds(r, 1)], sem)

    def start(r, c):
        for k in range(TOP_K):
            copy(r, k).start(priority=k % 2)
        return c

    def wait(r, c):
        for k in range(TOP_K):
            copy(r, k).wait()
        return c

    lax.fori_loop(0, tm, start, 0)
    lax.fori_loop(0, tm, wait, 0)
    gate = gate_ref[...]
    acc = h_ref[...]
    for k in range(TOP_K):
        acc = acc + gate[:, k:k + 1] * buf_ref[k]
    o_ref[...] = acc * lax.rsqrt(jnp.mean(acc * acc, axis=-1, keepdims=True) + EPS) * gn_ref[...]


def _combine(h, gate, slots, y, norm_g):
    n = h.shape[0]
    tm = min(256, n)
    nt = n // tm
    return pl.pallas_call(
        _combine_kernel,
        out_shape=jax.ShapeDtypeStruct((n, D_MODEL), F32),
        grid=(nt,),
        in_specs=[pl.BlockSpec((1, 1, tm * TOP_K), lambda i: (i, 0, 0), memory_space=pltpu.SMEM),
                  pl.BlockSpec((tm, D_MODEL), lambda i: (i, 0)),
                  pl.BlockSpec((tm, LANE), lambda i: (i, 0)),
                  pl.BlockSpec((1, D_MODEL), lambda i: (0, 0)),
                  pl.BlockSpec(memory_space=pl.ANY)],
        out_specs=pl.BlockSpec((tm, D_MODEL), lambda i: (i, 0)),
        scratch_shapes=[pltpu.VMEM((TOP_K, tm, D_MODEL), F32), pltpu.SemaphoreType.DMA(())],
        compiler_params=_cp(("arbitrary",)),
        name="moe_combine",
    )(slots.reshape(nt, 1, tm * TOP_K), h, gate, norm_g.reshape(1, D_MODEL), y)


def _moe_plan(ids_p, ids_s):
    n_p, n_s = ids_p.shape[0], ids_s.shape[0]
    pad = (-n_s) % 256
    ids = jnp.concatenate([ids_p, ids_s, jnp.full((pad, LANE), -1, jnp.int32)], axis=0)
    rank, cnt = _expert_ranks(ids)
    counts = cnt[0, :N_EXPERTS]
    padded = (counts + MOE_BLK - 1) // MOE_BLK * MOE_BLK
    pad_end = jnp.cumsum(padded)
    pad_start = pad_end - padded
    n_pairs = (n_p + n_s) * TOP_K
    n_blocks = -(-n_pairs // MOE_BLK) + N_EXPERTS
    blk_start = jnp.arange(n_blocks, dtype=jnp.int32) * MOE_BLK
    blk_e = jnp.minimum(jnp.sum((pad_end[None, :] <= blk_start[:, None]).astype(jnp.int32), axis=1),
                        N_EXPERTS - 1)
    n_used = (pad_end[-1:] // MOE_BLK).astype(jnp.int32)
    e = ids[:n_p + n_s, :TOP_K]
    slot = (pad_start[e] + rank[:n_p + n_s, :TOP_K]).astype(jnp.int32)
    return slot[:n_p], slot[n_p:], blk_e, n_used, n_blocks


def _moe(hn_p, hn_s, h_p, h_s, ids_p, gate_p, ids_s, gate_s, w_gu, b_gu, w_down, b_down, norm_final):
    slot_p, slot_s, blk_e, n_used, n_blocks = _moe_plan(ids_p, ids_s)
    xs = jnp.zeros((n_blocks * MOE_BLK, D_MODEL), F32)
    xs = _scatter_rows(hn_p, slot_p, xs)
    xs = _scatter_rows(hn_s, slot_s, xs)
    hid = _moe_up(xs, blk_e, n_used, w_gu, b_gu)
    y = _moe_down(hid, blk_e, n_used, w_down, b_down)
    return (_combine(h_p, gate_p, slot_p, y, norm_final),
            _combine(h_s, gate_s, slot_s, y, norm_final))


def _proj_sample_kernel(x_ref, g_ref, w_ref, o_ref):
    x = x_ref[...]
    xn = x * lax.rsqrt(jnp.mean(x * x, axis=-1, keepdims=True) + EPS) * g_ref[...]
    o_ref[...] = lax.dot_general(xn, w_ref[...], (((1,), (1,)), ((), ())), precision=HIGHEST,
                                 preferred_element_type=F32)


def _proj_sample(x, norm_g, w_t):
    n = x.shape[0]
    tn = 1024
    return pl.pallas_call(
        _proj_sample_kernel,
        out_shape=jax.ShapeDtypeStruct((n, N_IN), F32),
        grid=(pl.cdiv(N_IN, tn),),
        in_specs=[pl.BlockSpec((n, D_MODEL), lambda j: (0, 0)),
                  pl.BlockSpec((1, D_MODEL), lambda j: (0, 0)),
                  pl.BlockSpec((tn, D_MODEL), lambda j: (j, 0))],
        out_specs=pl.BlockSpec((n, tn), lambda j: (0, j)),
        compiler_params=_cp(("parallel",)),
        name="proj_sample",
    )(x, norm_g.reshape(1, D_MODEL), w_t)


def _gdn_sample_kernel(ext_ref, cw_ref, z_ref, br_ref, ar_ref, al_ref, dt_ref, ng_ref, s_ref,
                       o_ref, so_ref):
    H = GDN_HEADS
    y = ext_ref[0, 0] * cw_ref[0]
    for w in range(1, CONV_W):
        y = y + ext_ref[0, w] * cw_ref[w]
    y = _silu(y)
    q = _l2norm(y[0:H]) * (GDN_DK ** -0.5)
    k = _l2norm(y[H:2 * H])
    v = y[2 * H:3 * H]
    beta = _sigmoid(br_ref[0])
    eg = jnp.exp(-jnp.exp(al_ref[...]) * _softplus(ar_ref[0] + dt_ref[...]))
    qk = jnp.sum(q * k, axis=-1, keepdims=True)
    qT = q.T
    kT = k.T
    rows = []
    for h in range(H):
        S = s_ref[0, h]
        kc = kT[:, h:h + 1]
        qc = qT[:, h:h + 1]
        e = eg[h:h + 1]
        k_s = jnp.sum(S * kc, axis=0, keepdims=True)
        q_s = jnp.sum(S * qc, axis=0, keepdims=True)
        v_new = beta[h:h + 1] * (v[h:h + 1] - e * k_s)
        rows.append(e * q_s + qk[h:h + 1] * v_new)
        so_ref[0, h] = e * S + kc * v_new
    o = jnp.concatenate(rows, axis=0)
    on = o * lax.rsqrt(jnp.mean(o * o, axis=-1, keepdims=True) + EPS) * ng_ref[...]
    o_ref[0] = on * _silu(z_ref[0])


def _gdn_sample(ext, conv_w, z, b_raw, a_raw, a_log, dt_bias, norm_g, state):
    bsz = ext.shape[0]
    H = GDN_HEADS
    col = lambda a: a.reshape(bsz, H, 1)
    hcol = lambda a: a.astype(F32).reshape(H, 1)
    return pl.pallas_call(
        _gdn_sample_kernel,
        out_shape=(jax.ShapeDtypeStruct((bsz, H, GDN_DV), F32),
                   jax.ShapeDtypeStruct(state.shape, F32)),
        grid=(bsz,),
        in_specs=[pl.BlockSpec((1, CONV_W, 3 * H, LANE), lambda b: (b, 0, 0, 0)),
                  pl.BlockSpec((CONV_W, 3 * H, LANE), lambda b: (0, 0, 0)),
                  pl.BlockSpec((1, H, GDN_DV), lambda b: (b, 0, 0)),
                  pl.BlockSpec((1, H, 1), lambda b: (b, 0, 0)),
                  pl.BlockSpec((1, H, 1), lambda b: (b, 0, 0)),
                  pl.BlockSpec((H, 1), lambda b: (0, 0)),
                  pl.BlockSpec((H, 1), lambda b: (0, 0)),
                  pl.BlockSpec((1, GDN_DV), lambda b: (0, 0)),
                  pl.BlockSpec((1, H, GDN_DK, GDN_DV), lambda b: (b, 0, 0, 0))],
        out_specs=(pl.BlockSpec((1, H, GDN_DV), lambda b: (b, 0, 0)),
                   pl.BlockSpec((1, H, GDN_DK, GDN_DV), lambda b: (b, 0, 0, 0))),
        compiler_params=_cp(("parallel",)),
        name="gdn_sample",
    )(ext, conv_w.reshape(CONV_W, 3 * H, LANE), z.reshape(bsz, H, GDN_DV), col(b_raw), col(a_raw),
      hcol(a_log), hcol(dt_bias), norm_g.reshape(1, GDN_DV), state)


PAGES_PER_STEP = 8


def _page_sums_kernel(pt_ref, *refs):
    del pt_ref
    x_refs, w_ref, o_ref = refs[:-2], refs[-2], refs[-1]
    nch = PAGE_SIZE // CMP_STRIDE
    G = NSA_KV_HEADS
    wb = [jnp.broadcast_to(w_ref[k][None, :, None, :], (1, CMP_STRIDE, G, HEAD_DIM)) for k in range(4)]
    for i, x_ref in enumerate(x_refs):
        for e in range(2):
            x4 = x_ref[0, 0, :, e].reshape(nch, CMP_STRIDE, G, HEAD_DIM)
            for half in range(2):
                o_ref[0, 2 * e + half, i * nch:(i + 1) * nch] = jnp.sum(x4 * wb[2 * e + half], axis=1)


def _page_sums(kv_cache, layer, page_table, w4):
    bsz, n_pages = page_table.shape
    G = NSA_KV_HEADS
    nch = PAGE_SIZE // CMP_STRIDE
    pps = math.gcd(PAGES_PER_STEP, n_pages)

    def page(i):
        return pl.BlockSpec((1, 1, PAGE_SIZE, 2, G, HEAD_DIM),
                            lambda b, p, pt: (layer, pt[b * n_pages + p * pps + i], 0, 0, 0, 0))

    grid_spec = pltpu.PrefetchScalarGridSpec(
        num_scalar_prefetch=1,
        grid=(bsz, n_pages // pps),
        in_specs=[page(i) for i in range(pps)]
        + [pl.BlockSpec((4, CMP_STRIDE, HEAD_DIM), lambda b, p, pt: (0, 0, 0))],
        out_specs=pl.BlockSpec((1, 4, pps * nch, G, HEAD_DIM), lambda b, p, pt: (b, 0, p, 0, 0)))
    return pl.pallas_call(
        _page_sums_kernel,
        out_shape=jax.ShapeDtypeStruct((bsz, 4, n_pages * nch, G, HEAD_DIM), F32),
        grid_spec=grid_spec,
        compiler_params=_cp(("parallel", "arbitrary")),
        name="nsa_page_sums",
    )(page_table.reshape(-1), *([kv_cache] * pps), w4)


def _row_to_col(r):
    n = r.shape[1]
    eye = lax.broadcasted_iota(jnp.int32, (n, n), 0) == lax.broadcasted_iota(jnp.int32, (n, n), 1)
    return jnp.sum(jnp.where(eye, jnp.broadcast_to(r, (n, n)), 0), axis=1, keepdims=True)


def _hdot(a, b):
    return jnp.dot(a, b, precision=HIGHEST, preferred_element_type=F32)


def _hdot_nt(a, b):
    return lax.dot_general(a, b, (((1,), (1,)), ((), ())), precision=HIGHEST,
                           preferred_element_type=F32)


def _nsa_sample_select_kernel(sums_ref, new_ref, w_ref, proj_ref, q_ref, ocmp_ref, idx_ref, *, past):
    G = NSA_KV_HEADS
    nch = sums_ref.shape[2]
    row = lax.broadcasted_iota(jnp.int32, (nch, HEAD_DIM), 0)

    def pooled(e, g):
        new_s1 = new_ref[0, e, g:g + 1, :] * w_ref[2 * e + 1, 0:1]
        s1 = sums_ref[0, 2 * e + 1, :, g, :]
        nxt = jnp.where(row < nch - 1, pltpu.roll(s1, nch - 1, 0), new_s1)
        return sums_ref[0, 2 * e, :, g, :] + nxt

    n_slc = past // SEL_BLOCK + 1
    nsp = -(-n_slc // LANE) * LANE
    t = past
    cmp_end = CMP_STRIDE * lax.broadcasted_iota(jnp.int32, (1, nch), 1) + (CMP_LEN - 1)
    ci = CMP_STRIDE * lax.broadcasted_iota(jnp.int32, (nch, nsp), 0)
    bj = SEL_BLOCK * lax.broadcasted_iota(jnp.int32, (nch, nsp), 1)
    cmap = ((ci < bj + SEL_BLOCK) & (ci + CMP_LEN > bj)).astype(F32)
    j_idx = lax.broadcasted_iota(jnp.int32, (1, nsp), 1)
    jt = t // SEL_BLOCK
    force = (j_idx == 0) | (j_idx == jt) | (j_idx == jt - 1)
    i_idx = lax.broadcasted_iota(jnp.int32, (nsp, nsp), 0)
    jj_idx = lax.broadcasted_iota(jnp.int32, (nsp, nsp), 1)
    lane = lax.broadcasted_iota(jnp.int32, (1, LANE), 1)
    idx_rows = []
    for g in range(G):
        q = q_ref[0, g] * (HEAD_DIM ** -0.5)
        p = _masked_softmax(_hdot_nt(_hdot_nt(q, proj_ref[0]), pooled(0, g)), cmp_end <= t)
        ocmp_ref[0, g] = _hdot(_hdot(p, pooled(1, g)), proj_ref[1])
        psum = jnp.sum(p[0:NSA_HPG], axis=0, keepdims=True)
        imp = _hdot(psum, cmap)
        score = jnp.where(j_idx > jt, NEG, jnp.where(force, FORCE, imp))
        sc = _row_to_col(score)
        ahead = (sc > score) | ((sc == score) & (i_idx < jj_idx))
        cnt = jnp.sum(ahead.astype(jnp.int32), axis=0, keepdims=True)
        sel = (cnt < N_SEL) & (score > NEG / 2)
        selc = _row_to_col(sel.astype(jnp.int32))
        before = jnp.sum(jnp.where(i_idx < jj_idx, selc, 0), axis=0, keepdims=True)
        out = jnp.full((1, LANE), -1, jnp.int32)
        for k in range(N_SEL):
            hit = sel & (before == k)
            jk = jnp.sum(jnp.where(hit, j_idx + 1, 0), axis=1, keepdims=True) - 1
            out = jnp.where(lane == k, jk, out)
        idx_rows.append(out)
    idx_ref[0] = jnp.concatenate(idx_rows + [jnp.full((8 - G, LANE), -1, jnp.int32)], axis=0)


def _nsa_sample_select(sums, new6, w4, cmp_proj, q8, past):
    bsz = sums.shape[0]
    G = NSA_KV_HEADS
    return pl.pallas_call(
        functools.partial(_nsa_sample_select_kernel, past=past),
        out_shape=(jax.ShapeDtypeStruct((bsz, G, 8, HEAD_DIM), F32),
                   jax.ShapeDtypeStruct((bsz, 8, LANE), jnp.int32)),
        grid=(bsz,),
        in_specs=[pl.BlockSpec((1,) + sums.shape[1:], lambda b: (b, 0, 0, 0, 0)),
                  pl.BlockSpec((1, 6, G, HEAD_DIM), lambda b: (b, 0, 0, 0)),
                  pl.BlockSpec((4, CMP_STRIDE, HEAD_DIM), lambda b: (0, 0, 0)),
                  pl.BlockSpec((2, HEAD_DIM, HEAD_DIM), lambda b: (0, 0, 0)),
                  pl.BlockSpec((1, G, 8, HEAD_DIM), lambda b: (b, 0, 0, 0))],
        out_specs=(pl.BlockSpec((1, G, 8, HEAD_DIM), lambda b: (b, 0, 0, 0)),
                   pl.BlockSpec((1, 8, LANE), lambda b: (b, 0, 0))),
        compiler_params=_cp(("parallel",)),
        name="nsa_sample_select",
    )(sums, new6, w4, cmp_proj, q8)


def _nsa_sample_attend_kernel(idx_ref, pt_ref, kv_hbm, win_hbm, new_ref, q_ref, ocmp_ref, gate_ref,
                              o_ref, kv_buf, win_buf, sem, *, past, n_pages, layer):
    b = pl.program_id(0)
    g = pl.program_id(1)
    G = NSA_KV_HEADS
    t = past
    n_past_blk = past // SEL_BLOCK
    bpp = PAGE_SIZE // SEL_BLOCK
    step = b * G + g
    n_steps = pl.num_programs(0) * G

    def copies(st, slot):
        bb = st // G
        gg = lax.rem(st, G)
        out = []
        for k in range(N_SEL):
            j = jnp.clip(idx_ref[st * N_SEL + k], 0, n_past_blk - 1)
            page = pt_ref[bb * n_pages + j // bpp]
            r0 = pl.multiple_of(lax.rem(j, bpp) * SEL_BLOCK, SEL_BLOCK)
            for kv in range(2):
                out.append(pltpu.make_async_copy(
                    kv_hbm.at[layer, page, pl.ds(r0, SEL_BLOCK), 2 + kv, gg, :],
                    kv_buf.at[slot, kv, pl.ds(k * SEL_BLOCK, SEL_BLOCK), :], sem.at[slot]))
        for kv in range(2):
            out.append(pltpu.make_async_copy(win_hbm.at[layer, bb, :, kv, gg, :],
                                             win_buf.at[slot, kv], sem.at[slot]))
        return out

    slot = lax.rem(step, 2)

    @pl.when(step == 0)
    def _():
        for c in copies(0, 0):
            c.start()

    @pl.when(step + 1 < n_steps)
    def _():
        for c in copies(step + 1, 1 - slot):
            c.start()

    for c in copies(step, slot):
        c.wait()

    q = q_ref[0, 0] * (HEAD_DIM ** -0.5)
    k_new = new_ref[0, 2, pl.ds(g, 1), :]
    v_new = new_ref[0, 3, pl.ds(g, 1), :]
    kw_new = new_ref[0, 4, pl.ds(g, 1), :]
    vw_new = new_ref[0, 5, pl.ds(g, 1), :]

    r64 = lax.broadcasted_iota(jnp.int32, (SEL_BLOCK, HEAD_DIM), 0)
    ks, vs, kpos = [], [], []
    r_lane = lax.broadcasted_iota(jnp.int32, (1, SEL_BLOCK), 1)
    for k in range(N_SEL):
        j = idx_ref[(b * G + g) * N_SEL + k]
        is_new = j >= n_past_blk
        rows = pl.ds(k * SEL_BLOCK, SEL_BLOCK)
        ks.append(jnp.where(is_new, jnp.where(r64 == 0, k_new, 0.0), kv_buf[slot, 0, rows, :]))
        vs.append(jnp.where(is_new, jnp.where(r64 == 0, v_new, 0.0), kv_buf[slot, 1, rows, :]))
        kpos.append(jnp.where(j >= 0, j * SEL_BLOCK + r_lane, t + 1))
    ksel = jnp.concatenate(ks, axis=0)
    vsel = jnp.concatenate(vs, axis=0)
    kpos = jnp.concatenate(kpos, axis=1)
    p = _masked_softmax(_hdot_nt(q, ksel), kpos <= t)
    o_slc = _hdot(p, vsel)

    L = win_buf.shape[2]
    wpos = (past - L) + lax.broadcasted_iota(jnp.int32, (1, L), 1)
    ok = (wpos > t - WINDOW) & (wpos >= 0)
    s_w = jnp.where(ok, _hdot_nt(q, win_buf[slot, 0]), NEG)
    s_n = jnp.sum(q * kw_new, axis=-1, keepdims=True)
    m = jnp.maximum(jnp.max(s_w, axis=-1, keepdims=True), s_n)
    p_w = jnp.exp(s_w - m) * ok.astype(F32)
    p_n = jnp.exp(s_n - m)
    den = jnp.maximum(jnp.sum(p_w, axis=-1, keepdims=True) + p_n, 1e-30)
    o_win = (_hdot(p_w, win_buf[slot, 1]) + p_n * vw_new) / den

    gb = _sigmoid(gate_ref[0, 0])
    o_ref[0, 0] = ocmp_ref[0, 0] * gb[:, 0:1] + o_slc * gb[:, 1:2] + o_win * gb[:, 2:3]


def _nsa_sample_attend(idx, page_table, kv_cache, win_cache, layer, new6, q8, o_cmp, gate8, past):
    bsz, n_pages = page_table.shape
    G = NSA_KV_HEADS
    L = win_cache.shape[2]
    per_bg = pl.BlockSpec((1, 1, 8, HEAD_DIM), lambda b, g, i, p: (b, g, 0, 0))
    grid_spec = pltpu.PrefetchScalarGridSpec(
        num_scalar_prefetch=2,
        grid=(bsz, G),
        in_specs=[pl.BlockSpec(memory_space=pl.ANY), pl.BlockSpec(memory_space=pl.ANY),
                  pl.BlockSpec((1, 6, G, HEAD_DIM), lambda b, g, i, p: (b, 0, 0, 0)),
                  per_bg, per_bg, per_bg],
        out_specs=per_bg,
        scratch_shapes=[pltpu.VMEM((2, 2, N_SEL * SEL_BLOCK, HEAD_DIM), F32),
                        pltpu.VMEM((2, 2, L, HEAD_DIM), F32), pltpu.SemaphoreType.DMA((2,))])
    return pl.pallas_call(
        functools.partial(_nsa_sample_attend_kernel, past=past, n_pages=n_pages, layer=layer),
        out_shape=jax.ShapeDtypeStruct((bsz, G, 8, HEAD_DIM), F32),
        grid_spec=grid_spec,
        compiler_params=_cp(("arbitrary", "arbitrary")),
        name="nsa_sample_attend",
    )(idx, page_table.reshape(-1), kv_cache, win_cache, new6, q8, o_cmp, gate8)


def _prompt_mixers(x2d, batch, seq, norm_mix, w_in, conv_w, a_log, dt_bias, gdn_norm, cmp_pos,
                   cmp_proj, w_out_a, w_out_b):
    w_main, w_small = _repack_w_in(w_in)
    proj, small = _proj_prompt(x2d, norm_mix, w_main, w_small)
    o_a, s_new = _gdn_prompt(proj, small, conv_w, _head_params(a_log, dt_bias), gdn_norm, batch, seq)
    comp = _compress_prompt(proj, cmp_pos, cmp_proj, batch, seq)
    o_b = _nsa_prompt(proj, small, comp, batch, seq)
    y = _merge(o_a, o_b, w_out_a.astype(BF16), w_out_b.astype(BF16), proj, CB_MERGE,
               CB_MERGE + D_MODEL // LANE, False)
    G = NSA_KV_HEADS
    kv6 = proj[:, CB_KVB * LANE:].reshape(batch, seq, 6, G, HEAD_DIM)
    conv_new = proj.reshape(batch, seq, N_MAIN)[:, seq - (CONV_W - 1):, :CONV_DIM]
    return y, kv6[:, :, :4], kv6[:, max(seq - WINDOW, 0):, 4:], s_new, conv_new


def _sample_mixers(x2d, norm_mix, w_in, conv_w, a_log, dt_bias, gdn_norm, cmp_pos, cmp_proj,
                   w_out_a, w_out_b, kv_cache, win_cache, layer, state, conv_buf, page_table):
    bsz = x2d.shape[0]
    win_buf = win_cache[layer]
    G = NSA_KV_HEADS
    past = page_table.shape[1] * PAGE_SIZE
    ps = _proj_sample(x2d, norm_mix, w_in)
    qkv, z = ps[:, OFF_QKV:OFF_Z], ps[:, OFF_Z:OFF_BRAW]
    b_raw, a_raw = ps[:, OFF_BRAW:OFF_ARAW], ps[:, OFF_ARAW:OFF_QB]
    q_b, kv_b = ps[:, OFF_QB:OFF_KVB], ps[:, OFF_KVB:OFF_GATEB]
    gate_b, merge_raw = ps[:, OFF_GATEB:OFF_MERGE], ps[:, OFF_MERGE:]

    ext = jnp.concatenate([conv_buf.astype(F32), qkv[:, None]], axis=1)
    o_a, s_new = _gdn_sample(ext.reshape(bsz, CONV_W, 3 * GDN_HEADS, LANE), conv_w, z, b_raw, a_raw,
                             a_log, dt_bias, gdn_norm, state.astype(F32))

    w4 = cmp_pos.astype(F32).reshape(4, CMP_STRIDE, HEAD_DIM)
    sums = _page_sums(kv_cache, layer, page_table, w4)
    new6 = kv_b.reshape(bsz, 6, G, HEAD_DIM)
    q8 = jnp.pad(q_b.reshape(bsz, G, NSA_HPG, HEAD_DIM), ((0, 0), (0, 0), (0, 8 - NSA_HPG), (0, 0)))
    o_cmp, idx8 = _nsa_sample_select(sums, new6, w4, cmp_proj.astype(F32), q8, past)
    gate8 = jnp.pad(gate_b.reshape(bsz, G, NSA_HPG, 3),
                    ((0, 0), (0, 0), (0, 8 - NSA_HPG), (0, LANE - 3)))
    o_b8 = _nsa_sample_attend(idx8[:, :G, :N_SEL].reshape(-1), page_table, kv_cache, win_cache,
                              layer, new6, q8, o_cmp, gate8, past)
    o_b = o_b8[:, :, :NSA_HPG].reshape(bsz, NSA_HEADS * HEAD_DIM)

    y = _merge(o_a.reshape(bsz, GDN_HEADS * GDN_DV), o_b, w_out_a, w_out_b, merge_raw, 0,
               D_MODEL // LANE, True)
    kv6 = kv_b.reshape(bsz, 1, 6, G, HEAD_DIM)
    win_ext = jnp.concatenate([win_buf.astype(F32), kv6[:, :, 4:]], axis=1)
    keep = min(WINDOW, past + 1)
    return y, kv6[:, :, :4], win_ext[:, win_ext.shape[1] - keep:], s_new, ext[:, 1:]


def kernel(x_prompt, x_sample, cache_nsa_kv, cache_nsa_win, state_gdn, state_conv, page_table, norm_mix, w_in, conv_w, gdn_a_log, gdn_dt_bias, gdn_norm, w_out_a, cmp_pos, cmp_proj, w_out_b, w_out, norm_ffn, w_router, b_router, w_gu, b_gu, w_down, b_down, norm_final):
    assert w_in.shape[0] == 1 and x_sample.shape[1] == 1, "one layer, one new token per sequence"
    bp, seq, _ = x_prompt.shape
    bs = x_sample.shape[0]
    xp = x_prompt.reshape(bp * seq, D_MODEL)
    xs = x_sample.reshape(bs, D_MODEL)
    l = 0
    w_t = jnp.swapaxes(w_in[l], 0, 1)
    y_p, kv_p, win_p, gdn_p, conv_p = _prompt_mixers(
        xp, bp, seq, norm_mix[l], w_t, conv_w[l], gdn_a_log[l], gdn_dt_bias[l], gdn_norm[l],
        cmp_pos[l], cmp_proj[l], w_out_a[l], w_out_b[l])
    y_s, kv_s, win_s, gdn_s, conv_s = _sample_mixers(
        xs, norm_mix[l], w_t, conv_w[l], gdn_a_log[l], gdn_dt_bias[l], gdn_norm[l], cmp_pos[l],
        cmp_proj[l], w_out_a[l], w_out_b[l], cache_nsa_kv, cache_nsa_win, l, state_gdn[l],
        state_conv[l], page_table)

    wr = jnp.pad(w_router[l].astype(F32), ((0, 0), (0, LANE - N_EXPERTS)))
    br = jnp.pad(b_router[l].astype(F32), (0, LANE - N_EXPERTS)).reshape(1, LANE)
    h_p, hn_p, ids_p, gate_p = _outproj_router(xp, y_p, w_out[l].astype(BF16), norm_ffn[l], wr, br, False)
    h_s, hn_s, ids_s, gate_s = _outproj_router(xs, y_s, w_out[l], norm_ffn[l], wr, br, True)
    out_p, out_s = _moe(hn_p, hn_s, h_p, h_s, ids_p, gate_p, ids_s, gate_s, w_gu[l], b_gu[l],
                        w_down[l], b_down[l], norm_final)
    return (out_p.reshape(x_prompt.shape), out_s.reshape(x_sample.shape),
            kv_p[None], kv_s[None], win_p[None], win_s[None],
            gdn_p[None].astype(state_gdn.dtype), gdn_s[None].astype(state_gdn.dtype),
            conv_p[None], conv_s[None])
```

```python
import functools
import math

import jax
import jax.numpy as jnp
from jax import lax
from jax.experimental import pallas as pl
from jax.experimental.pallas import tpu as pltpu

F32 = jnp.float32
BF16 = jnp.bfloat16
HIGHEST = lax.Precision.HIGHEST

D_MODEL = 2048
GDN_HEADS = 16
GDN_DK = 128
GDN_DV = 128
CONV_W = 4
CONV_DIM = GDN_HEADS * (2 * GDN_DK + GDN_DV)
NSA_HEADS = 16
NSA_KV_HEADS = 4
HEAD_DIM = 128
NSA_HPG = NSA_HEADS // NSA_KV_HEADS
CMP_LEN = 32
CMP_STRIDE = 16
SEL_BLOCK = 64
N_SEL = 16
WINDOW = 512
PAGE_SIZE = 128
N_EXPERTS = 32
TOP_K = 4
D_FF = 2048
SWIGLU_LIMIT = 7.0
SWIGLU_ALPHA = 1.702
EPS = 1e-6
NEG = -1e30
FORCE = 1e6

IN_SPLITS = (CONV_DIM, GDN_HEADS * GDN_DV, GDN_HEADS, GDN_HEADS, NSA_HEADS * HEAD_DIM,
             6 * NSA_KV_HEADS * HEAD_DIM, 3 * NSA_HEADS, 2 * D_MODEL)
_OFF = [0]
for _s in IN_SPLITS:
    _OFF.append(_OFF[-1] + _s)
(OFF_QKV, OFF_Z, OFF_BRAW, OFF_ARAW, OFF_QB, OFF_KVB, OFF_GATEB, OFF_MERGE, N_IN) = _OFF

LANE = 128
CB_Q, CB_K, CB_V = 0, GDN_HEADS, 2 * GDN_HEADS
CB_Z = CONV_DIM // LANE
CB_MERGE = CB_Z + GDN_HEADS
CB_QB = CB_MERGE + 2 * D_MODEL // LANE
CB_KVB = CB_QB + NSA_HEADS
N_MAIN = (CB_KVB + 6 * NSA_KV_HEADS) * LANE
SM_BRAW, SM_ARAW, SM_GATE = 0, GDN_HEADS, 2 * GDN_HEADS

GDN_CHUNK = 128
GDN_HB = 8
NSA_TQ = 256
NSA_KT = 1024
MOE_BLK = 256
VMEM_LIMIT = 60000 * 1024


def _cp(sem, vmem=VMEM_LIMIT):
    return pltpu.CompilerParams(dimension_semantics=sem, vmem_limit_bytes=vmem)


def _lane_pick(x, lane):
    ids = lax.broadcasted_iota(jnp.int32, x.shape, 1)
    return jnp.sum(jnp.where(ids == lane, x, 0.0), axis=1, keepdims=True)


def _sigmoid(x):
    return 1.0 / (1.0 + jnp.exp(-x))


def _silu(x):
    return x * _sigmoid(x)


def _softplus(x):
    return jnp.maximum(x, 0.0) + jnp.log(1.0 + jnp.exp(-jnp.abs(x)))


def _proj_kernel(x_ref, g_ref, w_ref, ws_ref, o_ref, os_ref, xn_ref):
    @pl.when(pl.program_id(1) == 0)
    def _():
        x = x_ref[...]
        ms = jnp.mean(x * x, axis=-1, keepdims=True)
        xn = (x * lax.rsqrt(ms + EPS) * g_ref[...]).astype(BF16)
        xn_ref[...] = xn
        os_ref[...] = jnp.dot(xn, ws_ref[...], preferred_element_type=F32)

    o_ref[...] = jnp.dot(xn_ref[...], w_ref[...], preferred_element_type=F32)


REPACK_TN = 512
_RUNS = ((0, OFF_QKV), (OFF_BRAW - OFF_QKV, OFF_MERGE),
         (OFF_BRAW - OFF_QKV + N_IN - OFF_MERGE, OFF_QB))


def _repack_kernel(wt_hbm, o_ref, os_ref, buf_ref, sbuf_ref, sem, ssem):
    j = pl.program_id(0)
    nj = pl.num_programs(0)
    starts = [r[0] // REPACK_TN for r in _RUNS]

    def rows_copy(jj, slot):
        row = jj * REPACK_TN + (_RUNS[0][1] - _RUNS[0][0])
        for r in range(1, len(_RUNS)):
            row = jnp.where(jj >= starts[r], jj * REPACK_TN + (_RUNS[r][1] - _RUNS[r][0]), row)
        return pltpu.make_async_copy(wt_hbm.at[pl.ds(pl.multiple_of(row, 8), REPACK_TN), :],
                                     buf_ref.at[slot], sem.at[slot])

    def small_copies():
        return [pltpu.make_async_copy(wt_hbm.at[pl.ds(OFF_BRAW, SM_GATE), :],
                                      sbuf_ref.at[pl.ds(0, SM_GATE), :], ssem),
                pltpu.make_async_copy(wt_hbm.at[pl.ds(OFF_GATEB, 3 * NSA_HEADS), :],
                                      sbuf_ref.at[pl.ds(SM_GATE, 3 * NSA_HEADS), :], ssem)]

    @pl.when(j == 0)
    def _():
        rows_copy(0, 0).start()
        sbuf_ref[...] = jnp.zeros_like(sbuf_ref)
        for c in small_copies():
            c.start()
        for c in small_copies():
            c.wait()
        os_ref[...] = sbuf_ref[...].T.astype(BF16)

    slot = lax.rem(j, 2)

    @pl.when(j + 1 < nj)
    def _():
        rows_copy(j + 1, 1 - slot).start()

    rows_copy(j, slot).wait()
    o_ref[...] = buf_ref[slot].T.astype(BF16)


def _repack_w_in(w_t):
    assert all(r[0] % REPACK_TN == 0 and (r[1] - r[0]) % 8 == 0 for r in _RUNS)
    assert N_MAIN % REPACK_TN == 0 and OFF_QB - OFF_BRAW == SM_GATE
    return pl.pallas_call(
        _repack_kernel,
        out_shape=(jax.ShapeDtypeStruct((D_MODEL, N_MAIN), BF16),
                   jax.ShapeDtypeStruct((D_MODEL, LANE), BF16)),
        grid=(N_MAIN // REPACK_TN,),
        in_specs=[pl.BlockSpec(memory_space=pl.ANY)],
        out_specs=(pl.BlockSpec((D_MODEL, REPACK_TN), lambda j: (0, j)),
                   pl.BlockSpec((D_MODEL, LANE), lambda j: (0, 0))),
        scratch_shapes=[pltpu.VMEM((2, REPACK_TN, D_MODEL), F32), pltpu.VMEM((LANE, D_MODEL), F32),
                        pltpu.SemaphoreType.DMA((2,)), pltpu.SemaphoreType.DMA(())],
        compiler_params=_cp(("arbitrary",)),
        name="repack_w_in",
    )(w_t)


def _proj_prompt(x, norm_g, w_main, w_small):
    n = x.shape[0]
    tm = min(1024, n)
    tn = 1024
    return pl.pallas_call(
        _proj_kernel,
        out_shape=(jax.ShapeDtypeStruct((n, N_MAIN), F32), jax.ShapeDtypeStruct((n, LANE), F32)),
        grid=(n // tm, N_MAIN // tn),
        in_specs=[pl.BlockSpec((tm, D_MODEL), lambda i, j: (i, 0)),
                  pl.BlockSpec((1, D_MODEL), lambda i, j: (0, 0)),
                  pl.BlockSpec((D_MODEL, tn), lambda i, j: (0, j)),
                  pl.BlockSpec((D_MODEL, LANE), lambda i, j: (0, 0))],
        out_specs=(pl.BlockSpec((tm, tn), lambda i, j: (i, j)),
                   pl.BlockSpec((tm, LANE), lambda i, j: (i, 0))),
        scratch_shapes=[pltpu.VMEM((tm, D_MODEL), BF16)],
        compiler_params=_cp(("parallel", "arbitrary")),
        name="proj_prompt",
    )(x, norm_g.reshape(1, D_MODEL), w_main, w_small)


def _shift_rows(x, prev8, s):
    xs = pltpu.roll(x, s, 0)
    ps = pltpu.roll(prev8, s, 0)
    row = lax.broadcasted_iota(jnp.int32, prev8.shape, 0)
    top = jnp.where(row < s, ps, xs[0:8])
    return jnp.concatenate([top, xs[8:]], axis=0)


def _conv_silu(x, prev8, cw):
    y = x * cw[CONV_W - 1:CONV_W]
    for s in range(1, CONV_W):
        y = y + _shift_rows(x, prev8, s) * cw[CONV_W - 1 - s:CONV_W - s]
    return _silu(y)


def _l2norm(x):
    return x * lax.rsqrt(jnp.sum(x * x, axis=-1, keepdims=True) + EPS)


def _cumsum_rows(x):
    c = x.shape[0]
    row = lax.broadcasted_iota(jnp.int32, x.shape, 0)
    s = 1
    while s < c:
        x = x + jnp.where(row >= s, pltpu.roll(x, s, 0), 0.0)
        s *= 2
    return x


def _bdot(a, b):
    return jnp.dot(a.astype(BF16), b.astype(BF16), preferred_element_type=F32)


def _bdot_nt(a, b):
    return lax.dot_general(a.astype(BF16), b.astype(BF16), (((1,), (1,)), ((), ())),
                           preferred_element_type=F32)


def _bdot_tn(a, b):
    return lax.dot_general(a.astype(BF16), b.astype(BF16), (((0,), (0,)), ((), ())),
                           preferred_element_type=F32)


INV_BASE = 16


def _unit_lower_inverses(mats, ri, ci):
    c = mats[0].shape[0]
    same = lambda s: (ri // s) == (ci // s)
    eye = jnp.where(ri == ci, 1.0, 0.0)
    blk = same(INV_BASE)
    ps = [jnp.where(blk, a, 0.0) for a in mats]
    ts = [eye - p for p in ps]
    n = 2
    while n < INV_BASE:
        ps = [_bdot(p, p) for p in ps]
        ts = [t + _bdot(t, p) for t, p in zip(ts, ps)]
        n *= 2
    s = INV_BASE
    while s < c:
        off = same(2 * s) & jnp.logical_not(same(s))
        tm = [_bdot(t, jnp.where(off, a, 0.0)) for t, a in zip(ts, mats)]
        ts = [t - _bdot(x, t) for t, x in zip(ts, tm)]
        s *= 2
    return ts


def _gdn_kernel(q_ref, k_ref, v_ref, z_ref, sm_ref, cwq_ref, cwk_ref, cwv_ref, hp_ref, ng_ref,
                o_ref, s_out_ref, s_ref, pq_ref, pk_ref, pv_ref):
    hg = pl.program_id(1)
    c = pl.program_id(2)
    C = GDN_CHUNK

    @pl.when(c == 0)
    def _():
        s_ref[...] = jnp.zeros_like(s_ref)
        pq_ref[...] = jnp.zeros_like(pq_ref)
        pk_ref[...] = jnp.zeros_like(pk_ref)
        pv_ref[...] = jnp.zeros_like(pv_ref)

    xq, xk, xv = q_ref[...], k_ref[...], v_ref[...]
    yq = _conv_silu(xq, pq_ref[...], cwq_ref[...])
    yk = _conv_silu(xk, pk_ref[...], cwk_ref[...])
    yv = _conv_silu(xv, pv_ref[...], cwv_ref[...])
    pq_ref[...] = xq[C - 8:]
    pk_ref[...] = xk[C - 8:]
    pv_ref[...] = xv[C - 8:]

    sm = sm_ref[...]
    hp = hp_ref[...]
    g_all = -jnp.exp(hp[0:1]) * _softplus(sm + hp[1:2])
    G_all = _cumsum_rows(g_all)
    ri = lax.broadcasted_iota(jnp.int32, (C, C), 0)
    ci = lax.broadcasted_iota(jnp.int32, (C, C), 1)
    z = _silu(z_ref[...])
    HS = range(GDN_HB)
    cols = [slice(i * LANE, (i + 1) * LANE) for i in HS]
    q = [_l2norm(yq[:, cols[i]]) * (GDN_DK ** -0.5) for i in HS]
    k = [_l2norm(yk[:, cols[i]]) for i in HS]
    v = [yv[:, cols[i]] for i in HS]
    beta = [_sigmoid(_lane_pick(sm, SM_BRAW + hg * GDN_HB + i)) for i in HS]
    G = [_lane_pick(G_all, SM_ARAW + hg * GDN_HB + i) for i in HS]
    eG = [jnp.exp(G[i]) for i in HS]
    g_last = [G[i][C - 1:C] for i in HS]
    Gb = [jnp.broadcast_to(G[i], (C, C)) for i in HS]
    decay = [jnp.exp(jnp.where(ri >= ci, Gb[i] - Gb[i].T, NEG)) for i in HS]
    kb = [k[i] * beta[i] for i in HS]
    kk = [_bdot_nt(kb[i], k[i]) for i in HS]
    qk = [_bdot_nt(q[i], k[i]) for i in HS]
    a_mat = [jnp.where(ri > ci, kk[i] * decay[i], 0.0) for i in HS]
    t_inv = _unit_lower_inverses(a_mat, ri, ci)
    rhs = [jnp.concatenate([v[i] * beta[i], kb[i] * eG[i]], axis=1) for i in HS]
    x = [_bdot(t_inv[i], rhs[i]) for i in HS]
    S = [s_ref[i] for i in HS]
    q_s = [_bdot(q[i] * eG[i], S[i]) for i in HS]
    w_s = [_bdot(x[i][:, GDN_DV:], S[i]) for i in HS]
    v_new = [x[i][:, :GDN_DV] - w_s[i] for i in HS]
    o = [q_s[i] + _bdot(qk[i] * decay[i], v_new[i]) for i in HS]
    kv = [_bdot_tn(k[i] * jnp.exp(g_last[i] - G[i]), v_new[i]) for i in HS]
    s_new = [S[i] * jnp.exp(g_last[i]) + kv[i] for i in HS]
    for i in HS:
        s_ref[i] = s_new[i]
        on = o[i] * lax.rsqrt(jnp.mean(o[i] * o[i], axis=-1, keepdims=True) + EPS) * ng_ref[...]
        o_ref[:, cols[i]] = (on * z[:, cols[i]]).astype(o_ref.dtype)

    @pl.when(c == pl.num_programs(2) - 1)
    def _():
        for i in HS:
            s_out_ref[0, i] = s_new[i]


def _head_params(a_log, dt_bias):
    hp = jnp.zeros((8, LANE), F32)
    hp = hp.at[0, SM_ARAW:SM_ARAW + GDN_HEADS].set(a_log.astype(F32))
    return hp.at[1, SM_ARAW:SM_ARAW + GDN_HEADS].set(dt_bias.astype(F32))


def _gdn_prompt(proj, small, conv_w, head_par, norm_g, batch, seq):
    C = GDN_CHUNK
    nc = seq // C
    H = GDN_HEADS
    HB = GDN_HB
    W = HB * LANE

    def rows(cb):
        return pl.BlockSpec((C, W), lambda b, h, c, cb=cb: (b * nc + c, cb // HB + h))

    def cw(cb):
        return pl.BlockSpec((CONV_W, W), lambda b, h, c, cb=cb: (0, cb // HB + h))

    return pl.pallas_call(
        _gdn_kernel,
        out_shape=(jax.ShapeDtypeStruct((batch * seq, H * GDN_DV), BF16),
                   jax.ShapeDtypeStruct((batch, H, GDN_DK, GDN_DV), F32)),
        grid=(batch, H // HB, nc),
        in_specs=[rows(CB_Q), rows(CB_K), rows(CB_V), rows(CB_Z),
                  pl.BlockSpec((C, LANE), lambda b, h, c: (b * nc + c, 0)),
                  cw(CB_Q), cw(CB_K), cw(CB_V),
                  pl.BlockSpec((8, LANE), lambda b, h, c: (0, 0)),
                  pl.BlockSpec((1, GDN_DV), lambda b, h, c: (0, 0))],
        out_specs=(pl.BlockSpec((C, W), lambda b, h, c: (b * nc + c, h)),
                   pl.BlockSpec((1, HB, GDN_DK, GDN_DV), lambda b, h, c: (b, h, 0, 0))),
        scratch_shapes=[pltpu.VMEM((HB, GDN_DK, GDN_DV), F32), pltpu.VMEM((8, W), F32),
                        pltpu.VMEM((8, W), F32), pltpu.VMEM((8, W), F32)],
        compiler_params=_cp(("parallel", "parallel", "arbitrary")),
        name="gdn_prompt",
    )(proj, proj, proj, proj, small, conv_w, conv_w, conv_w, head_par, norm_g.reshape(1, GDN_DV))


def _cmp_kernel(x_ref, w_ref, p_ref, o_ref):
    t = x_ref.shape[0]
    nch = t // CMP_STRIDE
    x3 = x_ref[...].reshape(nch, CMP_STRIDE, HEAD_DIM)
    w = w_ref[0]
    s0 = jnp.sum(x3 * w[None, :CMP_STRIDE], axis=1)
    s1 = jnp.sum(x3 * w[None, CMP_STRIDE:], axis=1)
    row = lax.broadcasted_iota(jnp.int32, (nch, HEAD_DIM), 0)
    pooled = jnp.where(row < nch - 1, s0 + pltpu.roll(s1, nch - 1, 0), 0.0)
    o_ref[0, 0, 0] = jnp.dot(pooled, p_ref[0], precision=HIGHEST, preferred_element_type=F32)


def _compress_prompt(proj, cmp_pos, cmp_proj, batch, seq):
    nch = seq // CMP_STRIDE
    return pl.pallas_call(
        _cmp_kernel,
        out_shape=jax.ShapeDtypeStruct((batch, 2, NSA_KV_HEADS, nch, HEAD_DIM), F32),
        grid=(batch, 2, NSA_KV_HEADS),
        in_specs=[pl.BlockSpec((seq, LANE), lambda b, e, g: (b, CB_KVB + e * NSA_KV_HEADS + g)),
                  pl.BlockSpec((1, CMP_LEN, HEAD_DIM), lambda b, e, g: (e, 0, 0)),
                  pl.BlockSpec((1, HEAD_DIM, HEAD_DIM), lambda b, e, g: (e, 0, 0))],
        out_specs=pl.BlockSpec((1, 1, 1, nch, HEAD_DIM), lambda b, e, g: (b, e, g, 0, 0)),
        compiler_params=_cp(("parallel", "parallel", "parallel")),
        name="nsa_compress_prompt",
    )(proj, cmp_pos, cmp_proj)


def _masked_softmax(s, mask):
    s = jnp.where(mask, s, NEG)
    p = jnp.exp(s - jnp.max(s, axis=-1, keepdims=True)) * mask.astype(F32)
    return p / jnp.maximum(jnp.sum(p, axis=-1, keepdims=True), 1e-30)


def _select_blocks_t(score_t):
    nj = score_t.shape[0]
    sub = 8
    groups = [score_t[v:v + sub] for v in range(0, nj, sub)]
    j_in = lax.broadcasted_iota(jnp.int32, groups[0].shape, 0)
    cnts = [jnp.zeros(g.shape, jnp.int32) for g in groups]
    for jp in range(nj):
        row = score_t[jp:jp + 1, :]
        for v, g in enumerate(groups):
            if v * sub + sub - 1 <= jp:
                ahead = jnp.where(row > g, 1, 0)
            elif v * sub > jp:
                ahead = jnp.where(row >= g, 1, 0)
            else:
                ahead = jnp.where(j_in + v * sub > jp, jnp.where(row >= g, 1, 0),
                                  jnp.where(row > g, 1, 0))
            cnts[v] = cnts[v] + ahead
    cnt = jnp.concatenate(cnts, axis=0)
    return (cnt < N_SEL) & (score_t > NEG / 2)


def _nsa_kernel(q0_ref, q1_ref, q2_ref, q3_ref, sm_ref, ck_ref, cv_ref, ks_ref, vs_ref, kw_ref,
                vw_ref, o_ref, acc_ref, m_ref, l_ref, kvb_ref, s_ref):
    g = pl.program_id(1)
    qi = pl.program_id(2)
    TQ = q0_ref.shape[0]
    R4 = NSA_HPG * TQ
    T = ks_ref.shape[0]
    KT = min(NSA_KT, T)
    qs = qi * TQ

    @pl.when(qi == 0)
    def _():
        for i, r in enumerate((ks_ref, vs_ref, kw_ref, vw_ref)):
            kvb_ref[i] = r[...].astype(BF16)

    ksb_ref, vsb_ref, kwb_ref, vwb_ref = (kvb_ref.at[i] for i in range(4))
    q4 = jnp.concatenate([q0_ref[...], q1_ref[...], q2_ref[...], q3_ref[...]], axis=0)
    q4 = (q4 * (HEAD_DIM ** -0.5)).astype(BF16)
    t4 = qs + lax.rem(lax.broadcasted_iota(jnp.int32, (R4, 1), 0), TQ)
    t1 = qs + lax.broadcasted_iota(jnp.int32, (TQ, 1), 0)

    ck = ck_ref[0, 0, 0]
    nc = ck.shape[0]
    WK = min(WINDOW + TQ, T)
    w0 = pl.multiple_of(jnp.minimum(jnp.maximum(qs - WINDOW, 0), T - WK), TQ)
    s_cmp = _bdot_nt(q4, ck)
    s_win = _bdot_nt(q4, kwb_ref[pl.ds(w0, WK), :])
    s_ref[...] = _bdot_nt(q4, ksb_ref[pl.ds(0, KT), :])

    cmp_end = CMP_STRIDE * lax.broadcasted_iota(jnp.int32, (1, nc), 1) + (CMP_LEN - 1)
    p = _masked_softmax(s_cmp, cmp_end <= t4)
    o_cmp = _bdot(p, cv_ref[0, 0, 0])

    n_slc = T // SEL_BLOCK
    nsp = max(n_slc, LANE)
    psum = p[0:TQ]
    for j in range(1, NSA_HPG):
        psum = psum + p[j * TQ:(j + 1) * TQ]
    ci = CMP_STRIDE * lax.broadcasted_iota(jnp.int32, (nc, nsp), 0)
    bj = SEL_BLOCK * lax.broadcasted_iota(jnp.int32, (nc, nsp), 1)
    cmap = ((ci < bj + SEL_BLOCK) & (ci + CMP_LEN > bj)).astype(F32)
    imp = jnp.dot(psum, cmap, precision=HIGHEST, preferred_element_type=F32)
    imp_t = imp.T[:n_slc]
    j_idx = lax.broadcasted_iota(jnp.int32, (n_slc, TQ), 0)
    jt = lax.shift_right_logical(qs + lax.broadcasted_iota(jnp.int32, (1, TQ), 1),
                                 int(math.log2(SEL_BLOCK)))
    force = (j_idx == 0) | (j_idx == jt) | (j_idx == jt - 1)
    score_t = jnp.where(j_idx > jt, NEG, jnp.where(force, FORCE, imp_t))
    sel_t = jnp.where(_select_blocks_t(score_t), 1.0, 0.0)

    wpos = w0 + lax.broadcasted_iota(jnp.int32, (1, WK), 1)
    p = _masked_softmax(s_win, (wpos <= t4) & (wpos > t4 - WINDOW))
    o_win = _bdot(p, vwb_ref[pl.ds(w0, WK), :])

    acc_ref[...] = jnp.zeros_like(acc_ref)
    m_ref[...] = jnp.full_like(m_ref, NEG)
    l_ref[...] = jnp.zeros_like(l_ref)
    n_kt = (qs + TQ - 1) // KT + 1

    def slc_body(kt, carry):
        k0 = pl.multiple_of(kt * KT, KT)
        vb = vsb_ref[pl.ds(k0, KT), :]
        s = s_ref[...]
        k1 = pl.multiple_of(jnp.minimum(kt + 1, T // KT - 1) * KT, KT)
        s_next = _bdot_nt(q4, ksb_ref[pl.ds(k1, KT), :])
        kpos = k0 + lax.broadcasted_iota(jnp.int32, (1, KT), 1)
        eb = lax.broadcasted_iota(jnp.int32, (n_slc, KT), 0) == lax.shift_right_logical(
            k0 + lax.broadcasted_iota(jnp.int32, (n_slc, KT), 1), int(math.log2(SEL_BLOCK)))
        selx = _bdot_tn(sel_t, jnp.where(eb, 1.0, 0.0))
        bias = jnp.where((selx > 0.5) & (kpos <= t1), 0.0, NEG)
        s = s + jnp.concatenate([bias] * NSA_HPG, axis=0)
        m_old = m_ref[...]
        m_new = jnp.maximum(m_old, jnp.max(s, axis=-1, keepdims=True))
        pe = jnp.exp(s - m_new)
        alpha = jnp.exp(m_old - m_new)
        l_ref[...] = alpha * l_ref[...] + jnp.sum(pe, axis=-1, keepdims=True)
        acc_ref[...] = alpha * acc_ref[...] + _bdot(pe, vb)
        m_ref[...] = m_new
        s_ref[...] = s_next
        return carry

    lax.fori_loop(0, n_kt, slc_body, 0)
    o_slc = acc_ref[...] / jnp.maximum(l_ref[...], 1e-30)

    gates = _sigmoid(sm_ref[...])
    for j in range(NSA_HPG):
        lane0 = SM_GATE + (g * NSA_HPG + j) * 3
        rows = slice(j * TQ, (j + 1) * TQ)
        ob = (o_cmp[rows] * _lane_pick(gates, lane0) + o_slc[rows] * _lane_pick(gates, lane0 + 1)
              + o_win[rows] * _lane_pick(gates, lane0 + 2))
        o_ref[:, j * HEAD_DIM:(j + 1) * HEAD_DIM] = ob.astype(o_ref.dtype)


def _nsa_prompt(proj, small, comp, batch, seq):
    TQ = min(NSA_TQ, seq)
    nq = seq // TQ
    nch = seq // CMP_STRIDE
    G = NSA_KV_HEADS

    def qspec(j):
        return pl.BlockSpec((TQ, LANE), lambda b, g, i, j=j: (b * nq + i, CB_QB + g * NSA_HPG + j))

    def kvspec(e):
        return pl.BlockSpec((seq, LANE), lambda b, g, i, e=e: (b, CB_KVB + e * G + g))

    def cspec(e):
        return pl.BlockSpec((1, 1, 1, nch, HEAD_DIM), lambda b, g, i, e=e: (b, e, g, 0, 0))

    R4 = NSA_HPG * TQ
    return pl.pallas_call(
        _nsa_kernel,
        out_shape=jax.ShapeDtypeStruct((batch * seq, NSA_HEADS * HEAD_DIM), BF16),
        grid=(batch, G, nq),
        in_specs=[qspec(0), qspec(1), qspec(2), qspec(3),
                  pl.BlockSpec((TQ, LANE), lambda b, g, i: (b * nq + i, 0)),
                  cspec(0), cspec(1), kvspec(2), kvspec(3), kvspec(4), kvspec(5)],
        out_specs=pl.BlockSpec((TQ, NSA_HPG * HEAD_DIM), lambda b, g, i: (b * nq + i, g)),
        scratch_shapes=[pltpu.VMEM((R4, HEAD_DIM), F32), pltpu.VMEM((R4, 1), F32),
                        pltpu.VMEM((R4, 1), F32), pltpu.VMEM((4, seq, HEAD_DIM), BF16),
                        pltpu.VMEM((R4, min(NSA_KT, seq)), F32)],
        compiler_params=_cp(("parallel", "parallel", "arbitrary")),
        name="nsa_prompt",
    )(proj, proj, proj, proj, small, comp, comp, proj, proj, proj, proj)


def _dot(a, b, exact):
    if exact:
        return jnp.dot(a.astype(F32), b.astype(F32), precision=HIGHEST, preferred_element_type=F32)
    return jnp.dot(a.astype(BF16), b.astype(BF16), preferred_element_type=F32)


def _merge_kernel(oa_ref, ob_ref, wa_ref, wb_ref, m0_ref, m1_ref, y_ref, *, exact):
    ya = _dot(oa_ref[...], wa_ref[...], exact)
    yb = _dot(ob_ref[...], wb_ref[...], exact)
    y = _sigmoid(m0_ref[...]) * ya + _sigmoid(m1_ref[...]) * yb
    y_ref[...] = y.astype(y_ref.dtype)


def _merge(o_a, o_b, w_a, w_b, gates, m0_cb, m1_cb, exact):
    n = o_a.shape[0]
    tm = min(1024, n)
    tn = 512
    cpb = tn // LANE
    return pl.pallas_call(
        functools.partial(_merge_kernel, exact=exact),
        out_shape=jax.ShapeDtypeStruct((n, D_MODEL), F32 if exact else BF16),
        grid=(n // tm, D_MODEL // tn),
        in_specs=[pl.BlockSpec((tm, D_MODEL), lambda i, j: (i, 0)),
                  pl.BlockSpec((tm, D_MODEL), lambda i, j: (i, 0)),
                  pl.BlockSpec((D_MODEL, tn), lambda i, j: (0, j)),
                  pl.BlockSpec((D_MODEL, tn), lambda i, j: (0, j)),
                  pl.BlockSpec((tm, tn), lambda i, j: (i, m0_cb // cpb + j)),
                  pl.BlockSpec((tm, tn), lambda i, j: (i, m1_cb // cpb + j))],
        out_specs=pl.BlockSpec((tm, tn), lambda i, j: (i, j)),
        compiler_params=_cp(("parallel", "parallel")),
        name="merge_exact" if exact else "merge",
    )(o_a, o_b, w_a, w_b, gates, gates)


def _top4(logits):
    lane = lax.broadcasted_iota(jnp.int32, logits.shape, 1)
    ids = jnp.full(logits.shape, -1, jnp.int32)
    vals = jnp.full(logits.shape, NEG, F32)
    cur = logits
    for k in range(TOP_K):
        m = jnp.max(cur, axis=1, keepdims=True)
        idx = jnp.min(jnp.where(cur == m, lane, LANE), axis=1, keepdims=True)
        ids = jnp.where(lane == k, idx, ids)
        vals = jnp.where(lane == k, m, vals)
        cur = jnp.where(lane == idx, 2 * NEG, cur)
    e = jnp.where(lane < TOP_K, jnp.exp(vals - jnp.max(vals, axis=1, keepdims=True)), 0.0)
    return ids, e / jnp.sum(e, axis=1, keepdims=True)


def _outproj_kernel(x_ref, y_ref, w_ref, gn_ref, wr_ref, br_ref, h_ref, hn_ref, ids_ref, gate_ref,
                    *, exact):
    h = x_ref[...] + _dot(y_ref[...], w_ref[...], exact)
    h_ref[...] = h
    hn = h * lax.rsqrt(jnp.mean(h * h, axis=-1, keepdims=True) + EPS) * gn_ref[...]
    hn_ref[...] = hn
    logits = jnp.dot(hn, wr_ref[...], precision=HIGHEST, preferred_element_type=F32) + br_ref[...]
    lane = lax.broadcasted_iota(jnp.int32, logits.shape, 1)
    ids, gate = _top4(jnp.where(lane < N_EXPERTS, logits, NEG))
    ids_ref[...] = ids
    gate_ref[...] = gate


def _outproj_router(x, y, w_out, norm_g, w_router, b_router, exact):
    n = x.shape[0]
    tm = min(256, n)
    row = lambda i: (i, 0)
    fixed = lambda i: (0, 0)
    return pl.pallas_call(
        functools.partial(_outproj_kernel, exact=exact),
        out_shape=(jax.ShapeDtypeStruct((n, D_MODEL), F32), jax.ShapeDtypeStruct((n, D_MODEL), F32),
                   jax.ShapeDtypeStruct((n, LANE), jnp.int32), jax.ShapeDtypeStruct((n, LANE), F32)),
        grid=(n // tm,),
        in_specs=[pl.BlockSpec((tm, D_MODEL), row), pl.BlockSpec((tm, D_MODEL), row),
                  pl.BlockSpec((D_MODEL, D_MODEL), fixed), pl.BlockSpec((1, D_MODEL), fixed),
                  pl.BlockSpec((D_MODEL, LANE), fixed), pl.BlockSpec((1, LANE), fixed)],
        out_specs=(pl.BlockSpec((tm, D_MODEL), row), pl.BlockSpec((tm, D_MODEL), row),
                   pl.BlockSpec((tm, LANE), row), pl.BlockSpec((tm, LANE), row)),
        compiler_params=_cp(("parallel",)),
        name="outproj_router_exact" if exact else "outproj_router",
    )(x, y, w_out, norm_g.reshape(1, D_MODEL), w_router, b_router)


def _rank_kernel(ids_ref, rank_ref, cnt_ref, carry_ref):
    i = pl.program_id(0)

    @pl.when(i == 0)
    def _():
        carry_ref[...] = jnp.zeros_like(carry_ref)

    ids = ids_ref[...]
    tm = ids.shape[0]
    lane = lax.broadcasted_iota(jnp.int32, ids.shape, 1)
    cols = [jnp.sum(jnp.where(lane == k, ids, 0), axis=1, keepdims=True) for k in range(TOP_K)]
    mask = jnp.zeros(ids.shape, F32)
    for k in range(TOP_K):
        mask = mask + (lane == cols[k]).astype(F32)
    r = lax.broadcasted_iota(jnp.int32, (tm, tm), 0)
    c = lax.broadcasted_iota(jnp.int32, (tm, tm), 1)
    before = jnp.dot((r > c).astype(BF16), mask.astype(BF16), preferred_element_type=F32)
    pos = before + carry_ref[0:1]
    rank = jnp.zeros(ids.shape, F32)
    for k in range(TOP_K):
        rk = jnp.sum(jnp.where(lane == cols[k], pos, 0.0), axis=1, keepdims=True)
        rank = jnp.where(lane == k, rk, rank)
    rank_ref[...] = rank.astype(jnp.int32)
    total = carry_ref[0:1] + jnp.sum(mask, axis=0, keepdims=True)
    carry_ref[...] = jnp.broadcast_to(total, carry_ref.shape)
    cnt_ref[...] = jnp.broadcast_to(total, cnt_ref.shape).astype(jnp.int32)


def _expert_ranks(ids):
    n = ids.shape[0]
    tm = min(256, n)
    return pl.pallas_call(
        _rank_kernel,
        out_shape=(jax.ShapeDtypeStruct((n, LANE), jnp.int32), jax.ShapeDtypeStruct((8, LANE), jnp.int32)),
        grid=(n // tm,),
        in_specs=[pl.BlockSpec((tm, LANE), lambda i: (i, 0))],
        out_specs=(pl.BlockSpec((tm, LANE), lambda i: (i, 0)), pl.BlockSpec((8, LANE), lambda i: (0, 0))),
        scratch_shapes=[pltpu.VMEM((8, LANE), F32)],
        compiler_params=_cp(("arbitrary",)),
        name="expert_ranks",
    )(ids)


def _scatter_kernel(slot_ref, x_ref, xs_in_ref, xs_ref, sem):
    del xs_in_ref
    tm = x_ref.shape[0]

    def copy(r, k):
        return pltpu.make_async_copy(x_ref.at[pl.ds(r, 1)],
                                     xs_ref.at[pl.ds(slot_ref[0, 0, r * TOP_K + k], 1)], sem)

    def start(r, c):
        for k in range(TOP_K):
            copy(r, k).start(priority=k % 2)
        return c

    def wait(r, c):
        for k in range(TOP_K):
            copy(r, k).wait()
        return c

    lax.fori_loop(0, tm, start, 0)
    lax.fori_loop(0, tm, wait, 0)


def _scatter_rows(x, slots, xs):
    n = x.shape[0]
    tm = min(256, n)
    nt = n // tm
    return pl.pallas_call(
        _scatter_kernel,
        out_shape=jax.ShapeDtypeStruct(xs.shape, xs.dtype),
        grid=(nt,),
        in_specs=[pl.BlockSpec((1, 1, tm * TOP_K), lambda i: (i, 0, 0), memory_space=pltpu.SMEM),
                  pl.BlockSpec((tm, D_MODEL), lambda i: (i, 0)),
                  pl.BlockSpec(memory_space=pl.ANY)],
        out_specs=pl.BlockSpec(memory_space=pl.ANY),
        scratch_shapes=[pltpu.SemaphoreType.DMA(())],
        input_output_aliases={2: 0},
        compiler_params=_cp(("arbitrary",)),
        name="moe_scatter",
    )(slots.reshape(nt, 1, tm * TOP_K), x, xs)


def _stream_expert_weights(be_ref, nu_ref, ne_ref, nj_ref, cnt_ref, tile_copies, consume):
    j = pl.program_id(0)
    b = pl.program_id(1)
    step = j * pl.num_programs(1) + b
    used = b < nu_ref[0]
    e = be_ref[b]
    first = used & ((b == 0) | (e != be_ref[jnp.maximum(b - 1, 0)]))

    @pl.when(step == 0)
    def _():
        cnt_ref[0] = 0
        for c in tile_copies(e, j, 0):
            c.start()

    @pl.when(first)
    def _():
        slot = lax.rem(cnt_ref[0], 2)
        for c in tile_copies(e, j, slot):
            c.wait()
        consume(slot)
        ne = ne_ref[step]

        @pl.when(ne >= 0)
        def _():
            for c in tile_copies(ne, nj_ref[step], 1 - slot):
                c.start()

        cnt_ref[0] = cnt_ref[0] + 1

    return used


def _next_tiles(blk_e, n_used, n_col_tiles):
    nb = blk_e.shape[0]
    used = jnp.arange(nb) < n_used[0]
    prev = jnp.concatenate([jnp.full((1,), -1, blk_e.dtype), blk_e[:-1]])
    first = jnp.tile(used & (blk_e != prev), n_col_tiles)
    n = n_col_tiles * nb
    key = jnp.where(first, jnp.arange(n), n)
    nxt = jnp.concatenate([lax.cummin(key, axis=0, reverse=True)[1:], jnp.full((1,), n, key.dtype)])
    ok = nxt < n
    nxt = jnp.minimum(nxt, n - 1)
    return (jnp.where(ok, blk_e[nxt % nb], -1).astype(jnp.int32),
            jnp.where(ok, nxt // nb, 0).astype(jnp.int32))


def _moe_up_kernel(be_ref, nu_ref, ne_ref, nj_ref, x_ref, w_hbm, bg_ref, bu_ref, o_ref,
                   wbuf_ref, wgb_ref, wub_ref, cnt_ref, sem):
    tf = wgb_ref.shape[1]
    nf = pl.num_programs(0)

    def tile_copies(e, j, slot):
        return [pltpu.make_async_copy(
            w_hbm.at[e, :, pl.ds(pl.multiple_of((p * nf + j) * tf, tf), tf)],
            wbuf_ref.at[slot, p], sem.at[slot]) for p in range(2)]

    def consume(slot):
        wgb_ref[...] = wbuf_ref[slot, 0].astype(BF16)
        wub_ref[...] = wbuf_ref[slot, 1].astype(BF16)

    used = _stream_expert_weights(be_ref, nu_ref, ne_ref, nj_ref, cnt_ref, tile_copies, consume)

    @pl.when(used)
    def _():
        x = x_ref[...].astype(BF16)
        g = jnp.dot(x, wgb_ref[...], preferred_element_type=F32) + bg_ref[0]
        u = jnp.dot(x, wub_ref[...], preferred_element_type=F32) + bu_ref[0]
        g = jnp.minimum(g, SWIGLU_LIMIT)
        u = jnp.clip(u, -SWIGLU_LIMIT, SWIGLU_LIMIT)
        o_ref[...] = ((u + 1.0) * g * _sigmoid(SWIGLU_ALPHA * g)).astype(o_ref.dtype)

    @pl.when(jnp.logical_not(used))
    def _():
        o_ref[...] = jnp.zeros_like(o_ref)


def _moe_up(xs, blk_e, n_used, w_gu, b_gu, tf=1024):
    n_blocks = xs.shape[0] // MOE_BLK
    nf = D_FF // tf
    next_e, next_j = _next_tiles(blk_e, n_used, nf)
    grid_spec = pltpu.PrefetchScalarGridSpec(
        num_scalar_prefetch=4,
        grid=(nf, n_blocks),
        in_specs=[pl.BlockSpec((MOE_BLK, D_MODEL), lambda j, b, be, *_: (b, 0)),
                  pl.BlockSpec(memory_space=pl.ANY),
                  pl.BlockSpec((1, 1, tf), lambda j, b, be, *_: (be[b], 0, j)),
                  pl.BlockSpec((1, 1, tf), lambda j, b, be, *_: (be[b], 0, nf + j))],
        out_specs=pl.BlockSpec((MOE_BLK, tf), lambda j, b, be, *_: (b, j)),
        scratch_shapes=[pltpu.VMEM((2, 2, D_MODEL, tf), F32), pltpu.VMEM((D_MODEL, tf), BF16),
                        pltpu.VMEM((D_MODEL, tf), BF16), pltpu.SMEM((1,), jnp.int32),
                        pltpu.SemaphoreType.DMA((2,))])
    return pl.pallas_call(
        _moe_up_kernel,
        out_shape=jax.ShapeDtypeStruct((xs.shape[0], D_FF), BF16),
        grid_spec=grid_spec,
        compiler_params=_cp(("arbitrary", "arbitrary")),
        name="moe_up",
    )(blk_e, n_used, next_e, next_j, xs, w_gu, b_gu.reshape(N_EXPERTS, 1, 2 * D_FF),
      b_gu.reshape(N_EXPERTS, 1, 2 * D_FF))


def _moe_down_kernel(be_ref, nu_ref, ne_ref, nj_ref, h_ref, w_hbm, b_ref, o_ref,
                     wbuf_ref, wb_ref, cnt_ref, sem):
    tn = wb_ref.shape[1]

    def tile_copies(e, j, slot):
        return [pltpu.make_async_copy(w_hbm.at[e, :, pl.ds(pl.multiple_of(j * tn, tn), tn)],
                                      wbuf_ref.at[slot], sem.at[slot])]

    def consume(slot):
        wb_ref[...] = wbuf_ref[slot].astype(BF16)

    used = _stream_expert_weights(be_ref, nu_ref, ne_ref, nj_ref, cnt_ref, tile_copies, consume)

    @pl.when(used)
    def _():
        o_ref[...] = jnp.dot(h_ref[...], wb_ref[...], preferred_element_type=F32) + b_ref[0]

    @pl.when(jnp.logical_not(used))
    def _():
        o_ref[...] = jnp.zeros_like(o_ref)


def _moe_down(hid, blk_e, n_used, w_down, b_down, tn=D_MODEL):
    n_blocks = hid.shape[0] // MOE_BLK
    nn = D_MODEL // tn
    next_e, next_j = _next_tiles(blk_e, n_used, nn)
    grid_spec = pltpu.PrefetchScalarGridSpec(
        num_scalar_prefetch=4,
        grid=(nn, n_blocks),
        in_specs=[pl.BlockSpec((MOE_BLK, D_FF), lambda j, b, be, *_: (b, 0)),
                  pl.BlockSpec(memory_space=pl.ANY),
                  pl.BlockSpec((1, 1, tn), lambda j, b, be, *_: (be[b], 0, j))],
        out_specs=pl.BlockSpec((MOE_BLK, tn), lambda j, b, be, *_: (b, j)),
        scratch_shapes=[pltpu.VMEM((2, D_FF, tn), F32), pltpu.VMEM((D_FF, tn), BF16),
                        pltpu.SMEM((1,), jnp.int32), pltpu.SemaphoreType.DMA((2,))])
    return pl.pallas_call(
        _moe_down_kernel,
        out_shape=jax.ShapeDtypeStruct((hid.shape[0], D_MODEL), F32),
        grid_spec=grid_spec,
        compiler_params=_cp(("arbitrary", "arbitrary")),
        name="moe_down",
    )(blk_e, n_used, next_e, next_j, hid, w_down, b_down.reshape(N_EXPERTS, 1, D_MODEL))


def _combine_kernel(slot_ref, h_ref, gate_ref, gn_ref, y_ref, o_ref, buf_ref, sem):
    tm = h_ref.shape[0]

    def copy(r, k):
        return pltpu.make_async_copy(y_ref.at[pl.ds(slot_ref[0, 0, r * TOP_K + k], 1)],
                                     buf_ref.at[k, pl.ds(r, 1)], sem)

    def start(r, c):
        for k in range(TOP_K):
            copy(r, k).start(priority=k % 2)
        return c

    def wait(r, c):
        for k in range(TOP_K):
            copy(r, k).wait()
        return c

    lax.fori_loop(0, tm, start, 0)
    lax.fori_loop(0, tm, wait, 0)
    gate = gate_ref[...]
    acc = h_ref[...]
    for k in range(TOP_K):
        acc = acc + gate[:, k:k + 1] * buf_ref[k]
    o_ref[...] = acc * lax.rsqrt(jnp.mean(acc * acc, axis=-1, keepdims=True) + EPS) * gn_ref[...]


def _combine(h, gate, slots, y, norm_g):
    n = h.shape[0]
    tm = min(256, n)
    nt = n // tm
    return pl.pallas_call(
        _combine_kernel,
        out_shape=jax.ShapeDtypeStruct((n, D_MODEL), F32),
        grid=(nt,),
        in_specs=[pl.BlockSpec((1, 1, tm * TOP_K), lambda i: (i, 0, 0), memory_space=pltpu.SMEM),
                  pl.BlockSpec((tm, D_MODEL), lambda i: (i, 0)),
                  pl.BlockSpec((tm, LANE), lambda i: (i, 0)),
                  pl.BlockSpec((1, D_MODEL), lambda i: (0, 0)),
                  pl.BlockSpec(memory_space=pl.ANY)],
        out_specs=pl.BlockSpec((tm, D_MODEL), lambda i: (i, 0)),
        scratch_shapes=[pltpu.VMEM((TOP_K, tm, D_MODEL), F32), pltpu.SemaphoreType.DMA(())],
        compiler_params=_cp(("arbitrary",)),
        name="moe_combine",
    )(slots.reshape(nt, 1, tm * TOP_K), h, gate, norm_g.reshape(1, D_MODEL), y)


def _moe_plan(ids_p, ids_s):
    n_p, n_s = ids_p.shape[0], ids_s.shape[0]
    pad = (-n_s) % 256
    ids = jnp.concatenate([ids_p, ids_s, jnp.full((pad, LANE), -1, jnp.int32)], axis=0)
    rank, cnt = _expert_ranks(ids)
    counts = cnt[0, :N_EXPERTS]
    padded = (counts + MOE_BLK - 1) // MOE_BLK * MOE_BLK
    pad_end = jnp.cumsum(padded)
    pad_start = pad_end - padded
    n_pairs = (n_p + n_s) * TOP_K
    n_blocks = -(-n_pairs // MOE_BLK) + N_EXPERTS
    blk_start = jnp.arange(n_blocks, dtype=jnp.int32) * MOE_BLK
    blk_e = jnp.minimum(jnp.sum((pad_end[None, :] <= blk_start[:, None]).astype(jnp.int32), axis=1),
                        N_EXPERTS - 1)
    n_used = (pad_end[-1:] // MOE_BLK).astype(jnp.int32)
    e = ids[:n_p + n_s, :TOP_K]
    slot = (pad_start[e] + rank[:n_p + n_s, :TOP_K]).astype(jnp.int32)
    return slot[:n_p], slot[n_p:], blk_e, n_used, n_blocks


def _moe(hn_p, hn_s, h_p, h_s, ids_p, gate_p, ids_s, gate_s, w_gu, b_gu, w_down, b_down, norm_final):
    slot_p, slot_s, blk_e, n_used, n_blocks = _moe_plan(ids_p, ids_s)
    xs = jnp.zeros((n_blocks * MOE_BLK, D_MODEL), F32)
    xs = _scatter_rows(hn_p, slot_p, xs)
    xs = _scatter_rows(hn_s, slot_s, xs)
    hid = _moe_up(xs, blk_e, n_used, w_gu, b_gu)
    y = _moe_down(hid, blk_e, n_used, w_down, b_down)
    return (_combine(h_p, gate_p, slot_p, y, norm_final),
            _combine(h_s, gate_s, slot_s, y, norm_final))


def _proj_sample_kernel(x_ref, g_ref, w_ref, o_ref):
    x = x_ref[...]
    xn = x * lax.rsqrt(jnp.mean(x * x, axis=-1, keepdims=True) + EPS) * g_ref[...]
    o_ref[...] = lax.dot_general(xn, w_ref[...], (((1,), (1,)), ((), ())), precision=HIGHEST,
                                 preferred_element_type=F32)


def _proj_sample(x, norm_g, w_t):
    n = x.shape[0]
    tn = 1024
    return pl.pallas_call(
        _proj_sample_kernel,
        out_shape=jax.ShapeDtypeStruct((n, N_IN), F32),
        grid=(pl.cdiv(N_IN, tn),),
        in_specs=[pl.BlockSpec((n, D_MODEL), lambda j: (0, 0)),
                  pl.BlockSpec((1, D_MODEL), lambda j: (0, 0)),
                  pl.BlockSpec((tn, D_MODEL), lambda j: (j, 0))],
        out_specs=pl.BlockSpec((n, tn), lambda j: (0, j)),
        compiler_params=_cp(("parallel",)),
        name="proj_sample",
    )(x, norm_g.reshape(1, D_MODEL), w_t)


def _gdn_sample_kernel(ext_ref, cw_ref, z_ref, br_ref, ar_ref, al_ref, dt_ref, ng_ref, s_ref,
                       o_ref, so_ref):
    H = GDN_HEADS
    y = ext_ref[0, 0] * cw_ref[0]
    for w in range(1, CONV_W):
        y = y + ext_ref[0, w] * cw_ref[w]
    y = _silu(y)
    q = _l2norm(y[0:H]) * (GDN_DK ** -0.5)
    k = _l2norm(y[H:2 * H])
    v = y[2 * H:3 * H]
    beta = _sigmoid(br_ref[0])
    eg = jnp.exp(-jnp.exp(al_ref[...]) * _softplus(ar_ref[0] + dt_ref[...]))
    qk = jnp.sum(q * k, axis=-1, keepdims=True)
    qT = q.T
    kT = k.T
    rows = []
    for h in range(H):
        S = s_ref[0, h]
        kc = kT[:, h:h + 1]
        qc = qT[:, h:h + 1]
        e = eg[h:h + 1]
        k_s = jnp.sum(S * kc, axis=0, keepdims=True)
        q_s = jnp.sum(S * qc, axis=0, keepdims=True)
        v_new = beta[h:h + 1] * (v[h:h + 1] - e * k_s)
        rows.append(e * q_s + qk[h:h + 1] * v_new)
        so_ref[0, h] = e * S + kc * v_new
    o = jnp.concatenate(rows, axis=0)
    on = o * lax.rsqrt(jnp.mean(o * o, axis=-1, keepdims=True) + EPS) * ng_ref[...]
    o_ref[0] = on * _silu(z_ref[0])


def _gdn_sample(ext, conv_w, z, b_raw, a_raw, a_log, dt_bias, norm_g, state):
    bsz = ext.shape[0]
    H = GDN_HEADS
    col = lambda a: a.reshape(bsz, H, 1)
    hcol = lambda a: a.astype(F32).reshape(H, 1)
    return pl.pallas_call(
        _gdn_sample_kernel,
        out_shape=(jax.ShapeDtypeStruct((bsz, H, GDN_DV), F32),
                   jax.ShapeDtypeStruct(state.shape, F32)),
        grid=(bsz,),
        in_specs=[pl.BlockSpec((1, CONV_W, 3 * H, LANE), lambda b: (b, 0, 0, 0)),
                  pl.BlockSpec((CONV_W, 3 * H, LANE), lambda b: (0, 0, 0)),
                  pl.BlockSpec((1, H, GDN_DV), lambda b: (b, 0, 0)),
                  pl.BlockSpec((1, H, 1), lambda b: (b, 0, 0)),
                  pl.BlockSpec((1, H, 1), lambda b: (b, 0, 0)),
                  pl.BlockSpec((H, 1), lambda b: (0, 0)),
                  pl.BlockSpec((H, 1), lambda b: (0, 0)),
                  pl.BlockSpec((1, GDN_DV), lambda b: (0, 0)),
                  pl.BlockSpec((1, H, GDN_DK, GDN_DV), lambda b: (b, 0, 0, 0))],
        out_specs=(pl.BlockSpec((1, H, GDN_DV), lambda b: (b, 0, 0)),
                   pl.BlockSpec((1, H, GDN_DK, GDN_DV), lambda b: (b, 0, 0, 0))),
        compiler_params=_cp(("parallel",)),
        name="gdn_sample",
    )(ext, conv_w.reshape(CONV_W, 3 * H, LANE), z.reshape(bsz, H, GDN_DV), col(b_raw), col(a_raw),
      hcol(a_log), hcol(dt_bias), norm_g.reshape(1, GDN_DV), state)


PAGES_PER_STEP = 8


def _page_sums_kernel(pt_ref, *refs):
    del pt_ref
    x_refs, w_ref, o_ref = refs[:-2], refs[-2], refs[-1]
    nch = PAGE_SIZE // CMP_STRIDE
    G = NSA_KV_HEADS
    for i, x_ref in enumerate(x_refs):
        for e in range(2):
            x4 = x_ref[0, 0, :, e].reshape(nch, CMP_STRIDE, G, HEAD_DIM)
            for half in range(2):
                w = w_ref[2 * e + half]
                acc = x4[:, 0] * w[0:1][None]
                for j in range(1, CMP_STRIDE):
                    acc = acc + x4[:, j] * w[j:j + 1][None]
                o_ref[0, 2 * e + half, i * nch:(i + 1) * nch] = acc


def _page_sums(kv_cache, layer, page_table, w4):
    bsz, n_pages = page_table.shape
    G = NSA_KV_HEADS
    nch = PAGE_SIZE // CMP_STRIDE
    pps = math.gcd(PAGES_PER_STEP, n_pages)

    def page(i):
        return pl.BlockSpec((1, 1, PAGE_SIZE, 2, G, HEAD_DIM),
                            lambda b, p, pt: (layer, pt[b * n_pages + p * pps + i], 0, 0, 0, 0))

    grid_spec = pltpu.PrefetchScalarGridSpec(
        num_scalar_prefetch=1,
        grid=(bsz, n_pages // pps),
        in_specs=[page(i) for i in range(pps)]
        + [pl.BlockSpec((4, CMP_STRIDE, HEAD_DIM), lambda b, p, pt: (0, 0, 0))],
        out_specs=pl.BlockSpec((1, 4, pps * nch, G, HEAD_DIM), lambda b, p, pt: (b, 0, p, 0, 0)))
    return pl.pallas_call(
        _page_sums_kernel,
        out_shape=jax.ShapeDtypeStruct((bsz, 4, n_pages * nch, G, HEAD_DIM), F32),
        grid_spec=grid_spec,
        compiler_params=_cp(("parallel", "arbitrary")),
        name="nsa_page_sums",
    )(page_table.reshape(-1), *([kv_cache] * pps), w4)


def _row_to_col(r):
    n = r.shape[1]
    eye = lax.broadcasted_iota(jnp.int32, (n, n), 0) == lax.broadcasted_iota(jnp.int32, (n, n), 1)
    return jnp.sum(jnp.where(eye, jnp.broadcast_to(r, (n, n)), 0), axis=1, keepdims=True)


def _hdot(a, b):
    return jnp.dot(a, b, precision=HIGHEST, preferred_element_type=F32)


def _hdot_nt(a, b):
    return lax.dot_general(a, b, (((1,), (1,)), ((), ())), precision=HIGHEST,
                           preferred_element_type=F32)


def _nsa_sample_select_kernel(sums_ref, new_ref, w_ref, proj_ref, q_ref, ocmp_ref, idx_ref, *, past):
    G = NSA_KV_HEADS
    nch = sums_ref.shape[2]
    row = lax.broadcasted_iota(jnp.int32, (nch, HEAD_DIM), 0)

    def pooled(e, g):
        new_s1 = new_ref[0, e, g:g + 1, :] * w_ref[2 * e + 1, 0:1]
        s1 = sums_ref[0, 2 * e + 1, :, g, :]
        nxt = jnp.where(row < nch - 1, pltpu.roll(s1, nch - 1, 0), new_s1)
        return sums_ref[0, 2 * e, :, g, :] + nxt

    n_slc = past // SEL_BLOCK + 1
    nsp = -(-n_slc // LANE) * LANE
    t = past
    cmp_end = CMP_STRIDE * lax.broadcasted_iota(jnp.int32, (1, nch), 1) + (CMP_LEN - 1)
    ci = CMP_STRIDE * lax.broadcasted_iota(jnp.int32, (nch, nsp), 0)
    bj = SEL_BLOCK * lax.broadcasted_iota(jnp.int32, (nch, nsp), 1)
    cmap = ((ci < bj + SEL_BLOCK) & (ci + CMP_LEN > bj)).astype(F32)
    j_idx = lax.broadcasted_iota(jnp.int32, (1, nsp), 1)
    jt = t // SEL_BLOCK
    force = (j_idx == 0) | (j_idx == jt) | (j_idx == jt - 1)
    i_idx = lax.broadcasted_iota(jnp.int32, (nsp, nsp), 0)
    jj_idx = lax.broadcasted_iota(jnp.int32, (nsp, nsp), 1)
    lane = lax.broadcasted_iota(jnp.int32, (1, LANE), 1)
    idx_rows = []
    for g in range(G):
        q = q_ref[0, g] * (HEAD_DIM ** -0.5)
        p = _masked_softmax(_hdot_nt(_hdot_nt(q, proj_ref[0]), pooled(0, g)), cmp_end <= t)
        ocmp_ref[0, g] = _hdot(_hdot(p, pooled(1, g)), proj_ref[1])
        psum = jnp.sum(p[0:NSA_HPG], axis=0, keepdims=True)
        imp = _hdot(psum, cmap)
        score = jnp.where(j_idx > jt, NEG, jnp.where(force, FORCE, imp))
        sc = _row_to_col(score)
        ahead = (sc > score) | ((sc == score) & (i_idx < jj_idx))
        cnt = jnp.sum(ahead.astype(jnp.int32), axis=0, keepdims=True)
        sel = (cnt < N_SEL) & (score > NEG / 2)
        selc = _row_to_col(sel.astype(jnp.int32))
        before = jnp.sum(jnp.where(i_idx < jj_idx, selc, 0), axis=0, keepdims=True)
        out = jnp.full((1, LANE), -1, jnp.int32)
        for k in range(N_SEL):
            hit = sel & (before == k)
            jk = jnp.sum(jnp.where(hit, j_idx + 1, 0), axis=1, keepdims=True) - 1
            out = jnp.where(lane == k, jk, out)
        idx_rows.append(out)
    idx_ref[0] = jnp.concatenate(idx_rows + [jnp.full((8 - G, LANE), -1, jnp.int32)], axis=0)


def _nsa_sample_select(sums, new6, w4, cmp_proj, q8, past):
    bsz = sums.shape[0]
    G = NSA_KV_HEADS
    return pl.pallas_call(
        functools.partial(_nsa_sample_select_kernel, past=past),
        out_shape=(jax.ShapeDtypeStruct((bsz, G, 8, HEAD_DIM), F32),
                   jax.ShapeDtypeStruct((bsz, 8, LANE), jnp.int32)),
        grid=(bsz,),
        in_specs=[pl.BlockSpec((1,) + sums.shape[1:], lambda b: (b, 0, 0, 0, 0)),
                  pl.BlockSpec((1, 6, G, HEAD_DIM), lambda b: (b, 0, 0, 0)),
                  pl.BlockSpec((4, CMP_STRIDE, HEAD_DIM), lambda b: (0, 0, 0)),
                  pl.BlockSpec((2, HEAD_DIM, HEAD_DIM), lambda b: (0, 0, 0)),
                  pl.BlockSpec((1, G, 8, HEAD_DIM), lambda b: (b, 0, 0, 0))],
        out_specs=(pl.BlockSpec((1, G, 8, HEAD_DIM), lambda b: (b, 0, 0, 0)),
                   pl.BlockSpec((1, 8, LANE), lambda b: (b, 0, 0))),
        compiler_params=_cp(("parallel",)),
        name="nsa_sample_select",
    )(sums, new6, w4, cmp_proj, q8)


def _nsa_sample_attend_kernel(idx_ref, pt_ref, kv_hbm, win_hbm, new_ref, q_ref, ocmp_ref, gate_ref,
                              o_ref, kv_buf, win_buf, sem, *, past, n_pages, layer):
    b = pl.program_id(0)
    g = pl.program_id(1)
    G = NSA_KV_HEADS
    t = past
    n_past_blk = past // SEL_BLOCK
    bpp = PAGE_SIZE // SEL_BLOCK
    step = b * G + g
    n_steps = pl.num_programs(0) * G

    def copies(st, slot):
        bb = st // G
        gg = lax.rem(st, G)
        out = []
        for k in range(N_SEL):
            j = jnp.clip(idx_ref[st * N_SEL + k], 0, n_past_blk - 1)
            page = pt_ref[bb * n_pages + j // bpp]
            r0 = pl.multiple_of(lax.rem(j, bpp) * SEL_BLOCK, SEL_BLOCK)
            for kv in range(2):
                out.append(pltpu.make_async_copy(
                    kv_hbm.at[layer, page, pl.ds(r0, SEL_BLOCK), 2 + kv, gg, :],
                    kv_buf.at[slot, kv, pl.ds(k * SEL_BLOCK, SEL_BLOCK), :], sem.at[slot]))
        for kv in range(2):
            out.append(pltpu.make_async_copy(win_hbm.at[layer, bb, :, kv, gg, :],
                                             win_buf.at[slot, kv], sem.at[slot]))
        return out

    slot = lax.rem(step, 2)

    @pl.when(step == 0)
    def _():
        for c in copies(0, 0):
            c.start()

    @pl.when(step + 1 < n_steps)
    def _():
        for c in copies(step + 1, 1 - slot):
            c.start()

    for c in copies(step, slot):
        c.wait()

    q = q_ref[0, 0] * (HEAD_DIM ** -0.5)
    k_new = new_ref[0, 2, pl.ds(g, 1), :]
    v_new = new_ref[0, 3, pl.ds(g, 1), :]
    kw_new = new_ref[0, 4, pl.ds(g, 1), :]
    vw_new = new_ref[0, 5, pl.ds(g, 1), :]

    r64 = lax.broadcasted_iota(jnp.int32, (SEL_BLOCK, HEAD_DIM), 0)
    ks, vs, kpos = [], [], []
    r_lane = lax.broadcasted_iota(jnp.int32, (1, SEL_BLOCK), 1)
    for k in range(N_SEL):
        j = idx_ref[(b * G + g) * N_SEL + k]
        is_new = j >= n_past_blk
        rows = pl.ds(k * SEL_BLOCK, SEL_BLOCK)
        ks.append(jnp.where(is_new, jnp.where(r64 == 0, k_new, 0.0), kv_buf[slot, 0, rows, :]))
        vs.append(jnp.where(is_new, jnp.where(r64 == 0, v_new, 0.0), kv_buf[slot, 1, rows, :]))
        kpos.append(jnp.where(j >= 0, j * SEL_BLOCK + r_lane, t + 1))
    ksel = jnp.concatenate(ks, axis=0)
    vsel = jnp.concatenate(vs, axis=0)
    kpos = jnp.concatenate(kpos, axis=1)
    p = _masked_softmax(_hdot_nt(q, ksel), kpos <= t)
    o_slc = _hdot(p, vsel)

    L = win_buf.shape[2]
    wpos = (past - L) + lax.broadcasted_iota(jnp.int32, (1, L), 1)
    ok = (wpos > t - WINDOW) & (wpos >= 0)
    s_w = jnp.where(ok, _hdot_nt(q, win_buf[slot, 0]), NEG)
    s_n = jnp.sum(q * kw_new, axis=-1, keepdims=True)
    m = jnp.maximum(jnp.max(s_w, axis=-1, keepdims=True), s_n)
    p_w = jnp.exp(s_w - m) * ok.astype(F32)
    p_n = jnp.exp(s_n - m)
    den = jnp.maximum(jnp.sum(p_w, axis=-1, keepdims=True) + p_n, 1e-30)
    o_win = (_hdot(p_w, win_buf[slot, 1]) + p_n * vw_new) / den

    gb = _sigmoid(gate_ref[0, 0])
    o_ref[0, 0] = ocmp_ref[0, 0] * gb[:, 0:1] + o_slc * gb[:, 1:2] + o_win * gb[:, 2:3]


def _nsa_sample_attend(idx, page_table, kv_cache, win_cache, layer, new6, q8, o_cmp, gate8, past):
    bsz, n_pages = page_table.shape
    G = NSA_KV_HEADS
    L = win_cache.shape[2]
    per_bg = pl.BlockSpec((1, 1, 8, HEAD_DIM), lambda b, g, i, p: (b, g, 0, 0))
    grid_spec = pltpu.PrefetchScalarGridSpec(
        num_scalar_prefetch=2,
        grid=(bsz, G),
        in_specs=[pl.BlockSpec(memory_space=pl.ANY), pl.BlockSpec(memory_space=pl.ANY),
                  pl.BlockSpec((1, 6, G, HEAD_DIM), lambda b, g, i, p: (b, 0, 0, 0)),
                  per_bg, per_bg, per_bg],
        out_specs=per_bg,
        scratch_shapes=[pltpu.VMEM((2, 2, N_SEL * SEL_BLOCK, HEAD_DIM), F32),
                        pltpu.VMEM((2, 2, L, HEAD_DIM), F32), pltpu.SemaphoreType.DMA((2,))])
    return pl.pallas_call(
        functools.partial(_nsa_sample_attend_kernel, past=past, n_pages=n_pages, layer=layer),
        out_shape=jax.ShapeDtypeStruct((bsz, G, 8, HEAD_DIM), F32),
        grid_spec=grid_spec,
        compiler_params=_cp(("arbitrary", "arbitrary")),
        name="nsa_sample_attend",
    )(idx, page_table.reshape(-1), kv_cache, win_cache, new6, q8, o_cmp, gate8)


def _prompt_mixers(x2d, batch, seq, norm_mix, w_in, conv_w, a_log, dt_bias, gdn_norm, cmp_pos,
                   cmp_proj, w_out_a, w_out_b):
    w_main, w_small = _repack_w_in(w_in)
    proj, small = _proj_prompt(x2d, norm_mix, w_main, w_small)
    o_a, s_new = _gdn_prompt(proj, small, conv_w, _head_params(a_log, dt_bias), gdn_norm, batch, seq)
    comp = _compress_prompt(proj, cmp_pos, cmp_proj, batch, seq)
    o_b = _nsa_prompt(proj, small, comp, batch, seq)
    y = _merge(o_a, o_b, w_out_a.astype(BF16), w_out_b.astype(BF16), proj, CB_MERGE,
               CB_MERGE + D_MODEL // LANE, False)
    G = NSA_KV_HEADS
    kv6 = proj[:, CB_KVB * LANE:].reshape(batch, seq, 6, G, HEAD_DIM)
    conv_new = proj.reshape(batch, seq, N_MAIN)[:, seq - (CONV_W - 1):, :CONV_DIM]
    return y, kv6[:, :, :4], kv6[:, max(seq - WINDOW, 0):, 4:], s_new, conv_new


def _sample_mixers(x2d, norm_mix, w_in, conv_w, a_log, dt_bias, gdn_norm, cmp_pos, cmp_proj,
                   w_out_a, w_out_b, kv_cache, win_cache, layer, state, conv_buf, page_table):
    bsz = x2d.shape[0]
    win_buf = win_cache[layer]
    G = NSA_KV_HEADS
    past = page_table.shape[1] * PAGE_SIZE
    ps = _proj_sample(x2d, norm_mix, w_in)
    qkv, z = ps[:, OFF_QKV:OFF_Z], ps[:, OFF_Z:OFF_BRAW]
    b_raw, a_raw = ps[:, OFF_BRAW:OFF_ARAW], ps[:, OFF_ARAW:OFF_QB]
    q_b, kv_b = ps[:, OFF_QB:OFF_KVB], ps[:, OFF_KVB:OFF_GATEB]
    gate_b, merge_raw = ps[:, OFF_GATEB:OFF_MERGE], ps[:, OFF_MERGE:]

    ext = jnp.concatenate([conv_buf.astype(F32), qkv[:, None]], axis=1)
    o_a, s_new = _gdn_sample(ext.reshape(bsz, CONV_W, 3 * GDN_HEADS, LANE), conv_w, z, b_raw, a_raw,
                             a_log, dt_bias, gdn_norm, state.astype(F32))

    w4 = cmp_pos.astype(F32).reshape(4, CMP_STRIDE, HEAD_DIM)
    sums = _page_sums(kv_cache, layer, page_table, w4)
    new6 = kv_b.reshape(bsz, 6, G, HEAD_DIM)
    q8 = jnp.pad(q_b.reshape(bsz, G, NSA_HPG, HEAD_DIM), ((0, 0), (0, 0), (0, 8 - NSA_HPG), (0, 0)))
    o_cmp, idx8 = _nsa_sample_select(sums, new6, w4, cmp_proj.astype(F32), q8, past)
    gate8 = jnp.pad(gate_b.reshape(bsz, G, NSA_HPG, 3),
                    ((0, 0), (0, 0), (0, 8 - NSA_HPG), (0, LANE - 3)))
    o_b8 = _nsa_sample_attend(idx8[:, :G, :N_SEL].reshape(-1), page_table, kv_cache, win_cache,
                              layer, new6, q8, o_cmp, gate8, past)
    o_b = o_b8[:, :, :NSA_HPG].reshape(bsz, NSA_HEADS * HEAD_DIM)

    y = _merge(o_a.reshape(bsz, GDN_HEADS * GDN_DV), o_b, w_out_a, w_out_b, merge_raw, 0,
               D_MODEL // LANE, True)
    kv6 = kv_b.reshape(bsz, 1, 6, G, HEAD_DIM)
    win_ext = jnp.concatenate([win_buf.astype(F32), kv6[:, :, 4:]], axis=1)
    keep = min(WINDOW, past + 1)
    return y, kv6[:, :, :4], win_ext[:, win_ext.shape[1] - keep:], s_new, ext[:, 1:]


def kernel(x_prompt, x_sample, cache_nsa_kv, cache_nsa_win, state_gdn, state_conv, page_table, norm_mix, w_in, conv_w, gdn_a_log, gdn_dt_bias, gdn_norm, w_out_a, cmp_pos, cmp_proj, w_out_b, w_out, norm_ffn, w_router, b_router, w_gu, b_gu, w_down, b_down, norm_final):
    assert w_in.shape[0] == 1 and x_sample.shape[1] == 1, "one layer, one new token per sequence"
    bp, seq, _ = x_prompt.shape
    bs = x_sample.shape[0]
    xp = x_prompt.reshape(bp * seq, D_MODEL)
    xs = x_sample.reshape(bs, D_MODEL)
    l = 0
    w_t = jnp.swapaxes(w_in[l], 0, 1)
    y_p, kv_p, win_p, gdn_p, conv_p = _prompt_mixers(
        xp, bp, seq, norm_mix[l], w_t, conv_w[l], gdn_a_log[l], gdn_dt_bias[l], gdn_norm[l],
        cmp_pos[l], cmp_proj[l], w_out_a[l], w_out_b[l])
    y_s, kv_s, win_s, gdn_s, conv_s = _sample_mixers(
        xs, norm_mix[l], w_t, conv_w[l], gdn_a_log[l], gdn_dt_bias[l], gdn_norm[l], cmp_pos[l],
        cmp_proj[l], w_out_a[l], w_out_b[l], cache_nsa_kv, cache_nsa_win, l, state_gdn[l],
        state_conv[l], page_table)

    wr = jnp.pad(w_router[l].astype(F32), ((0, 0), (0, LANE - N_EXPERTS)))
    br = jnp.pad(b_router[l].astype(F32), (0, LANE - N_EXPERTS)).reshape(1, LANE)
    h_p, hn_p, ids_p, gate_p = _outproj_router(xp, y_p, w_out[l].astype(BF16), norm_ffn[l], wr, br, False)
    h_s, hn_s, ids_s, gate_s = _outproj_router(xs, y_s, w_out[l], norm_ffn[l], wr, br, True)
    out_p, out_s = _moe(hn_p, hn_s, h_p, h_s, ids_p, gate_p, ids_s, gate_s, w_gu[l], b_gu[l],
                        w_down[l], b_down[l], norm_final)
    return (out_p.reshape(x_prompt.shape), out_s.reshape(x_sample.shape),
            kv_p[None], kv_s[None], win_p[None], win_s[None],
            gdn_p[None].astype(state_gdn.dtype), gdn_s[None].astype(state_gdn.dtype),
            conv_p[None], conv_s[None])
```
